```python
import jax, jax.numpy as jnp
from jax import lax
import numpy as np

D_MODEL = 2048
BATCH = 2
SEQ = 4096
DEPTH = 2

GRID_W = 64
CTX_LEN = 256
FOURIER_GROUPS = 4
FOURIER_GROUP_W = 128
FOURIER_W = FOURIER_GROUPS * FOURIER_GROUP_W
MLA_HEADS = 8
Q_RANK = 512
KV_RANK = 256
QK_NOPE = 128
QK_ROPE = 64
QK_HEAD = QK_NOPE + QK_ROPE
V_HEAD = 128
ROPE_THETA = 10000.0
AXIS_PAIRS = QK_ROPE // 4
Q_BLOCK = 128
CONV_W = 512
CONV_K = 3
N_BRANCH = 3
D_FF = ((8 * D_MODEL // 3 + 255) // 256) * 256
N_MOD = 6
RMS_EPS = 1e-6

OFF_F = 0
OFF_CQ = OFF_F + FOURIER_W
OFF_CKV = OFF_CQ + Q_RANK
OFF_KR = OFF_CKV + KV_RANK
OFF_CX = OFF_KR + QK_ROPE
OFF_CB = OFF_CX + CONV_W
OFF_CC = OFF_CB + CONV_W
OFF_G = OFF_CC + CONV_W
N_IN = OFF_G + N_BRANCH * D_MODEL

kernel_name = "hybrid_fourier_mla_shortconv_dit"


def rms_norm(x, gain):
    xf = x.astype(jnp.float32)
    y = xf * lax.rsqrt(jnp.mean(jnp.square(xf), axis=-1, keepdims=True) + RMS_EPS)
    return (y * gain.astype(jnp.float32)).astype(x.dtype)


def modulate(h, shift, scale):
    return h * (1.0 + scale) + shift


def axial_rope_tables(rows, dtype):
    row = jnp.repeat(jnp.arange(rows), GRID_W)
    col = jnp.tile(jnp.arange(GRID_W), rows)
    inv_freq = ROPE_THETA ** (-jnp.arange(AXIS_PAIRS, dtype=jnp.float32) / AXIS_PAIRS)
    ang = jnp.concatenate([row[:, None] * inv_freq, col[:, None] * inv_freq], axis=-1)
    return jnp.cos(ang)[:, None, :].astype(dtype), jnp.sin(ang)[:, None, :].astype(dtype)


def apply_rope(x, cos, sin):
    x1, x2 = x[..., : QK_ROPE // 2], x[..., QK_ROPE // 2:]
    return jnp.concatenate([x1 * cos - x2 * sin, x1 * sin + x2 * cos], axis=-1)


def rope_tail(x, rope):
    if rope is None:
        return x
    cos, sin = rope
    return jnp.concatenate([x[..., :QK_NOPE], apply_rope(x[..., QK_NOPE:], cos, sin)], axis=-1)


def mla_queries(p_cq, lw, rope):
    B, T, _ = p_cq.shape
    cq = rms_norm(p_cq, lw["q_a_norm"])
    q = (cq @ lw["w_uq"]).reshape(B, T, MLA_HEADS, QK_HEAD)
    q = rms_norm(q, lw["q_norm"])
    return rope_tail(q, rope)


def mla_keys_values(p_ckv, p_kr, lw, rope):
    B, T, _ = p_ckv.shape
    ckv = rms_norm(p_ckv, lw["kv_a_norm"])
    kv = (ckv @ lw["w_ukv"]).reshape(B, T, MLA_HEADS, QK_NOPE + V_HEAD)
    k_nope, v = kv[..., :QK_NOPE], kv[..., QK_NOPE:]
    k_rope = jnp.broadcast_to(p_kr[:, :, None, :], (B, T, MLA_HEADS, QK_ROPE))
    k = rms_norm(jnp.concatenate([k_nope, k_rope], axis=-1), lw["k_norm"])
    return rope_tail(k, rope), v


def attention(q, k, v):
    B, T, H, Dh = q.shape
    nb = T // Q_BLOCK
    scale = QK_HEAD ** -0.5
    qb = q.reshape(B, nb, Q_BLOCK, H, Dh).transpose(1, 0, 2, 3, 4)

    def one_block(q_blk):
        s = jnp.einsum("bqhd,bkhd->bhqk", q_blk, k).astype(jnp.float32) * scale
        pr = jax.nn.softmax(s, axis=-1).astype(v.dtype)
        return jnp.einsum("bhqk,bkhd->bqhd", pr, v)

    out = lax.map(one_block, qb)
    return out.transpose(1, 0, 2, 3, 4).reshape(B, T, H * V_HEAD)


def fourier_mix(pf):
    B, T, _ = pf.shape
    f = pf.astype(jnp.float32).reshape(B, T, FOURIER_GROUPS, FOURIER_GROUP_W)
    f = jnp.fft.fft2(f, axes=(1, 3), norm="ortho").real
    return f.reshape(B, T, FOURIER_W).astype(pf.dtype)


def short_conv_mix(px, pb, pc, conv_w):
    u = pc * px
    up = jnp.pad(u, ((0, 0), (1, 1), (0, 0)))
    y = up[:, :-2] * conv_w[0] + up[:, 1:-1] * conv_w[1] + up[:, 2:] * conv_w[2]
    return pb * y


def merge_branches(p, attn, lw):
    B, T, _ = p.shape
    y_f = fourier_mix(p[..., OFF_F:OFF_CQ]) @ lw["w_f_out"]
    y_m = attn @ lw["w_mla_out"]
    y_c = short_conv_mix(p[..., OFF_CX:OFF_CB], p[..., OFF_CB:OFF_CC], p[..., OFF_CC:OFF_G],
                         lw["conv_w"]) @ lw["w_conv_out"]
    g = jax.nn.sigmoid(p[..., OFF_G:] + lw["b_gate"]).reshape(B, T, N_BRANCH, D_MODEL)
    merged = g[..., 0, :] * y_f + g[..., 1, :] * y_m + g[..., 2, :] * y_c
    return merged @ lw["w_out"]


def latent_mixer(p, rope, k_ctx, v_ctx, lw):
    q = mla_queries(p[..., OFF_CQ:OFF_CKV], lw, rope)
    k, v = mla_keys_values(p[..., OFF_CKV:OFF_KR], p[..., OFF_KR:OFF_CX], lw, rope)
    attn = attention(q, jnp.concatenate([k_ctx, k], axis=1), jnp.concatenate([v_ctx, v], axis=1))
    return merge_branches(p, attn, lw)


def context_mixer(pc, k_ctx, v_ctx, lw):
    q = mla_queries(pc[..., OFF_CQ:OFF_CKV], lw, None)
    return merge_branches(pc, attention(q, k_ctx, v_ctx), lw)


def swiglu(h, lw):
    return (jax.nn.silu(h @ lw["w_ffn_gate"]) * (h @ lw["w_ffn_up"])) @ lw["w_ffn_down"]


def setup_inputs(seed: int = 0) -> dict:
    key = jax.random.key(seed)
    ks = jax.random.split(key, 24)
    L, D = DEPTH, D_MODEL

    def nrm(k, shape):
        return jax.random.normal(k, shape, jnp.float32)

    def w(k, shape, fan_in, gain=1.0):
        return (gain * fan_in ** -0.5) * nrm(k, shape)

    def g(k, shape):
        return 1.0 + 0.01 * nrm(k, shape)

    def b(k, shape):
        return 0.02 * nrm(k, shape)

    return {
        "x": nrm(ks[0], (BATCH, SEQ, D)),
        "c": nrm(ks[1], (BATCH, D)),
        "ctx": nrm(ks[2], (BATCH, CTX_LEN, D)),
        "c_ctx": nrm(ks[3], (D,)),
        "w_ada": w(ks[4], (L, D, N_MOD * D), D, 0.5),
        "b_ada": b(ks[5], (L, N_MOD * D)),
        "norm_mix": g(ks[6], (L, D)),
        "norm_ffn": g(ks[7], (L, D)),
        "w_in": w(ks[8], (L, D, N_IN), D),
        "b_gate": b(ks[9], (L, N_BRANCH * D)),
        "q_a_norm": g(ks[10], (L, Q_RANK)),
        "kv_a_norm": g(ks[11], (L, KV_RANK)),
        "w_uq": w(ks[12], (L, Q_RANK, MLA_HEADS * QK_HEAD), Q_RANK),
        "w_ukv": w(ks[13], (L, KV_RANK, MLA_HEADS * (QK_NOPE + V_HEAD)), KV_RANK),
        "q_norm": g(ks[14], (L, QK_HEAD)),
        "k_norm": g(ks[15], (L, QK_HEAD)),
        "w_f_out": w(ks[16], (L, FOURIER_W, D), FOURIER_W),
        "w_mla_out": w(ks[17], (L, MLA_HEADS * V_HEAD, D), MLA_HEADS * V_HEAD),
        "conv_w": w(ks[18], (L, CONV_K, CONV_W), CONV_K),
        "w_conv_out": w(ks[19], (L, CONV_W, D), CONV_W),
        "w_out": w(ks[20], (L, D, D), D),
        "w_ffn_gate": w(ks[21], (L, D, D_FF), D),
        "w_ffn_up": w(ks[22], (L, D, D_FF), D),
        "w_ffn_down": w(ks[23], (L, D_FF, D), D_FF),
    }


def reference(x, c, ctx, c_ctx, w_ada, b_ada, norm_mix, norm_ffn, w_in, b_gate,
              q_a_norm, kv_a_norm, w_uq, w_ukv, q_norm, k_norm,
              w_f_out, w_mla_out, conv_w, w_conv_out, w_out,
              w_ffn_gate, w_ffn_up, w_ffn_down):
    B, S, D = x.shape
    rows = S // GRID_W
    rope = axial_rope_tables(rows, x.dtype)
    ada_lat = jax.nn.silu(c)
    ada_ctx = jax.nn.silu(c_ctx)
    xc = ctx
    for l in range(DEPTH):
        last = l == DEPTH - 1
        lw = {
            "w_in": w_in[l], "b_gate": b_gate[l],
            "q_a_norm": q_a_norm[l], "kv_a_norm": kv_a_norm[l],
            "w_uq": w_uq[l], "w_ukv": w_ukv[l], "q_norm": q_norm[l], "k_norm": k_norm[l],
            "w_f_out": w_f_out[l], "w_mla_out": w_mla_out[l],
            "conv_w": conv_w[l], "w_conv_out": w_conv_out[l], "w_out": w_out[l],
            "w_ffn_gate": w_ffn_gate[l], "w_ffn_up": w_ffn_up[l], "w_ffn_down": w_ffn_down[l],
        }
        mod = (ada_lat @ w_ada[l] + b_ada[l]).reshape(B, N_MOD, 1, D)
        modc = (ada_ctx @ w_ada[l] + b_ada[l]).reshape(N_MOD, D)

        h = modulate(rms_norm(x, norm_mix[l]), mod[:, 0], mod[:, 1])
        hc = modulate(rms_norm(xc, norm_mix[l]), modc[0], modc[1])
        p = h @ lw["w_in"]
        if last:
            pc = hc @ lw["w_in"][:, OFF_CKV:OFF_CX]
            pc_ckv, pc_kr = pc[..., :KV_RANK], pc[..., KV_RANK:]
        else:
            pc = hc @ lw["w_in"]
            pc_ckv, pc_kr = pc[..., OFF_CKV:OFF_KR], pc[..., OFF_KR:OFF_CX]
        k_ctx, v_ctx = mla_keys_values(pc_ckv, pc_kr, lw, None)
        x = x + mod[:, 2] * latent_mixer(p, rope, k_ctx, v_ctx, lw)
        if not last:
            xc = xc + modc[2] * context_mixer(pc, k_ctx, v_ctx, lw)

        h2 = modulate(rms_norm(x, norm_ffn[l]), mod[:, 3], mod[:, 4])
        x = x + mod[:, 5] * swiglu(h2, lw)
        if not last:
            hc2 = modulate(rms_norm(xc, norm_ffn[l]), modc[3], modc[4])
            xc = xc + modc[5] * swiglu(hc2, lw)
    return x
```

```python
import functools

import numpy as np
import jax
import jax.numpy as jnp
from jax import lax
from jax.experimental import pallas as pl
from jax.experimental.pallas import tpu as pltpu

F32 = jnp.float32
BF16 = jnp.bfloat16

D_MODEL = 2048
BATCH = 2
SEQ = 4096
DEPTH = 2
GRID_W = 64
CTX_LEN = 256
FOURIER_GROUPS = 4
FOURIER_GROUP_W = 128
FOURIER_W = FOURIER_GROUPS * FOURIER_GROUP_W
MLA_HEADS = 8
Q_RANK = 512
KV_RANK = 256
QK_NOPE = 128
QK_ROPE = 64
QK_HEAD = QK_NOPE + QK_ROPE
V_HEAD = 128
ROPE_THETA = 10000.0
AXIS_PAIRS = QK_ROPE // 4
CONV_W = 512
N_BRANCH = 3
D_FF = ((8 * D_MODEL // 3 + 255) // 256) * 256
N_MOD = 6
RMS_EPS = 1e-6

OFF_F = 0
OFF_CQ = OFF_F + FOURIER_W
OFF_CKV = OFF_CQ + Q_RANK
OFF_KR = OFF_CKV + KV_RANK
OFF_CX = OFF_KR + QK_ROPE
OFF_G = OFF_CX + 3 * CONV_W
N_IN = OFF_G + N_BRANCH * D_MODEL

LAT_ROWS = BATCH * SEQ
CTX_ROWS = BATCH * CTX_LEN
ROWS = LAT_ROWS + CTX_ROWS
KEYS = SEQ + CTX_LEN
TM = 512
N_TILES = ROWS // TM
LAT_TILES = LAT_ROWS // TM
TILES_PER_BATCH = SEQ // TM
TQ = 256
LAT_QTILES = SEQ // TQ
MOD_ROWS = 8

P_TN = 1536
P_COLS = 6 * P_TN
PC_F = 0
PC_CQ = 512
PC_CKV = 1024
PC_CX = 1536
PC_G = 3072
HEAD_PAD = 256
Q_EXT = 384


def _rms(x, eps=RMS_EPS):
    return x * lax.rsqrt(jnp.mean(x * x, axis=-1, keepdims=True) + eps)


def _sigmoid(z):
    return 1.0 / (1.0 + jnp.exp(-z))


ADA_TN = 1024


def _ada_kernel(a_ref, w_ref, b_ref, o_ref):
    a = a_ref[...]
    a = (a * _sigmoid(a)).astype(BF16)
    o_ref[0] = jnp.dot(a, w_ref[0].astype(BF16), preferred_element_type=F32) + b_ref[0]


def _ada(cond, w_ada, b_ada):
    n = N_MOD * D_MODEL
    return pl.pallas_call(
        _ada_kernel,
        grid=(DEPTH, n // ADA_TN),
        in_specs=[
            pl.BlockSpec((MOD_ROWS, D_MODEL), lambda l, j: (0, 0)),
            pl.BlockSpec((1, D_MODEL, ADA_TN), lambda l, j: (l, 0, j)),
            pl.BlockSpec((1, 1, ADA_TN), lambda l, j: (l, 0, j)),
        ],
        out_specs=pl.BlockSpec((1, MOD_ROWS, ADA_TN), lambda l, j: (l, 0, j)),
        out_shape=jax.ShapeDtypeStruct((DEPTH, MOD_ROWS, n), F32),
        name="ada",
    )(cond, w_ada, b_ada.reshape(DEPTH, 1, n))


def _mod_spec(layer, which, width=D_MODEL, col=None):
    base = layer * MOD_ROWS * N_MOD + which
    if col is None:
        return pl.BlockSpec((1, 1, width), lambda i, j: (base + (i // TILES_PER_BATCH) * N_MOD, 0, 0))
    return pl.BlockSpec((1, 1, width), lambda i, j: (base + (i // TILES_PER_BATCH) * N_MOD, 0, j))


def _inproj_kernel(x_ref, gain_ref, shift_ref, scale_ref, w_ref, bias_ref, o_ref, h_ref):
    j = pl.program_id(1)

    @pl.when(j == 0)
    def _():
        y = _rms(x_ref[...]) * gain_ref[...]
        h_ref[...] = (y * (1.0 + scale_ref[0]) + shift_ref[0]).astype(BF16)

    acc = jnp.dot(h_ref[...], w_ref[...], preferred_element_type=F32)

    @pl.when(j < PC_G // P_TN)
    def _():
        o_ref[...] = acc.astype(BF16)

    @pl.when(j >= PC_G // P_TN)
    def _():
        o_ref[...] = _sigmoid(acc + bias_ref[...]).astype(BF16)


def _inproj(x_all, gain, mod, layer, w_pack, bias_pack):
    return pl.pallas_call(
        _inproj_kernel,
        grid=(N_TILES, P_COLS // P_TN),
        in_specs=[
            pl.BlockSpec((TM, D_MODEL), lambda i, j: (i, 0)),
            pl.BlockSpec((1, D_MODEL), lambda i, j: (0, 0)),
            _mod_spec(layer, 0),
            _mod_spec(layer, 1),
            pl.BlockSpec((D_MODEL, P_TN), lambda i, j: (0, j)),
            pl.BlockSpec((1, P_TN), lambda i, j: (0, j)),
        ],
        out_specs=pl.BlockSpec((TM, P_TN), lambda i, j: (i, j)),
        out_shape=jax.ShapeDtypeStruct((ROWS, P_COLS), BF16),
        scratch_shapes=[pltpu.VMEM((TM, D_MODEL), BF16)],
        compiler_params=pltpu.CompilerParams(dimension_semantics=("parallel", "arbitrary")),
        name="inproj",
    )(x_all, gain, mod, mod, w_pack, bias_pack)


def _qkv_kernel(cq_ref, ckv_ref, wq_ref, wkv_ref, qa_ref, kva_ref, gq_ref, gk_ref, ca_ref, sb_ref,
                q_ref, k_ref, v_ref):
    ca = ca_ref[...]
    sb = sb_ref[...]
    scale = QK_HEAD ** -0.5

    cqn = (_rms(cq_ref[...].astype(F32)) * qa_ref[...]).astype(BF16)
    qe = jnp.dot(cqn, wq_ref[...], preferred_element_type=F32)
    gq = gq_ref[...]
    rot_a = gq[1:2] * ca
    rot_b = gq[2:3] * sb
    for h in range(MLA_HEADS):
        nope = qe[:, h * Q_EXT:h * Q_EXT + QK_NOPE]
        ra = qe[:, h * Q_EXT + QK_NOPE:h * Q_EXT + 2 * QK_NOPE]
        rb = qe[:, h * Q_EXT + 2 * QK_NOPE:(h + 1) * Q_EXT]
        ssq = jnp.sum(nope * nope, axis=-1, keepdims=True) + jnp.sum(ra * ra, axis=-1, keepdims=True)
        r = lax.rsqrt(ssq * (1.0 / QK_HEAD) + RMS_EPS) * scale
        q_ref[h, :, 0:QK_NOPE] = (nope * gq[0:1] * r).astype(BF16)
        q_ref[h, :, QK_NOPE:HEAD_PAD] = ((ra * rot_a + rb * rot_b) * r).astype(BF16)

    c2 = ckv_ref[...].astype(F32)
    ckvn = (_rms(c2[:, 0:KV_RANK]) * kva_ref[...]).astype(BF16)
    kv = jnp.dot(ckvn, wkv_ref[...], preferred_element_type=F32)
    kra = c2[:, KV_RANK:KV_RANK + QK_NOPE]
    krb = c2[:, KV_RANK + QK_NOPE:KV_RANK + 2 * QK_NOPE]
    gk = gk_ref[...]
    krot = kra * (gk[1:2] * ca) + krb * (gk[2:3] * sb)
    ssq_r = jnp.sum(kra * kra, axis=-1, keepdims=True)
    for h in range(MLA_HEADS):
        knope = kv[:, h * 2 * QK_NOPE:h * 2 * QK_NOPE + QK_NOPE]
        ssq = jnp.sum(knope * knope, axis=-1, keepdims=True) + ssq_r
        r = lax.rsqrt(ssq * (1.0 / QK_HEAD) + RMS_EPS)
        k_ref[0, h, :, 0:QK_NOPE] = (knope * gk[0:1] * r).astype(BF16)
        k_ref[0, h, :, QK_NOPE:HEAD_PAD] = (krot * r).astype(BF16)
        v_ref[0, h] = kv[:, h * 2 * QK_NOPE + QK_NOPE:(h + 1) * 2 * QK_NOPE].astype(BF16)


def _kv_index(t):
    lat = t < BATCH * LAT_QTILES
    b = jnp.where(lat, t // LAT_QTILES, t - BATCH * LAT_QTILES)
    pos = jnp.where(lat, t % LAT_QTILES, LAT_QTILES)
    return b, pos


def _qkv(p, wq_ext, w_ukv, qa, kva, gq, gk, rope_ca, rope_sb):
    def kv_map(t):
        b, pos = _kv_index(t)
        return (b, 0, pos, 0)

    full = lambda shape: pl.BlockSpec(shape, lambda t: (0,) * len(shape))
    return pl.pallas_call(
        _qkv_kernel,
        grid=(ROWS // TQ,),
        in_specs=[
            pl.BlockSpec((TQ, Q_RANK), lambda t: (t, PC_CQ // Q_RANK)),
            pl.BlockSpec((TQ, 512), lambda t: (t, PC_CKV // 512)),
            full((Q_RANK, MLA_HEADS * Q_EXT)),
            full((KV_RANK, MLA_HEADS * 2 * QK_NOPE)),
            full((1, Q_RANK)),
            full((1, KV_RANK)),
            full((8, QK_NOPE)),
            full((8, QK_NOPE)),
            pl.BlockSpec((TQ, QK_NOPE), lambda t: (t, 0)),
            pl.BlockSpec((TQ, QK_NOPE), lambda t: (t, 0)),
        ],
        out_specs=[
            pl.BlockSpec((MLA_HEADS, TQ, HEAD_PAD), lambda t: (0, t, 0)),
            pl.BlockSpec((1, MLA_HEADS, TQ, HEAD_PAD), kv_map),
            pl.BlockSpec((1, MLA_HEADS, TQ, V_HEAD), kv_map),
        ],
        out_shape=[
            jax.ShapeDtypeStruct((MLA_HEADS, ROWS, HEAD_PAD), BF16),
            jax.ShapeDtypeStruct((BATCH, MLA_HEADS, KEYS, HEAD_PAD), BF16),
            jax.ShapeDtypeStruct((BATCH, MLA_HEADS, KEYS, V_HEAD), BF16),
        ],
        name="qkv",
    )(p, p, wq_ext, w_ukv, qa, kva, gq, gk, rope_ca, rope_sb)


def _softmax_pv(q, k, v):
    s = lax.dot_general(q, k, (((1,), (1,)), ((), ())), preferred_element_type=F32)
    m = jnp.max(s, axis=-1, keepdims=True)
    e = jnp.exp(s - m)
    l = jnp.sum(e, axis=-1, keepdims=True)
    o = jnp.dot(e.astype(BF16), v, preferred_element_type=F32)
    return o / l


def _attn_kernel(q_ref, k_ref, v_ref, o_ref):
    r = pl.program_id(2)

    @pl.when(r < LAT_QTILES)
    def _():
        o_ref[...] = _softmax_pv(q_ref[0], k_ref[0, 0], v_ref[0, 0]).astype(BF16)

    @pl.when(r == LAT_QTILES)
    def _():
        o_ref[...] = _softmax_pv(q_ref[0], k_ref[0, 0, SEQ:KEYS, :], v_ref[0, 0, SEQ:KEYS, :]).astype(BF16)


def _attention(q, k, v, with_ctx):
    def qtile(b, r):
        return jnp.where(r < LAT_QTILES, b * LAT_QTILES + r, BATCH * LAT_QTILES + b)

    return pl.pallas_call(
        _attn_kernel,
        grid=(BATCH, MLA_HEADS, LAT_QTILES + (1 if with_ctx else 0)),
        in_specs=[
            pl.BlockSpec((1, TQ, HEAD_PAD), lambda b, h, r: (h, qtile(b, r), 0)),
            pl.BlockSpec((1, 1, KEYS, HEAD_PAD), lambda b, h, r: (b, h, 0, 0)),
            pl.BlockSpec((1, 1, KEYS, V_HEAD), lambda b, h, r: (b, h, 0, 0)),
        ],
        out_specs=pl.BlockSpec((TQ, V_HEAD), lambda b, h, r: (qtile(b, r), h)),
        out_shape=jax.ShapeDtypeStruct((ROWS, MLA_HEADS * V_HEAD), BF16),
        compiler_params=pltpu.CompilerParams(dimension_semantics=("parallel", "parallel", "arbitrary")),
        name="attention",
    )(q, k, v)


def _chan_dft_kernel(pf_ref, w_ref, o_ref):
    uv = jnp.dot(pf_ref[...], w_ref[...], preferred_element_type=F32)
    o_ref[0] = uv[:, 0:FOURIER_W].astype(BF16)
    o_ref[1] = uv[:, FOURIER_W:2 * FOURIER_W].astype(BF16)


def _chan_dft(p, w_chan, n_tiles):
    return pl.pallas_call(
        _chan_dft_kernel,
        grid=(n_tiles,),
        in_specs=[
            pl.BlockSpec((TM, FOURIER_W), lambda i: (i, PC_F // FOURIER_W)),
            pl.BlockSpec((FOURIER_W, 2 * FOURIER_W), lambda i: (0, 0)),
        ],
        out_specs=pl.BlockSpec((2, TM, FOURIER_W), lambda i: (0, i, 0)),
        out_shape=jax.ShapeDtypeStruct((2, ROWS, FOURIER_W), BF16),
        name="chan_dft",
    )(p, w_chan)


def _pos_dft_kernel(d_ref, uv_ref, o_ref, acc_ref):
    k = pl.program_id(2)

    @pl.when(k == 0)
    def _():
        acc_ref[...] = jnp.zeros_like(acc_ref)

    acc_ref[...] += jnp.dot(d_ref[...], uv_ref[0], preferred_element_type=F32)

    @pl.when(k == pl.num_programs(2) - 1)
    def _():
        o_ref[...] = acc_ref[...].astype(BF16)


def _pos_dft(dft, uv, length, row0, tm, tk):
    nk_half = length // tk
    kb0 = row0 // tk

    def uv_map(b, m, k):
        return (k // nk_half, kb0 + b * nk_half + k % nk_half, 0)

    return pl.pallas_call(
        _pos_dft_kernel,
        grid=(BATCH, length // tm, 2 * nk_half),
        in_specs=[
            pl.BlockSpec((tm, tk), lambda b, m, k: (m, k)),
            pl.BlockSpec((1, tk, FOURIER_W), uv_map),
        ],
        out_specs=pl.BlockSpec((tm, FOURIER_W), lambda b, m, k: (b * (length // tm) + m, 0)),
        out_shape=jax.ShapeDtypeStruct((BATCH * length, FOURIER_W), BF16),
        scratch_shapes=[pltpu.VMEM((tm, FOURIER_W), F32)],
        compiler_params=pltpu.CompilerParams(dimension_semantics=("parallel", "parallel", "arbitrary")),
        name="pos_dft_%d" % length,
    )(dft, uv)


HALO = 16
STREAM_STARTS = (0, SEQ, LAT_ROWS, LAT_ROWS + CTX_LEN)


def _conv_kernel(cx_ref, cb_ref, cc_ref, pcx_ref, pcc_ref, ncx_ref, ncc_ref, w_ref, o_ref):
    i = pl.program_id(0)
    u = cc_ref[...].astype(F32) * cx_ref[...].astype(F32)
    row = lax.broadcasted_iota(jnp.int32, (TM, 1), 0)
    g = row + i * TM
    first = functools.reduce(jnp.logical_or, [g == s for s in STREAM_STARTS])
    last = functools.reduce(jnp.logical_or, [g == s - 1 for s in STREAM_STARTS[1:] + (ROWS,)])
    halo_prev = pcc_ref[HALO - 1:HALO, :].astype(F32) * pcx_ref[HALO - 1:HALO, :].astype(F32)
    halo_next = ncc_ref[0:1, :].astype(F32) * ncx_ref[0:1, :].astype(F32)
    up = jnp.where(row == 0, halo_prev, pltpu.roll(u, 1, 0))
    un = jnp.where(row == TM - 1, halo_next, pltpu.roll(u, TM - 1, 0))
    up = jnp.where(first, 0.0, up)
    un = jnp.where(last, 0.0, un)
    w = w_ref[...]
    y = up * w[0:1] + u * w[1:2] + un * w[2:3]
    o_ref[...] = (cb_ref[...].astype(F32) * y).astype(BF16)


def _short_conv(p, conv_w, n_tiles):
    cb0 = PC_CX // CONV_W
    per = TM // HALO
    prev_map = lambda c: (lambda i: (jnp.maximum(i * per - 1, 0), c))
    next_map = lambda c: (lambda i: (jnp.minimum((i + 1) * per, ROWS // HALO - 1), c))
    return pl.pallas_call(
        _conv_kernel,
        grid=(n_tiles,),
        in_specs=[
            pl.BlockSpec((TM, CONV_W), lambda i: (i, cb0)),
            pl.BlockSpec((TM, CONV_W), lambda i: (i, cb0 + 1)),
            pl.BlockSpec((TM, CONV_W), lambda i: (i, cb0 + 2)),
            pl.BlockSpec((HALO, CONV_W), prev_map(cb0)),
            pl.BlockSpec((HALO, CONV_W), prev_map(cb0 + 2)),
            pl.BlockSpec((HALO, CONV_W), next_map(cb0)),
            pl.BlockSpec((HALO, CONV_W), next_map(cb0 + 2)),
            pl.BlockSpec((8, CONV_W), lambda i: (0, 0)),
        ],
        out_specs=pl.BlockSpec((TM, CONV_W), lambda i: (i, 0)),
        out_shape=jax.ShapeDtypeStruct((ROWS, CONV_W), BF16),
        name="short_conv",
    )(p, p, p, p, p, p, p, conv_w)


MERGE_TN = 1024


def _merge_kernel(fl_ref, fc_ref, a_ref, c_ref, wf_ref, wm_ref, wc_ref, g0_ref, g1_ref, g2_ref, o_ref):
    i = pl.program_id(0)
    f = jnp.where(i == LAT_TILES, fc_ref[...], fl_ref[...])
    y_f = jnp.dot(f, wf_ref[...], preferred_element_type=F32)
    y_m = jnp.dot(a_ref[...], wm_ref[...], preferred_element_type=F32)
    y_c = jnp.dot(c_ref[...], wc_ref[...], preferred_element_type=F32)
    merged = (g0_ref[...].astype(F32) * y_f + g1_ref[...].astype(F32) * y_m
              + g2_ref[...].astype(F32) * y_c)
    o_ref[...] = merged.astype(BF16)


def _merge(f_lat, f_ctx, attn, cmix, p, w_f, w_m, w_c, n_tiles):
    gb = PC_G // MERGE_TN
    per = D_MODEL // MERGE_TN
    return pl.pallas_call(
        _merge_kernel,
        grid=(n_tiles, per),
        in_specs=[
            pl.BlockSpec((TM, FOURIER_W), lambda i, n: (jnp.minimum(i, LAT_TILES - 1), 0)),
            pl.BlockSpec((TM, FOURIER_W), lambda i, n: (0, 0)),
            pl.BlockSpec((TM, MLA_HEADS * V_HEAD), lambda i, n: (i, 0)),
            pl.BlockSpec((TM, CONV_W), lambda i, n: (i, 0)),
            pl.BlockSpec((FOURIER_W, MERGE_TN), lambda i, n: (0, n)),
            pl.BlockSpec((MLA_HEADS * V_HEAD, MERGE_TN), lambda i, n: (0, n)),
            pl.BlockSpec((CONV_W, MERGE_TN), lambda i, n: (0, n)),
            pl.BlockSpec((TM, MERGE_TN), lambda i, n: (i, gb + n)),
            pl.BlockSpec((TM, MERGE_TN), lambda i, n: (i, gb + per + n)),
            pl.BlockSpec((TM, MERGE_TN), lambda i, n: (i, gb + 2 * per + n)),
        ],
        out_specs=pl.BlockSpec((TM, MERGE_TN), lambda i, n: (i, n)),
        out_shape=jax.ShapeDtypeStruct((n_tiles * TM, D_MODEL), BF16),
        name="merge",
    )(f_lat, f_ctx, attn, cmix, w_f, w_m, w_c, p, p, p)


def _outproj_kernel(m_ref, w_ref, x_ref, gate_ref, o_ref):
    acc = jnp.dot(m_ref[...], w_ref[...], preferred_element_type=F32)
    o_ref[...] = x_ref[...] + gate_ref[0] * acc


def _outproj(merged, w_out, x_all, mod, layer, n_tiles):
    return pl.pallas_call(
        _outproj_kernel,
        grid=(n_tiles, D_MODEL // MERGE_TN),
        in_specs=[
            pl.BlockSpec((TM, D_MODEL), lambda i, n: (i, 0)),
            pl.BlockSpec((D_MODEL, MERGE_TN), lambda i, n: (0, n)),
            pl.BlockSpec((TM, MERGE_TN), lambda i, n: (i, n)),
            _mod_spec(layer, 2, MERGE_TN, col=True),
        ],
        out_specs=pl.BlockSpec((TM, MERGE_TN), lambda i, n: (i, n)),
        out_shape=jax.ShapeDtypeStruct((n_tiles * TM, D_MODEL), F32),
        name="outproj",
    )(merged, w_out, x_all, mod)


FF_TN = 512


def _ffn_kernel(x_ref, gain_ref, shift_ref, scale_ref, gate_ref, wg_ref, wu_ref, wd_ref, o_ref,
                h_ref, acc_ref):
    f = pl.program_id(1)

    @pl.when(f == 0)
    def _():
        y = _rms(x_ref[...]) * gain_ref[...]
        h_ref[...] = (y * (1.0 + scale_ref[0]) + shift_ref[0]).astype(BF16)
        acc_ref[...] = jnp.zeros_like(acc_ref)

    h = h_ref[...]
    g = jnp.dot(h, wg_ref[...], preferred_element_type=F32)
    u = jnp.dot(h, wu_ref[...], preferred_element_type=F32)
    a = (g * _sigmoid(g) * u).astype(BF16)
    acc_ref[...] += jnp.dot(a, wd_ref[...], preferred_element_type=F32)

    @pl.when(f == pl.num_programs(1) - 1)
    def _():
        o_ref[...] = x_ref[...] + gate_ref[0] * acc_ref[...]


def _ffn(x_all, gain, mod, layer, w_gate, w_up, w_down, n_tiles):
    return pl.pallas_call(
        _ffn_kernel,
        grid=(n_tiles, D_FF // FF_TN),
        in_specs=[
            pl.BlockSpec((TM, D_MODEL), lambda i, f: (i, 0)),
            pl.BlockSpec((1, D_MODEL), lambda i, f: (0, 0)),
            _mod_spec(layer, 3),
            _mod_spec(layer, 4),
            _mod_spec(layer, 5),
            pl.BlockSpec((D_MODEL, FF_TN), lambda i, f: (0, f)),
            pl.BlockSpec((D_MODEL, FF_TN), lambda i, f: (0, f)),
            pl.BlockSpec((FF_TN, D_MODEL), lambda i, f: (f, 0)),
        ],
        out_specs=pl.BlockSpec((TM, D_MODEL), lambda i, f: (i, 0)),
        out_shape=jax.ShapeDtypeStruct((n_tiles * TM, D_MODEL), F32),
        scratch_shapes=[pltpu.VMEM((TM, D_MODEL), BF16), pltpu.VMEM((TM, D_MODEL), F32)],
        compiler_params=pltpu.CompilerParams(dimension_semantics=("parallel", "arbitrary")),
        name="ffn",
    )(x_all, gain, mod, mod, mod, w_gate, w_up, w_down)


def _dft_cos_sin(length):
    kn = np.outer(np.arange(length), np.arange(length)) % length
    ang = 2.0 * np.pi * kn / length
    return np.cos(ang), np.sin(ang)


def _chan_dft_matrix():
    c, s = _dft_cos_sin(FOURIER_GROUP_W)
    eye = np.eye(FOURIER_GROUPS)
    return jnp.asarray(np.concatenate([np.kron(eye, c), np.kron(eye, s)], axis=1), F32).astype(BF16)


def _ctx_dft_matrix():
    c, s = _dft_cos_sin(CTX_LEN)
    norm = (CTX_LEN * FOURIER_GROUP_W) ** -0.5
    return jnp.asarray(np.concatenate([c, -s], axis=1) * norm, F32).astype(BF16)


def _lat_dft_matrix():
    r = 64
    n = np.arange(SEQ)
    hi = 2.0 * np.pi * (np.outer(np.arange(r), n) % r) / r
    lo = 2.0 * np.pi * (np.outer(np.arange(r), n) % SEQ) / SEQ
    norm = (SEQ * FOURIER_GROUP_W) ** -0.5
    pc, ps = jnp.asarray(np.cos(hi) * norm, F32)[:, None, :], jnp.asarray(np.sin(hi) * norm, F32)[:, None, :]
    qc, qs = jnp.asarray(np.cos(lo), F32)[None, :, :], jnp.asarray(np.sin(lo), F32)[None, :, :]
    c = (pc * qc - ps * qs).reshape(SEQ, SEQ)
    s = (ps * qc + pc * qs).reshape(SEQ, SEQ)
    return jnp.concatenate([c.astype(BF16), (-s).astype(BF16)], axis=1)


def _rope_tables():
    rows = SEQ // GRID_W
    row = jnp.repeat(jnp.arange(rows), GRID_W)
    col = jnp.tile(jnp.arange(GRID_W), rows)
    inv_freq = ROPE_THETA ** (-jnp.arange(AXIS_PAIRS, dtype=F32) / AXIS_PAIRS)
    ang = jnp.concatenate([row[:, None] * inv_freq, col[:, None] * inv_freq], axis=-1)
    cos, sin = jnp.cos(ang), jnp.sin(ang)
    zeros = jnp.zeros((SEQ, QK_NOPE - QK_ROPE), F32)
    ca = jnp.concatenate([cos, cos, zeros], axis=-1)
    sb = jnp.concatenate([-sin, sin, zeros], axis=-1)
    ca = jnp.concatenate([ca, ca, jnp.ones((CTX_ROWS, QK_NOPE), F32)], axis=0)
    sb = jnp.concatenate([sb, sb, jnp.zeros((CTX_ROWS, QK_NOPE), F32)], axis=0)
    return ca, sb


def _rope_split(w):
    half = QK_ROPE // 2
    pad = jnp.zeros(w.shape[:-1] + (QK_NOPE - QK_ROPE,), w.dtype)
    a = jnp.concatenate([w, pad], axis=-1)
    b = jnp.concatenate([w[..., half:], w[..., :half], pad], axis=-1)
    return a, b


def _pack_w_in(w_in):
    kr_a, kr_b = _rope_split(w_in[..., OFF_KR:OFF_CX])
    return jnp.concatenate([w_in[..., :OFF_KR], kr_a, kr_b, w_in[..., OFF_CX:]], axis=-1).astype(BF16)


def _pack_w_uq(w_uq):
    w = w_uq.reshape(DEPTH, Q_RANK, MLA_HEADS, QK_HEAD)
    ra, rb = _rope_split(w[..., QK_NOPE:])
    return jnp.concatenate([w[..., :QK_NOPE], ra, rb], axis=-1).reshape(DEPTH, Q_RANK, MLA_HEADS * Q_EXT).astype(BF16)


def _pack_head_gain(g):
    ga, gb = _rope_split(g[:, QK_NOPE:])
    rows = jnp.stack([g[:, :QK_NOPE], ga, gb], axis=1)
    return jnp.concatenate([rows, jnp.zeros((DEPTH, 5, QK_NOPE), F32)], axis=1)


def kernel(x, c, ctx, c_ctx, w_ada, b_ada, norm_mix, norm_ffn, w_in, b_gate, q_a_norm, kv_a_norm, w_uq, w_ukv,
           q_norm, k_norm, w_f_out, w_mla_out, conv_w, w_conv_out, w_out, w_ffn_gate, w_ffn_up, w_ffn_down):
    x_all = jnp.concatenate([x.reshape(LAT_ROWS, D_MODEL), ctx.reshape(CTX_ROWS, D_MODEL)], axis=0)
    cond = jnp.concatenate([c, c_ctx[None, :], jnp.zeros((MOD_ROWS - BATCH - 1, D_MODEL), F32)], axis=0)
    w_pack = _pack_w_in(w_in)
    bias_pack = jnp.concatenate([jnp.zeros((DEPTH, 1, PC_G), F32), b_gate[:, None, :]], axis=-1)
    wq_ext = _pack_w_uq(w_uq)
    gq = _pack_head_gain(q_norm)
    gk = _pack_head_gain(k_norm)
    conv_w8 = jnp.concatenate([conv_w, jnp.zeros((DEPTH, 5, CONV_W), F32)], axis=1)
    rope_ca, rope_sb = _rope_tables()
    w_chan = _chan_dft_matrix()
    dft_lat = _lat_dft_matrix()
    dft_ctx = _ctx_dft_matrix()
    bf = lambda w: w.astype(BF16)
    w_ukv_b, w_f_b, w_m_b, w_c_b, w_out_b = bf(w_ukv), bf(w_f_out), bf(w_mla_out), bf(w_conv_out), bf(w_out)
    w_g_b, w_u_b, w_d_b = bf(w_ffn_gate), bf(w_ffn_up), bf(w_ffn_down)

    mod = _ada(cond, w_ada, b_ada).reshape(DEPTH * MOD_ROWS * N_MOD, 1, D_MODEL)

    for l in range(DEPTH):
        last = l == DEPTH - 1
        n_tiles = LAT_TILES if last else N_TILES
        p = _inproj(x_all, norm_mix[l][None, :], mod, l, w_pack[l], bias_pack[l])
        q, k, v = _qkv(p, wq_ext[l], w_ukv_b[l], q_a_norm[l][None, :], kv_a_norm[l][None, :], gq[l], gk[l],
                       rope_ca, rope_sb)
        attn = _attention(q, k, v, with_ctx=not last)
        uv = _chan_dft(p, w_chan, n_tiles)
        f_lat = _pos_dft(dft_lat, uv, SEQ, 0, 1024, 2048)
        f_ctx = f_lat if last else _pos_dft(dft_ctx, uv, CTX_LEN, LAT_ROWS, CTX_LEN, CTX_LEN)
        cmix = _short_conv(p, conv_w8[l], n_tiles)
        merged = _merge(f_lat, f_ctx, attn, cmix, p, w_f_b[l], w_m_b[l], w_c_b[l], n_tiles)
        x_all = _outproj(merged, w_out_b[l], x_all, mod, l, n_tiles)
        x_all = _ffn(x_all, norm_ffn[l][None, :], mod, l, w_g_b[l], w_u_b[l], w_d_b[l], n_tiles)
    return x_all.reshape(BATCH, SEQ, D_MODEL)
```

```python
import functools

import numpy as np
import jax
import jax.numpy as jnp
from jax import lax
from jax.experimental import pallas as pl
from jax.experimental.pallas import tpu as pltpu

F32 = jnp.float32
BF16 = jnp.bfloat16

D_MODEL = 2048
BATCH = 2
SEQ = 4096
DEPTH = 2
GRID_W = 64
CTX_LEN = 256
FOURIER_GROUPS = 4
FOURIER_GROUP_W = 128
FOURIER_W = FOURIER_GROUPS * FOURIER_GROUP_W
MLA_HEADS = 8
Q_RANK = 512
KV_RANK = 256
QK_NOPE = 128
QK_ROPE = 64
QK_HEAD = QK_NOPE + QK_ROPE
V_HEAD = 128
ROPE_THETA = 10000.0
AXIS_PAIRS = QK_ROPE // 4
CONV_W = 512
N_BRANCH = 3
D_FF = ((8 * D_MODEL // 3 + 255) // 256) * 256
N_MOD = 6
RMS_EPS = 1e-6

OFF_F = 0
OFF_CQ = OFF_F + FOURIER_W
OFF_CKV = OFF_CQ + Q_RANK
OFF_KR = OFF_CKV + KV_RANK
OFF_CX = OFF_KR + QK_ROPE
OFF_G = OFF_CX + 3 * CONV_W
N_IN = OFF_G + N_BRANCH * D_MODEL

LAT_ROWS = BATCH * SEQ
CTX_ROWS = BATCH * CTX_LEN
ROWS = LAT_ROWS + CTX_ROWS
KEYS = SEQ + CTX_LEN
TM = 512
N_TILES = ROWS // TM
LAT_TILES = LAT_ROWS // TM
TILES_PER_BATCH = SEQ // TM
TQ = 256
LAT_QTILES = SEQ // TQ
MOD_ROWS = 8

P_TN = 1536
P_COLS = 6 * P_TN
PC_F = 0
PC_CQ = 512
PC_CKV = 1024
PC_CX = 1536
PC_G = 3072
HEAD_PAD = 256
Q_EXT = 384


def _rms(x, eps=RMS_EPS):
    return x * lax.rsqrt(jnp.mean(x * x, axis=-1, keepdims=True) + eps)


def _sigmoid(z):
    return 1.0 / (1.0 + jnp.exp(-z))


def _rope_split(w):
    half = QK_ROPE // 2
    pad = jnp.zeros(w.shape[:-1] + (QK_NOPE - QK_ROPE,), w.dtype)
    a = jnp.concatenate([w, pad], axis=-1)
    b = jnp.concatenate([w[..., half:], w[..., :half], pad], axis=-1)
    return a, b


ADA_TN = 1024


def _ada_kernel(a_ref, w_ref, b_ref, o_ref):
    a = a_ref[...]
    a = (a * _sigmoid(a)).astype(BF16)
    o_ref[...] = jnp.dot(a, w_ref[...].astype(BF16), preferred_element_type=F32) + b_ref[...]


def _ada(cond, w_ada, b_ada):
    n = N_MOD * D_MODEL
    return pl.pallas_call(
        _ada_kernel,
        grid=(DEPTH, n // ADA_TN),
        in_specs=[
            pl.BlockSpec((MOD_ROWS, D_MODEL), lambda l, j: (0, 0)),
            pl.BlockSpec((None, D_MODEL, ADA_TN), lambda l, j: (l, 0, j)),
            pl.BlockSpec((None, 1, ADA_TN), lambda l, j: (l, 0, j)),
        ],
        out_specs=pl.BlockSpec((None, MOD_ROWS, ADA_TN), lambda l, j: (l, 0, j)),
        out_shape=jax.ShapeDtypeStruct((DEPTH, MOD_ROWS, n), F32),
        name="ada",
    )(cond, w_ada, b_ada.reshape(DEPTH, 1, n))


def _mod_spec(layer, which, width=D_MODEL, col=False):
    base = layer * MOD_ROWS * N_MOD + which
    if col:
        return pl.BlockSpec((None, 1, width), lambda i, j: (base + (i // TILES_PER_BATCH) * N_MOD, 0, j))
    return pl.BlockSpec((None, 1, width), lambda i, j: (base + (i // TILES_PER_BATCH) * N_MOD, 0, 0))


PACK_ROWS = 256


def _pack_kernel(w_ref, o_ref):
    o_ref[:, 0:OFF_KR] = w_ref[:, 0:OFF_KR].astype(BF16)
    kr_a, kr_b = _rope_split(w_ref[:, OFF_KR:OFF_CX])
    o_ref[:, OFF_KR:OFF_KR + QK_NOPE] = kr_a.astype(BF16)
    o_ref[:, OFF_KR + QK_NOPE:PC_CX] = kr_b.astype(BF16)
    o_ref[:, PC_CX:P_COLS] = w_ref[:, OFF_CX:N_IN].astype(BF16)


def _pack_w_in(w_in):
    return pl.pallas_call(
        _pack_kernel,
        grid=(DEPTH, D_MODEL // PACK_ROWS),
        in_specs=[pl.BlockSpec((None, PACK_ROWS, N_IN), lambda l, r: (l, r, 0))],
        out_specs=pl.BlockSpec((None, PACK_ROWS, P_COLS), lambda l, r: (l, r, 0)),
        out_shape=jax.ShapeDtypeStruct((DEPTH, D_MODEL, P_COLS), BF16),
        name="pack_w_in",
    )(w_in)


def _inproj_kernel(x_ref, gain_ref, shift_ref, scale_ref, w_ref, bias_ref, o_ref, h_ref):
    j = pl.program_id(1)

    @pl.when(j == 0)
    def _():
        y = _rms(x_ref[...]) * gain_ref[...]
        h_ref[...] = (y * (1.0 + scale_ref[...]) + shift_ref[...]).astype(BF16)

    acc = jnp.dot(h_ref[...], w_ref[...], preferred_element_type=F32)

    @pl.when(j < PC_G // P_TN)
    def _():
        o_ref[...] = acc.astype(BF16)

    @pl.when(j >= PC_G // P_TN)
    def _():
        o_ref[...] = _sigmoid(acc + bias_ref[...]).astype(BF16)


def _inproj(x_all, gain, mod, layer, w_pack, bias_pack):
    return pl.pallas_call(
        _inproj_kernel,
        grid=(N_TILES, P_COLS // P_TN),
        in_specs=[
            pl.BlockSpec((TM, D_MODEL), lambda i, j: (i, 0)),
            pl.BlockSpec((None, 1, D_MODEL), lambda i, j: (layer, 0, 0)),
            _mod_spec(layer, 0),
            _mod_spec(layer, 1),
            pl.BlockSpec((None, D_MODEL, P_TN), lambda i, j: (layer, 0, j)),
            pl.BlockSpec((None, 1, P_TN), lambda i, j: (layer, 0, j)),
        ],
        out_specs=pl.BlockSpec((TM, P_TN), lambda i, j: (i, j)),
        out_shape=jax.ShapeDtypeStruct((ROWS, P_COLS), BF16),
        scratch_shapes=[pltpu.VMEM((TM, D_MODEL), BF16)],
        compiler_params=pltpu.CompilerParams(dimension_semantics=("parallel", "arbitrary")),
        name="inproj",
    )(x_all, gain, mod, mod, w_pack, bias_pack)


def _qkv_kernel(cq_ref, ckv_ref, wq_ref, wkv_ref, qa_ref, kva_ref, gq_ref, gk_ref, ca_ref, sb_ref,
                q_ref, k_ref, v_ref):
    ca = ca_ref[...]
    sb = sb_ref[...]
    scale = QK_HEAD ** -0.5

    cqn = (_rms(cq_ref[...].astype(F32)) * qa_ref[...]).astype(BF16)
    qe = jnp.dot(cqn, wq_ref[...], preferred_element_type=F32)
    gq = gq_ref[...]
    rot_a = gq[1:2] * ca
    rot_b = gq[2:3] * sb
    for h in range(MLA_HEADS):
        nope = qe[:, h * Q_EXT:h * Q_EXT + QK_NOPE]
        ra = qe[:, h * Q_EXT + QK_NOPE:h * Q_EXT + 2 * QK_NOPE]
        rb = qe[:, h * Q_EXT + 2 * QK_NOPE:(h + 1) * Q_EXT]
        ssq = jnp.sum(nope * nope, axis=-1, keepdims=True) + jnp.sum(ra * ra, axis=-1, keepdims=True)
        r = lax.rsqrt(ssq * (1.0 / QK_HEAD) + RMS_EPS) * scale
        q_ref[h, :, 0:QK_NOPE] = (nope * gq[0:1] * r).astype(BF16)
        q_ref[h, :, QK_NOPE:HEAD_PAD] = ((ra * rot_a + rb * rot_b) * r).astype(BF16)

    c2 = ckv_ref[...].astype(F32)
    ckvn = (_rms(c2[:, 0:KV_RANK]) * kva_ref[...]).astype(BF16)
    kv = jnp.dot(ckvn, wkv_ref[...], preferred_element_type=F32)
    kra = c2[:, KV_RANK:KV_RANK + QK_NOPE]
    krb = c2[:, KV_RANK + QK_NOPE:KV_RANK + 2 * QK_NOPE]
    gk = gk_ref[...]
    krot = kra * (gk[1:2] * ca) + krb * (gk[2:3] * sb)
    ssq_r = jnp.sum(kra * kra, axis=-1, keepdims=True)
    for h in range(MLA_HEADS):
        knope = kv[:, h * 2 * QK_NOPE:h * 2 * QK_NOPE + QK_NOPE]
        ssq = jnp.sum(knope * knope, axis=-1, keepdims=True) + ssq_r
        r = lax.rsqrt(ssq * (1.0 / QK_HEAD) + RMS_EPS)
        k_ref[0, h, :, 0:QK_NOPE] = (knope * gk[0:1] * r).astype(BF16)
        k_ref[0, h, :, QK_NOPE:HEAD_PAD] = (krot * r).astype(BF16)
        v_ref[0, h] = kv[:, h * 2 * QK_NOPE + QK_NOPE:(h + 1) * 2 * QK_NOPE].astype(BF16)


def _kv_index(t):
    lat = t < BATCH * LAT_QTILES
    b = jnp.where(lat, t // LAT_QTILES, t - BATCH * LAT_QTILES)
    pos = jnp.where(lat, t % LAT_QTILES, LAT_QTILES)
    return b, pos


def _qkv(p, layer, wq_ext, w_ukv, qa, kva, gq, gk, rope_ca, rope_sb):
    def kv_map(t):
        b, pos = _kv_index(t)
        return (b, 0, pos, 0)

    def per_layer(shape):
        return pl.BlockSpec((None,) + shape, lambda t: (layer,) + (0,) * len(shape))

    return pl.pallas_call(
        _qkv_kernel,
        grid=(ROWS // TQ,),
        in_specs=[
            pl.BlockSpec((TQ, Q_RANK), lambda t: (t, PC_CQ // Q_RANK)),
            pl.BlockSpec((TQ, 512), lambda t: (t, PC_CKV // 512)),
            per_layer((Q_RANK, MLA_HEADS * Q_EXT)),
            per_layer((KV_RANK, MLA_HEADS * 2 * QK_NOPE)),
            per_layer((1, Q_RANK)),
            per_layer((1, KV_RANK)),
            per_layer((8, QK_NOPE)),
            per_layer((8, QK_NOPE)),
            pl.BlockSpec((TQ, QK_NOPE), lambda t: (t, 0)),
            pl.BlockSpec((TQ, QK_NOPE), lambda t: (t, 0)),
        ],
        out_specs=[
            pl.BlockSpec((MLA_HEADS, TQ, HEAD_PAD), lambda t: (0, t, 0)),
            pl.BlockSpec((1, MLA_HEADS, TQ, HEAD_PAD), kv_map),
            pl.BlockSpec((1, MLA_HEADS, TQ, V_HEAD), kv_map),
        ],
        out_shape=[
            jax.ShapeDtypeStruct((MLA_HEADS, ROWS, HEAD_PAD), BF16),
            jax.ShapeDtypeStruct((BATCH, MLA_HEADS, KEYS, HEAD_PAD), BF16),
            jax.ShapeDtypeStruct((BATCH, MLA_HEADS, KEYS, V_HEAD), BF16),
        ],
        name="qkv",
    )(p, p, wq_ext, w_ukv, qa, kva, gq, gk, rope_ca, rope_sb)


def _softmax_pv(q, k, v):
    s = lax.dot_general(q, k, (((1,), (1,)), ((), ())), preferred_element_type=F32)
    m = jnp.max(s, axis=-1, keepdims=True)
    e = jnp.exp(s - m)
    l = jnp.sum(e, axis=-1, keepdims=True)
    o = jnp.dot(e.astype(BF16), v, preferred_element_type=F32)
    return o / l


def _attn_kernel(q_ref, k_ref, v_ref, o_ref):
    r = pl.program_id(2)

    @pl.when(r < LAT_QTILES)
    def _():
        o_ref[...] = _softmax_pv(q_ref[0], k_ref[0, 0], v_ref[0, 0]).astype(BF16)

    @pl.when(r == LAT_QTILES)
    def _():
        o_ref[...] = _softmax_pv(q_ref[0], k_ref[0, 0, SEQ:KEYS, :], v_ref[0, 0, SEQ:KEYS, :]).astype(BF16)


def _attention(q, k, v, with_ctx):
    def qtile(b, r):
        return jnp.where(r < LAT_QTILES, b * LAT_QTILES + r, BATCH * LAT_QTILES + b)

    return pl.pallas_call(
        _attn_kernel,
        grid=(BATCH, MLA_HEADS, LAT_QTILES + (1 if with_ctx else 0)),
        in_specs=[
            pl.BlockSpec((1, TQ, HEAD_PAD), lambda b, h, r: (h, qtile(b, r), 0)),
            pl.BlockSpec((1, 1, KEYS, HEAD_PAD), lambda b, h, r: (b, h, 0, 0)),
            pl.BlockSpec((1, 1, KEYS, V_HEAD), lambda b, h, r: (b, h, 0, 0)),
        ],
        out_specs=pl.BlockSpec((TQ, V_HEAD), lambda b, h, r: (qtile(b, r), h)),
        out_shape=jax.ShapeDtypeStruct((ROWS if with_ctx else LAT_ROWS, MLA_HEADS * V_HEAD), BF16),
        compiler_params=pltpu.CompilerParams(dimension_semantics=("parallel", "parallel", "arbitrary")),
        name="attention",
    )(q, k, v)


def _chan_dft_kernel(pf_ref, w_ref, o_ref):
    uv = jnp.dot(pf_ref[...], w_ref[...], preferred_element_type=F32)
    o_ref[0] = uv[:, 0:FOURIER_W].astype(BF16)
    o_ref[1] = uv[:, FOURIER_W:2 * FOURIER_W].astype(BF16)


def _chan_dft(p, w_chan, n_tiles):
    return pl.pallas_call(
        _chan_dft_kernel,
        grid=(n_tiles,),
        in_specs=[
            pl.BlockSpec((TM, FOURIER_W), lambda i: (i, PC_F // FOURIER_W)),
            pl.BlockSpec((FOURIER_W, 2 * FOURIER_W), lambda i: (0, 0)),
        ],
        out_specs=pl.BlockSpec((2, TM, FOURIER_W), lambda i: (0, i, 0)),
        out_shape=jax.ShapeDtypeStruct((2, n_tiles * TM, FOURIER_W), BF16),
        name="chan_dft",
    )(p, w_chan)


def _pos_dft_kernel(d_ref, uv0_ref, uv1_ref, o_ref, acc_ref):
    k = pl.program_id(1)

    @pl.when(k == 0)
    def _():
        acc_ref[...] = jnp.zeros_like(acc_ref)

    d = d_ref[...]
    acc_ref[0] += jnp.dot(d, uv0_ref[...], preferred_element_type=F32)
    acc_ref[1] += jnp.dot(d, uv1_ref[...], preferred_element_type=F32)

    @pl.when(k == pl.num_programs(1) - 1)
    def _():
        o_ref[...] = acc_ref[...].astype(BF16)


def _pos_dft(dft, uv, length, row0, tm, tk):
    nk_half = length // tk
    kb0 = row0 // tk

    def uv_spec(b):
        return pl.BlockSpec((None, tk, FOURIER_W),
                            lambda m, k: (k // nk_half, kb0 + b * nk_half + k % nk_half, 0))

    out = pl.pallas_call(
        _pos_dft_kernel,
        grid=(length // tm, 2 * nk_half),
        in_specs=[pl.BlockSpec((tm, tk), lambda m, k: (m, k)), uv_spec(0), uv_spec(1)],
        out_specs=pl.BlockSpec((BATCH, tm, FOURIER_W), lambda m, k: (0, m, 0)),
        out_shape=jax.ShapeDtypeStruct((BATCH, length, FOURIER_W), BF16),
        scratch_shapes=[pltpu.VMEM((BATCH, tm, FOURIER_W), F32)],
        compiler_params=pltpu.CompilerParams(dimension_semantics=("parallel", "arbitrary")),
        name="pos_dft_%d" % length,
    )(dft, uv, uv)
    return out.reshape(BATCH * length, FOURIER_W)


HALO = 16
STREAM_STARTS = (0, SEQ, LAT_ROWS, LAT_ROWS + CTX_LEN)


def _conv_kernel(cx_ref, cb_ref, cc_ref, pcx_ref, pcc_ref, ncx_ref, ncc_ref, w_ref, o_ref):
    i = pl.program_id(0)
    u = cc_ref[...].astype(F32) * cx_ref[...].astype(F32)
    row = lax.broadcasted_iota(jnp.int32, (TM, 1), 0)
    g = row + i * TM
    first = functools.reduce(jnp.logical_or, [g == s for s in STREAM_STARTS])
    last = functools.reduce(jnp.logical_or, [g == s - 1 for s in STREAM_STARTS[1:] + (ROWS,)])
    halo_prev = pcc_ref[HALO - 1:HALO, :].astype(F32) * pcx_ref[HALO - 1:HALO, :].astype(F32)
    halo_next = ncc_ref[0:1, :].astype(F32) * ncx_ref[0:1, :].astype(F32)
    up = jnp.where(row == 0, halo_prev, pltpu.roll(u, 1, 0))
    un = jnp.where(row == TM - 1, halo_next, pltpu.roll(u, TM - 1, 0))
    up = jnp.where(first, 0.0, up)
    un = jnp.where(last, 0.0, un)
    w = w_ref[...]
    y = up * w[0:1] + u * w[1:2] + un * w[2:3]
    o_ref[...] = (cb_ref[...].astype(F32) * y).astype(BF16)


def _short_conv(p, conv_w, layer, n_tiles):
    cb0 = PC_CX // CONV_W
    per = TM // HALO
    prev_map = lambda c: (lambda i: (jnp.maximum(i * per - 1, 0), c))
    next_map = lambda c: (lambda i: (jnp.minimum((i + 1) * per, ROWS // HALO - 1), c))
    return pl.pallas_call(
        _conv_kernel,
        grid=(n_tiles,),
        in_specs=[
            pl.BlockSpec((TM, CONV_W), lambda i: (i, cb0)),
            pl.BlockSpec((TM, CONV_W), lambda i: (i, cb0 + 1)),
            pl.BlockSpec((TM, CONV_W), lambda i: (i, cb0 + 2)),
            pl.BlockSpec((HALO, CONV_W), prev_map(cb0)),
            pl.BlockSpec((HALO, CONV_W), prev_map(cb0 + 2)),
            pl.BlockSpec((HALO, CONV_W), next_map(cb0)),
            pl.BlockSpec((HALO, CONV_W), next_map(cb0 + 2)),
            pl.BlockSpec((None, 8, CONV_W), lambda i: (layer, 0, 0)),
        ],
        out_specs=pl.BlockSpec((TM, CONV_W), lambda i: (i, 0)),
        out_shape=jax.ShapeDtypeStruct((n_tiles * TM, CONV_W), BF16),
        name="short_conv",
    )(p, p, p, p, p, p, p, conv_w)


MERGE_TN = 1024


def _merge_kernel(fl_ref, fc_ref, a_ref, c_ref, wf_ref, wm_ref, wc_ref, g0_ref, g1_ref, g2_ref, o_ref):
    i = pl.program_id(0)
    f = jnp.where(i == LAT_TILES, fc_ref[...], fl_ref[...])
    y_f = jnp.dot(f, wf_ref[...], preferred_element_type=F32)
    y_m = jnp.dot(a_ref[...], wm_ref[...], preferred_element_type=F32)
    y_c = jnp.dot(c_ref[...], wc_ref[...], preferred_element_type=F32)
    merged = (g0_ref[...].astype(F32) * y_f + g1_ref[...].astype(F32) * y_m
              + g2_ref[...].astype(F32) * y_c)
    o_ref[...] = merged.astype(BF16)


def _merge(f_lat, f_ctx, attn, cmix, p, layer, w_f, w_m, w_c, n_tiles):
    gb = PC_G // MERGE_TN
    per = D_MODEL // MERGE_TN
    return pl.pallas_call(
        _merge_kernel,
        grid=(n_tiles, per),
        in_specs=[
            pl.BlockSpec((TM, FOURIER_W), lambda i, n: (jnp.minimum(i, LAT_TILES - 1), 0)),
            pl.BlockSpec((TM, FOURIER_W), lambda i, n: (0, 0)),
            pl.BlockSpec((TM, MLA_HEADS * V_HEAD), lambda i, n: (i, 0)),
            pl.BlockSpec((TM, CONV_W), lambda i, n: (i, 0)),
            pl.BlockSpec((None, FOURIER_W, MERGE_TN), lambda i, n: (layer, 0, n)),
            pl.BlockSpec((None, MLA_HEADS * V_HEAD, MERGE_TN), lambda i, n: (layer, 0, n)),
            pl.BlockSpec((None, CONV_W, MERGE_TN), lambda i, n: (layer, 0, n)),
            pl.BlockSpec((TM, MERGE_TN), lambda i, n: (i, gb + n)),
            pl.BlockSpec((TM, MERGE_TN), lambda i, n: (i, gb + per + n)),
            pl.BlockSpec((TM, MERGE_TN), lambda i, n: (i, gb + 2 * per + n)),
        ],
        out_specs=pl.BlockSpec((TM, MERGE_TN), lambda i, n: (i, n)),
        out_shape=jax.ShapeDtypeStruct((n_tiles * TM, D_MODEL), BF16),
        name="merge",
    )(f_lat, f_ctx, attn, cmix, w_f, w_m, w_c, p, p, p)


def _outproj_kernel(m_ref, w_ref, x_ref, gate_ref, o_ref):
    acc = jnp.dot(m_ref[...], w_ref[...], preferred_element_type=F32)
    o_ref[...] = x_ref[...] + gate_ref[...] * acc


def _outproj(merged, w_out, x_all, mod, layer, n_tiles):
    return pl.pallas_call(
        _outproj_kernel,
        grid=(n_tiles, D_MODEL // MERGE_TN),
        in_specs=[
            pl.BlockSpec((TM, D_MODEL), lambda i, n: (i, 0)),
            pl.BlockSpec((None, D_MODEL, MERGE_TN), lambda i, n: (layer, 0, n)),
            pl.BlockSpec((TM, MERGE_TN), lambda i, n: (i, n)),
            _mod_spec(layer, 2, MERGE_TN, col=True),
        ],
        out_specs=pl.BlockSpec((TM, MERGE_TN), lambda i, n: (i, n)),
        out_shape=jax.ShapeDtypeStruct((n_tiles * TM, D_MODEL), F32),
        name="outproj",
    )(merged, w_out, x_all, mod)


FF_TN = 512


def _ffn_kernel(x_ref, gain_ref, shift_ref, scale_ref, gate_ref, wg_ref, wu_ref, wd_ref, o_ref,
                h_ref, acc_ref):
    f = pl.program_id(1)

    @pl.when(f == 0)
    def _():
        y = _rms(x_ref[...]) * gain_ref[...]
        h_ref[...] = (y * (1.0 + scale_ref[...]) + shift_ref[...]).astype(BF16)
        acc_ref[...] = jnp.zeros_like(acc_ref)

    h = h_ref[...]
    g = jnp.dot(h, wg_ref[...], preferred_element_type=F32)
    u = jnp.dot(h, wu_ref[...], preferred_element_type=F32)
    a = (g * _sigmoid(g) * u).astype(BF16)
    acc_ref[...] += jnp.dot(a, wd_ref[...], preferred_element_type=F32)

    @pl.when(f == pl.num_programs(1) - 1)
    def _():
        o_ref[...] = x_ref[...] + gate_ref[...] * acc_ref[...]


def _ffn(x_all, gain, mod, layer, w_gate, w_up, w_down, n_tiles):
    return pl.pallas_call(
        _ffn_kernel,
        grid=(n_tiles, D_FF // FF_TN),
        in_specs=[
            pl.BlockSpec((TM, D_MODEL), lambda i, f: (i, 0)),
            pl.BlockSpec((None, 1, D_MODEL), lambda i, f: (layer, 0, 0)),
            _mod_spec(layer, 3),
            _mod_spec(layer, 4),
            _mod_spec(layer, 5),
            pl.BlockSpec((None, D_MODEL, FF_TN), lambda i, f: (layer, 0, f)),
            pl.BlockSpec((None, D_MODEL, FF_TN), lambda i, f: (layer, 0, f)),
            pl.BlockSpec((None, FF_TN, D_MODEL), lambda i, f: (layer, f, 0)),
        ],
        out_specs=pl.BlockSpec((TM, D_MODEL), lambda i, f: (i, 0)),
        out_shape=jax.ShapeDtypeStruct((n_tiles * TM, D_MODEL), F32),
        scratch_shapes=[pltpu.VMEM((TM, D_MODEL), BF16), pltpu.VMEM((TM, D_MODEL), F32)],
        compiler_params=pltpu.CompilerParams(dimension_semantics=("parallel", "arbitrary")),
        name="ffn",
    )(x_all, gain, mod, mod, mod, w_gate, w_up, w_down)


def _dft_cos_sin(length):
    kn = np.outer(np.arange(length), np.arange(length)) % length
    ang = 2.0 * np.pi * kn / length
    return np.cos(ang), np.sin(ang)


def _chan_dft_matrix():
    c, s = _dft_cos_sin(FOURIER_GROUP_W)
    eye = np.eye(FOURIER_GROUPS)
    return jnp.asarray(np.concatenate([np.kron(eye, c), np.kron(eye, s)], axis=1), F32).astype(BF16)


def _ctx_dft_matrix():
    c, s = _dft_cos_sin(CTX_LEN)
    norm = (CTX_LEN * FOURIER_GROUP_W) ** -0.5
    return jnp.asarray(np.concatenate([c, -s], axis=1) * norm, F32).astype(BF16)


def _lat_dft_matrix():
    r = 64
    n = np.arange(SEQ)
    hi = 2.0 * np.pi * (np.outer(np.arange(r), n) % r) / r
    lo = 2.0 * np.pi * (np.outer(np.arange(r), n) % SEQ) / SEQ
    norm = (SEQ * FOURIER_GROUP_W) ** -0.5
    tab = lambda t: jnp.asarray(t, F32)
    hi_c = jnp.concatenate([tab(np.cos(hi) * norm)] * 2, axis=1)[:, None, :]
    hi_s = jnp.concatenate([tab(np.sin(hi) * norm)] * 2, axis=1)[:, None, :]
    lo_a = jnp.concatenate([tab(np.cos(lo)), tab(-np.sin(lo))], axis=1)[None, :, :]
    lo_b = jnp.concatenate([tab(-np.sin(lo)), tab(-np.cos(lo))], axis=1)[None, :, :]
    return (hi_c * lo_a + hi_s * lo_b).astype(BF16).reshape(SEQ, 2 * SEQ)


def _rope_tables():
    rows = SEQ // GRID_W
    row = jnp.repeat(jnp.arange(rows), GRID_W)
    col = jnp.tile(jnp.arange(GRID_W), rows)
    inv_freq = ROPE_THETA ** (-jnp.arange(AXIS_PAIRS, dtype=F32) / AXIS_PAIRS)
    ang = jnp.concatenate([row[:, None] * inv_freq, col[:, None] * inv_freq], axis=-1)
    cos, sin = jnp.cos(ang), jnp.sin(ang)
    zeros = jnp.zeros((SEQ, QK_NOPE - QK_ROPE), F32)
    ca = jnp.concatenate([cos, cos, zeros], axis=-1)
    sb = jnp.concatenate([-sin, sin, zeros], axis=-1)
    ca = jnp.concatenate([ca, ca, jnp.ones((CTX_ROWS, QK_NOPE), F32)], axis=0)
    sb = jnp.concatenate([sb, sb, jnp.zeros((CTX_ROWS, QK_NOPE), F32)], axis=0)
    return ca, sb


def _pack_w_uq(w_uq):
    w = w_uq.reshape(DEPTH, Q_RANK, MLA_HEADS, QK_HEAD)
    ra, rb = _rope_split(w[..., QK_NOPE:])
    return jnp.concatenate([w[..., :QK_NOPE], ra, rb], axis=-1).reshape(DEPTH, Q_RANK, MLA_HEADS * Q_EXT).astype(BF16)


def _pack_head_gain(g):
    ga, gb = _rope_split(g[:, QK_NOPE:])
    rows = jnp.stack([g[:, :QK_NOPE], ga, gb], axis=1)
    return jnp.concatenate([rows, jnp.zeros((DEPTH, 5, QK_NOPE), F32)], axis=1)


def kernel(x, c, ctx, c_ctx, w_ada, b_ada, norm_mix, norm_ffn, w_in, b_gate, q_a_norm, kv_a_norm, w_uq, w_ukv,
           q_norm, k_norm, w_f_out, w_mla_out, conv_w, w_conv_out, w_out, w_ffn_gate, w_ffn_up, w_ffn_down):
    x_all = jnp.concatenate([x.reshape(LAT_ROWS, D_MODEL), ctx.reshape(CTX_ROWS, D_MODEL)], axis=0)
    cond = jnp.concatenate([c, c_ctx[None, :], jnp.zeros((MOD_ROWS - BATCH - 1, D_MODEL), F32)], axis=0)
    bias_pack = jnp.concatenate([jnp.zeros((DEPTH, 1, PC_G), F32), b_gate[:, None, :]], axis=-1)
    wq_ext = _pack_w_uq(w_uq)
    gq = _pack_head_gain(q_norm)
    gk = _pack_head_gain(k_norm)
    conv_w8 = jnp.concatenate([conv_w, jnp.zeros((DEPTH, 5, CONV_W), F32)], axis=1)
    rope_ca, rope_sb = _rope_tables()
    w_chan = _chan_dft_matrix()
    dft_lat = _lat_dft_matrix()
    dft_ctx = _ctx_dft_matrix()
    bf = lambda w: w.astype(BF16)
    w_ukv_b, w_f_b, w_m_b, w_c_b, w_out_b = bf(w_ukv), bf(w_f_out), bf(w_mla_out), bf(w_conv_out), bf(w_out)
    w_g_b, w_u_b, w_d_b = bf(w_ffn_gate), bf(w_ffn_up), bf(w_ffn_down)
    gain_mix, gain_ffn = norm_mix[:, None, :], norm_ffn[:, None, :]
    qa, kva = q_a_norm[:, None, :], kv_a_norm[:, None, :]

    w_pack = _pack_w_in(w_in)
    mod = _ada(cond, w_ada, b_ada).reshape(DEPTH * MOD_ROWS * N_MOD, 1, D_MODEL)

    for l in range(DEPTH):
        last = l == DEPTH - 1
        n_tiles = LAT_TILES if last else N_TILES
        p = _inproj(x_all, gain_mix, mod, l, w_pack, bias_pack)
        q, k, v = _qkv(p, l, wq_ext, w_ukv_b, qa, kva, gq, gk, rope_ca, rope_sb)
        attn = _attention(q, k, v, with_ctx=not last)
        uv = _chan_dft(p, w_chan, n_tiles)
        f_lat = _pos_dft(dft_lat, uv, SEQ, 0, 1024, 2048)
        f_ctx = f_lat if last else _pos_dft(dft_ctx, uv, CTX_LEN, LAT_ROWS, CTX_LEN, CTX_LEN)
        cmix = _short_conv(p, conv_w8, l, n_tiles)
        merged = _merge(f_lat, f_ctx, attn, cmix, p, l, w_f_b, w_m_b, w_c_b, n_tiles)
        x_all = _outproj(merged, w_out_b, x_all, mod, l, n_tiles)
        x_all = _ffn(x_all, gain_ffn, mod, l, w_g_b, w_u_b, w_d_b, n_tiles)
    return x_all.reshape(BATCH, SEQ, D_MODEL)
```

```python
import functools

import numpy as np
import jax
import jax.numpy as jnp
from jax import lax
from jax.experimental import pallas as pl
from jax.experimental.pallas import tpu as pltpu

F32 = jnp.float32
BF16 = jnp.bfloat16

D_MODEL = 2048
BATCH = 2
SEQ = 4096
DEPTH = 2
GRID_W = 64
CTX_LEN = 256
FOURIER_GROUPS = 4
FOURIER_GROUP_W = 128
FOURIER_W = FOURIER_GROUPS * FOURIER_GROUP_W
MLA_HEADS = 8
Q_RANK = 512
KV_RANK = 256
QK_NOPE = 128
QK_ROPE = 64
QK_HEAD = QK_NOPE + QK_ROPE
V_HEAD = 128
ROPE_THETA = 10000.0
AXIS_PAIRS = QK_ROPE // 4
CONV_W = 512
N_BRANCH = 3
D_FF = ((8 * D_MODEL // 3 + 255) // 256) * 256
N_MOD = 6
RMS_EPS = 1e-6

OFF_F = 0
OFF_CQ = OFF_F + FOURIER_W
OFF_CKV = OFF_CQ + Q_RANK
OFF_KR = OFF_CKV + KV_RANK
OFF_CX = OFF_KR + QK_ROPE
OFF_G = OFF_CX + 3 * CONV_W
N_IN = OFF_G + N_BRANCH * D_MODEL

LAT_ROWS = BATCH * SEQ
CTX_ROWS = BATCH * CTX_LEN
ROWS = LAT_ROWS + CTX_ROWS
KEYS = SEQ + CTX_LEN
TM = 512
N_TILES = ROWS // TM
LAT_TILES = LAT_ROWS // TM
TILES_PER_BATCH = SEQ // TM
TQ = 256
LAT_QTILES = SEQ // TQ
MOD_ROWS = 8

P_TN = 1536
P_COLS = 6 * P_TN
PC_F = 0
PC_CQ = 512
PC_CKV = 1024
PC_CX = 1536
PC_G = 3072
HEAD_PAD = 256
Q_EXT = 384


def _rms(x, eps=RMS_EPS):
    return x * lax.rsqrt(jnp.mean(x * x, axis=-1, keepdims=True) + eps)


def _sigmoid(z):
    return 1.0 / (1.0 + jnp.exp(-z))


def _rope_split(w):
    half = QK_ROPE // 2
    pad = jnp.zeros(w.shape[:-1] + (QK_NOPE - QK_ROPE,), w.dtype)
    a = jnp.concatenate([w, pad], axis=-1)
    b = jnp.concatenate([w[..., half:], w[..., :half], pad], axis=-1)
    return a, b


ADA_TN = 1024


def _ada_kernel(a_ref, w_ref, b_ref, o_ref):
    a = a_ref[...]
    a = (a * _sigmoid(a)).astype(BF16)
    o_ref[...] = jnp.dot(a, w_ref[...].astype(BF16), preferred_element_type=F32) + b_ref[...]


def _ada(cond, w_ada, b_ada):
    n = N_MOD * D_MODEL
    return pl.pallas_call(
        _ada_kernel,
        grid=(DEPTH, n // ADA_TN),
        in_specs=[
            pl.BlockSpec((MOD_ROWS, D_MODEL), lambda l, j: (0, 0)),
            pl.BlockSpec((None, D_MODEL, ADA_TN), lambda l, j: (l, 0, j)),
            pl.BlockSpec((None, 1, ADA_TN), lambda l, j: (l, 0, j)),
        ],
        out_specs=pl.BlockSpec((None, MOD_ROWS, ADA_TN), lambda l, j: (l, 0, j)),
        out_shape=jax.ShapeDtypeStruct((DEPTH, MOD_ROWS, n), F32),
        name="ada",
    )(cond, w_ada, b_ada.reshape(DEPTH, 1, n))


def _mod_spec(layer, which, width=D_MODEL, col=False):
    base = layer * MOD_ROWS * N_MOD + which
    if col:
        return pl.BlockSpec((None, 1, width), lambda i, j: (base + (i // TILES_PER_BATCH) * N_MOD, 0, j))
    return pl.BlockSpec((None, 1, width), lambda i, j: (base + (i // TILES_PER_BATCH) * N_MOD, 0, 0))


PACK_STEP = P_TN - OFF_CX


def _pack_kernel(w_ref, o_ref):
    j = pl.program_id(1)
    half = QK_ROPE // 2

    @pl.when(j == 0)
    def _():
        o_ref[0:OFF_CX, :] = w_ref[0, 0:OFF_CX, :].astype(BF16)
        o_ref[OFF_CX:OFF_KR + QK_NOPE, :] = jnp.zeros((QK_NOPE - QK_ROPE, D_MODEL), BF16)
        o_ref[OFF_KR + QK_NOPE:OFF_KR + QK_NOPE + half, :] = w_ref[0, OFF_KR + half:OFF_CX, :].astype(BF16)
        o_ref[OFF_KR + QK_NOPE + half:OFF_KR + QK_NOPE + QK_ROPE, :] = w_ref[0, OFF_KR:OFF_KR + half, :].astype(BF16)
        o_ref[OFF_KR + QK_NOPE + QK_ROPE:P_TN, :] = jnp.zeros((QK_NOPE - QK_ROPE, D_MODEL), BF16)

    @pl.when(j > 0)
    def _():
        o_ref[...] = w_ref[0].astype(BF16)


def _pack_w_in(w_in_t):
    return pl.pallas_call(
        _pack_kernel,
        grid=(DEPTH, P_COLS // P_TN),
        in_specs=[pl.BlockSpec((pl.Element(1), pl.Element(P_TN), pl.Element(D_MODEL)),
                               lambda l, j: (l, jnp.maximum(j * (P_TN // PACK_STEP) - 1, 0) * PACK_STEP, 0))],
        out_specs=pl.BlockSpec((None, P_TN, D_MODEL), lambda l, j: (l, j, 0)),
        out_shape=jax.ShapeDtypeStruct((DEPTH, P_COLS, D_MODEL), BF16),
        name="pack_w_in",
    )(w_in_t)


def _inproj_kernel(x_ref, gain_ref, shift_ref, scale_ref, w_ref, bias_ref, o_ref, h_ref):
    j = pl.program_id(1)

    @pl.when(j == 0)
    def _():
        y = _rms(x_ref[...]) * gain_ref[...]
        h_ref[...] = (y * (1.0 + scale_ref[...]) + shift_ref[...]).astype(BF16)

    acc = lax.dot_general(h_ref[...], w_ref[...], (((1,), (1,)), ((), ())), preferred_element_type=F32)

    @pl.when(j < PC_G // P_TN)
    def _():
        o_ref[...] = acc.astype(BF16)

    @pl.when(j >= PC_G // P_TN)
    def _():
        o_ref[...] = _sigmoid(acc + bias_ref[...]).astype(BF16)


def _inproj(x_all, gain, mod, layer, w_pack, bias_pack):
    return pl.pallas_call(
        _inproj_kernel,
        grid=(N_TILES, P_COLS // P_TN),
        in_specs=[
            pl.BlockSpec((TM, D_MODEL), lambda i, j: (i, 0)),
            pl.BlockSpec((None, 1, D_MODEL), lambda i, j: (layer, 0, 0)),
            _mod_spec(layer, 0),
            _mod_spec(layer, 1),
            pl.BlockSpec((None, P_TN, D_MODEL), lambda i, j: (layer, j, 0)),
            pl.BlockSpec((None, 1, P_TN), lambda i, j: (layer, 0, j)),
        ],
        out_specs=pl.BlockSpec((TM, P_TN), lambda i, j: (i, j)),
        out_shape=jax.ShapeDtypeStruct((ROWS, P_COLS), BF16),
        scratch_shapes=[pltpu.VMEM((TM, D_MODEL), BF16)],
        compiler_params=pltpu.CompilerParams(dimension_semantics=("parallel", "arbitrary")),
        name="inproj",
    )(x_all, gain, mod, mod, w_pack, bias_pack)


def _qkv_kernel(cq_ref, ckv_ref, wq_ref, wkv_ref, qa_ref, kva_ref, gq_ref, gk_ref, ca_ref, sb_ref,
                q_ref, k_ref, v_ref):
    ca = ca_ref[...]
    sb = sb_ref[...]
    scale = QK_HEAD ** -0.5 * np.log2(np.e)
    ones_col = jnp.where(lax.broadcasted_iota(jnp.int32, (TQ, V_HEAD), 1) == 0, 1.0, 0.0).astype(BF16)

    cqn = (_rms(cq_ref[...].astype(F32)) * qa_ref[...]).astype(BF16)
    qe = jnp.dot(cqn, wq_ref[...], preferred_element_type=F32)
    gq = gq_ref[...]
    rot_a = gq[1:2] * ca
    rot_b = gq[2:3] * sb
    for h in range(MLA_HEADS):
        nope = qe[:, h * Q_EXT:h * Q_EXT + QK_NOPE]
        ra = qe[:, h * Q_EXT + QK_NOPE:h * Q_EXT + 2 * QK_NOPE]
        rb = qe[:, h * Q_EXT + 2 * QK_NOPE:(h + 1) * Q_EXT]
        ssq = jnp.sum(nope * nope, axis=-1, keepdims=True) + jnp.sum(ra * ra, axis=-1, keepdims=True)
        r = lax.rsqrt(ssq * (1.0 / QK_HEAD) + RMS_EPS) * scale
        q_ref[h, :, 0:QK_NOPE] = (nope * gq[0:1] * r).astype(BF16)
        q_ref[h, :, QK_NOPE:HEAD_PAD] = ((ra * rot_a + rb * rot_b) * r).astype(BF16)

    c2 = ckv_ref[...].astype(F32)
    ckvn = (_rms(c2[:, 0:KV_RANK]) * kva_ref[...]).astype(BF16)
    kv = jnp.dot(ckvn, wkv_ref[...], preferred_element_type=F32)
    kra = c2[:, KV_RANK:KV_RANK + QK_NOPE]
    krb = c2[:, KV_RANK + QK_NOPE:KV_RANK + 2 * QK_NOPE]
    gk = gk_ref[...]
    krot = kra * (gk[1:2] * ca) + krb * (gk[2:3] * sb)
    ssq_r = jnp.sum(kra * kra, axis=-1, keepdims=True)
    for h in range(MLA_HEADS):
        knope = kv[:, h * 2 * QK_NOPE:h * 2 * QK_NOPE + QK_NOPE]
        ssq = jnp.sum(knope * knope, axis=-1, keepdims=True) + ssq_r
        r = lax.rsqrt(ssq * (1.0 / QK_HEAD) + RMS_EPS)
        k_ref[0, h, :, 0:QK_NOPE] = (knope * gk[0:1] * r).astype(BF16)
        k_ref[0, h, :, QK_NOPE:HEAD_PAD] = (krot * r).astype(BF16)
        v_ref[0, h, :, 0:V_HEAD] = kv[:, h * 2 * QK_NOPE + QK_NOPE:(h + 1) * 2 * QK_NOPE].astype(BF16)
        v_ref[0, h, :, V_HEAD:2 * V_HEAD] = ones_col


def _kv_index(t):
    lat = t < BATCH * LAT_QTILES
    b = jnp.where(lat, t // LAT_QTILES, t - BATCH * LAT_QTILES)
    pos = jnp.where(lat, t % LAT_QTILES, LAT_QTILES)
    return b, pos


def _qkv(p, layer, wq_ext, w_ukv, qa, kva, gq, gk, rope_ca, rope_sb):
    def kv_map(t):
        b, pos = _kv_index(t)
        return (b, 0, pos, 0)

    def per_layer(shape):
        return pl.BlockSpec((None,) + shape, lambda t: (layer,) + (0,) * len(shape))

    return pl.pallas_call(
        _qkv_kernel,
        grid=(ROWS // TQ,),
        in_specs=[
            pl.BlockSpec((TQ, Q_RANK), lambda t: (t, PC_CQ // Q_RANK)),
            pl.BlockSpec((TQ, 512), lambda t: (t, PC_CKV // 512)),
            per_layer((Q_RANK, MLA_HEADS * Q_EXT)),
            per_layer((KV_RANK, MLA_HEADS * 2 * QK_NOPE)),
            per_layer((1, Q_RANK)),
            per_layer((1, KV_RANK)),
            per_layer((8, QK_NOPE)),
            per_layer((8, QK_NOPE)),
            pl.BlockSpec((TQ, QK_NOPE), lambda t: (t, 0)),
            pl.BlockSpec((TQ, QK_NOPE), lambda t: (t, 0)),
        ],
        out_specs=[
            pl.BlockSpec((MLA_HEADS, TQ, HEAD_PAD), lambda t: (0, t, 0)),
            pl.BlockSpec((1, MLA_HEADS, TQ, HEAD_PAD), kv_map),
            pl.BlockSpec((1, MLA_HEADS, TQ, 2 * V_HEAD), kv_map),
        ],
        out_shape=[
            jax.ShapeDtypeStruct((MLA_HEADS, ROWS, HEAD_PAD), BF16),
            jax.ShapeDtypeStruct((BATCH, MLA_HEADS, KEYS, HEAD_PAD), BF16),
            jax.ShapeDtypeStruct((BATCH, MLA_HEADS, KEYS, 2 * V_HEAD), BF16),
        ],
        name="qkv",
    )(p, p, wq_ext, w_ukv, qa, kva, gq, gk, rope_ca, rope_sb)


def _attn_kernel(q_ref, k_ref, v_ref, o_ref, *, n_sub):
    k = k_ref[...]
    v = v_ref[...]
    for t in range(n_sub):
        q = q_ref[t * TQ:(t + 1) * TQ, :]
        s = lax.dot_general(q, k, (((1,), (1,)), ((), ())), preferred_element_type=F32)
        e = jnp.exp2(s - jnp.max(s, axis=-1, keepdims=True)).astype(BF16)
        o = jnp.dot(e, v, preferred_element_type=F32)
        o_ref[t * TQ:(t + 1) * TQ, :] = (o[:, 0:V_HEAD] / o[:, V_HEAD:V_HEAD + 1]).astype(BF16)


ATT_SUB = 4


def _attention_lat(q, k, v):
    tq = ATT_SUB * TQ
    per_batch = SEQ // tq
    return pl.pallas_call(
        functools.partial(_attn_kernel, n_sub=ATT_SUB),
        grid=(BATCH, MLA_HEADS, per_batch),
        in_specs=[
            pl.BlockSpec((None, tq, HEAD_PAD), lambda b, h, r: (h, b * per_batch + r, 0)),
            pl.BlockSpec((None, None, KEYS, HEAD_PAD), lambda b, h, r: (b, h, 0, 0)),
            pl.BlockSpec((None, None, KEYS, 2 * V_HEAD), lambda b, h, r: (b, h, 0, 0)),
        ],
        out_specs=pl.BlockSpec((tq, V_HEAD), lambda b, h, r: (b * per_batch + r, h)),
        out_shape=jax.ShapeDtypeStruct((LAT_ROWS, MLA_HEADS * V_HEAD), BF16),
        compiler_params=pltpu.CompilerParams(dimension_semantics=("parallel", "parallel", "arbitrary")),
        name="attention_lat",
    )(q, k, v)


def _attention_ctx(q, k, v):
    return pl.pallas_call(
        functools.partial(_attn_kernel, n_sub=1),
        grid=(BATCH, MLA_HEADS),
        in_specs=[
            pl.BlockSpec((None, CTX_LEN, HEAD_PAD), lambda b, h: (h, LAT_ROWS // CTX_LEN + b, 0)),
            pl.BlockSpec((None, None, CTX_LEN, HEAD_PAD), lambda b, h: (b, h, SEQ // CTX_LEN, 0)),
            pl.BlockSpec((None, None, CTX_LEN, 2 * V_HEAD), lambda b, h: (b, h, SEQ // CTX_LEN, 0)),
        ],
        out_specs=pl.BlockSpec((CTX_LEN, V_HEAD), lambda b, h: (b, h)),
        out_shape=jax.ShapeDtypeStruct((CTX_ROWS, MLA_HEADS * V_HEAD), BF16),
        name="attention_ctx",
    )(q, k, v)


def _chan_dft_kernel(pf_ref, w_ref, o_ref):
    uv = jnp.dot(pf_ref[...], w_ref[...], preferred_element_type=F32)
    o_ref[0] = uv[:, 0:FOURIER_W].astype(BF16)
    o_ref[1] = uv[:, FOURIER_W:2 * FOURIER_W].astype(BF16)


def _chan_dft(p, w_chan, n_tiles):
    return pl.pallas_call(
        _chan_dft_kernel,
        grid=(n_tiles,),
        in_specs=[
            pl.BlockSpec((TM, FOURIER_W), lambda i: (i, PC_F // FOURIER_W)),
            pl.BlockSpec((FOURIER_W, 2 * FOURIER_W), lambda i: (0, 0)),
        ],
        out_specs=pl.BlockSpec((2, TM, FOURIER_W), lambda i: (0, i, 0)),
        out_shape=jax.ShapeDtypeStruct((2, n_tiles * TM, FOURIER_W), BF16),
        name="chan_dft",
    )(p, w_chan)


def _pos_dft_kernel(d_ref, uv0_ref, uv1_ref, o_ref, acc_ref):
    k = pl.program_id(1)

    @pl.when(k == 0)
    def _():
        acc_ref[...] = jnp.zeros_like(acc_ref)

    d = d_ref[...]
    acc_ref[0] += jnp.dot(d, uv0_ref[...], preferred_element_type=F32)
    acc_ref[1] += jnp.dot(d, uv1_ref[...], preferred_element_type=F32)

    @pl.when(k == pl.num_programs(1) - 1)
    def _():
        o_ref[...] = acc_ref[...].astype(BF16)


def _pos_dft(dft, uv, length, row0, tm, tk):
    nk_half = length // tk
    kb0 = row0 // tk

    def uv_spec(b):
        return pl.BlockSpec((None, tk, FOURIER_W),
                            lambda m, k: (k // nk_half, kb0 + b * nk_half + k % nk_half, 0))

    out = pl.pallas_call(
        _pos_dft_kernel,
        grid=(length // tm, 2 * nk_half),
        in_specs=[pl.BlockSpec((tm, tk), lambda m, k: (m, k)), uv_spec(0), uv_spec(1)],
        out_specs=pl.BlockSpec((BATCH, tm, FOURIER_W), lambda m, k: (0, m, 0)),
        out_shape=jax.ShapeDtypeStruct((BATCH, length, FOURIER_W), BF16),
        scratch_shapes=[pltpu.VMEM((BATCH, tm, FOURIER_W), F32)],
        compiler_params=pltpu.CompilerParams(dimension_semantics=("parallel", "arbitrary")),
        name="pos_dft_%d" % length,
    )(dft, uv, uv)
    return out.reshape(BATCH * length, FOURIER_W)


HALO = 16
STREAM_STARTS = (0, SEQ, LAT_ROWS, LAT_ROWS + CTX_LEN)


def _conv_kernel(cx_ref, cb_ref, cc_ref, pcx_ref, pcc_ref, ncx_ref, ncc_ref, w_ref, o_ref):
    i = pl.program_id(0)
    u = cc_ref[...].astype(F32) * cx_ref[...].astype(F32)
    row = lax.broadcasted_iota(jnp.int32, (TM, 1), 0)
    g = row + i * TM
    first = functools.reduce(jnp.logical_or, [g == s for s in STREAM_STARTS])
    last = functools.reduce(jnp.logical_or, [g == s - 1 for s in STREAM_STARTS[1:] + (ROWS,)])
    halo_prev = pcc_ref[HALO - 1:HALO, :].astype(F32) * pcx_ref[HALO - 1:HALO, :].astype(F32)
    halo_next = ncc_ref[0:1, :].astype(F32) * ncx_ref[0:1, :].astype(F32)
    up = jnp.where(row == 0, halo_prev, pltpu.roll(u, 1, 0))
    un = jnp.where(row == TM - 1, halo_next, pltpu.roll(u, TM - 1, 0))
    up = jnp.where(first, 0.0, up)
    un = jnp.where(last, 0.0, un)
    w = w_ref[...]
    y = up * w[0:1] + u * w[1:2] + un * w[2:3]
    o_ref[...] = (cb_ref[...].astype(F32) * y).astype(BF16)


def _short_conv(p, conv_w, layer, n_tiles):
    cb0 = PC_CX // CONV_W
    per = TM // HALO
    prev_map = lambda c: (lambda i: (jnp.maximum(i * per - 1, 0), c))
    next_map = lambda c: (lambda i: (jnp.minimum((i + 1) * per, ROWS // HALO - 1), c))
    return pl.pallas_call(
        _conv_kernel,
        grid=(n_tiles,),
        in_specs=[
            pl.BlockSpec((TM, CONV_W), lambda i: (i, cb0)),
            pl.BlockSpec((TM, CONV_W), lambda i: (i, cb0 + 1)),
            pl.BlockSpec((TM, CONV_W), lambda i: (i, cb0 + 2)),
            pl.BlockSpec((HALO, CONV_W), prev_map(cb0)),
            pl.BlockSpec((HALO, CONV_W), prev_map(cb0 + 2)),
            pl.BlockSpec((HALO, CONV_W), next_map(cb0)),
            pl.BlockSpec((HALO, CONV_W), next_map(cb0 + 2)),
            pl.BlockSpec((None, 8, CONV_W), lambda i: (layer, 0, 0)),
        ],
        out_specs=pl.BlockSpec((TM, CONV_W), lambda i: (i, 0)),
        out_shape=jax.ShapeDtypeStruct((n_tiles * TM, CONV_W), BF16),
        name="short_conv",
    )(p, p, p, p, p, p, p, conv_w)


MERGE_TN = 1024


def _merge_kernel(fl_ref, fc_ref, al_ref, ac_ref, c_ref, wf_ref, wm_ref, wc_ref, g0_ref, g1_ref, g2_ref, o_ref):
    is_ctx = pl.program_id(0) == LAT_TILES
    f = jnp.where(is_ctx, fc_ref[...], fl_ref[...])
    a = jnp.where(is_ctx, ac_ref[...], al_ref[...])
    y_f = jnp.dot(f, wf_ref[...], preferred_element_type=F32)
    y_m = jnp.dot(a, wm_ref[...], preferred_element_type=F32)
    y_c = jnp.dot(c_ref[...], wc_ref[...], preferred_element_type=F32)
    merged = (g0_ref[...].astype(F32) * y_f + g1_ref[...].astype(F32) * y_m
              + g2_ref[...].astype(F32) * y_c)
    o_ref[...] = merged.astype(BF16)


def _merge(f_lat, f_ctx, a_lat, a_ctx, cmix, p, layer, w_f, w_m, w_c, n_tiles):
    gb = PC_G // MERGE_TN
    per = D_MODEL // MERGE_TN
    return pl.pallas_call(
        _merge_kernel,
        grid=(n_tiles, per),
        in_specs=[
            pl.BlockSpec((TM, FOURIER_W), lambda i, n: (jnp.minimum(i, LAT_TILES - 1), 0)),
            pl.BlockSpec((TM, FOURIER_W), lambda i, n: (0, 0)),
            pl.BlockSpec((TM, MLA_HEADS * V_HEAD), lambda i, n: (jnp.minimum(i, LAT_TILES - 1), 0)),
            pl.BlockSpec((TM, MLA_HEADS * V_HEAD), lambda i, n: (0, 0)),
            pl.BlockSpec((TM, CONV_W), lambda i, n: (i, 0)),
            pl.BlockSpec((None, FOURIER_W, MERGE_TN), lambda i, n: (layer, 0, n)),
            pl.BlockSpec((None, MLA_HEADS * V_HEAD, MERGE_TN), lambda i, n: (layer, 0, n)),
            pl.BlockSpec((None, CONV_W, MERGE_TN), lambda i, n: (layer, 0, n)),
            pl.BlockSpec((TM, MERGE_TN), lambda i, n: (i, gb + n)),
            pl.BlockSpec((TM, MERGE_TN), lambda i, n: (i, gb + per + n)),
            pl.BlockSpec((TM, MERGE_TN), lambda i, n: (i, gb + 2 * per + n)),
        ],
        out_specs=pl.BlockSpec((TM, MERGE_TN), lambda i, n: (i, n)),
        out_shape=jax.ShapeDtypeStruct((n_tiles * TM, D_MODEL), BF16),
        name="merge",
    )(f_lat, f_ctx, a_lat, a_ctx, cmix, w_f, w_m, w_c, p, p, p)


def _outproj_kernel(m_ref, w_ref, x_ref, gate_ref, o_ref):
    acc = jnp.dot(m_ref[...], w_ref[...], preferred_element_type=F32)
    o_ref[...] = x_ref[...] + gate_ref[...] * acc


def _outproj(merged, w_out, x_all, mod, layer, n_tiles):
    return pl.pallas_call(
        _outproj_kernel,
        grid=(n_tiles, D_MODEL // MERGE_TN),
        in_specs=[
            pl.BlockSpec((TM, D_MODEL), lambda i, n: (i, 0)),
            pl.BlockSpec((None, D_MODEL, MERGE_TN), lambda i, n: (layer, 0, n)),
            pl.BlockSpec((TM, MERGE_TN), lambda i, n: (i, n)),
            _mod_spec(layer, 2, MERGE_TN, col=True),
        ],
        out_specs=pl.BlockSpec((TM, MERGE_TN), lambda i, n: (i, n)),
        out_shape=jax.ShapeDtypeStruct((n_tiles * TM, D_MODEL), F32),
        name="outproj",
    )(merged, w_out, x_all, mod)


FF_TN = 512


def _ffn_kernel(x_ref, gain_ref, shift_ref, scale_ref, gate_ref, wg_ref, wu_ref, wd_ref, o_ref,
                h_ref, acc_ref):
    f = pl.program_id(1)

    @pl.when(f == 0)
    def _():
        y = _rms(x_ref[...]) * gain_ref[...]
        h_ref[...] = (y * (1.0 + scale_ref[...]) + shift_ref[...]).astype(BF16)
        acc_ref[...] = jnp.zeros_like(acc_ref)

    h = h_ref[...]
    g = jnp.dot(h, wg_ref[...], preferred_element_type=F32)
    u = jnp.dot(h, wu_ref[...], preferred_element_type=F32)
    a = (g * _sigmoid(g) * u).astype(BF16)
    acc_ref[...] += jnp.dot(a, wd_ref[...], preferred_element_type=F32)

    @pl.when(f == pl.num_programs(1) - 1)
    def _():
        o_ref[...] = x_ref[...] + gate_ref[...] * acc_ref[...]


def _ffn(x_all, gain, mod, layer, w_gate, w_up, w_down, n_tiles):
    return pl.pallas_call(
        _ffn_kernel,
        grid=(n_tiles, D_FF // FF_TN),
        in_specs=[
            pl.BlockSpec((TM, D_MODEL), lambda i, f: (i, 0)),
            pl.BlockSpec((None, 1, D_MODEL), lambda i, f: (layer, 0, 0)),
            _mod_spec(layer, 3),
            _mod_spec(layer, 4),
            _mod_spec(layer, 5),
            pl.BlockSpec((None, D_MODEL, FF_TN), lambda i, f: (layer, 0, f)),
            pl.BlockSpec((None, D_MODEL, FF_TN), lambda i, f: (layer, 0, f)),
            pl.BlockSpec((None, FF_TN, D_MODEL), lambda i, f: (layer, f, 0)),
        ],
        out_specs=pl.BlockSpec((TM, D_MODEL), lambda i, f: (i, 0)),
        out_shape=jax.ShapeDtypeStruct((n_tiles * TM, D_MODEL), F32),
        scratch_shapes=[pltpu.VMEM((TM, D_MODEL), BF16), pltpu.VMEM((TM, D_MODEL), F32)],
        compiler_params=pltpu.CompilerParams(dimension_semantics=("parallel", "arbitrary")),
        name="ffn",
    )(x_all, gain, mod, mod, mod, w_gate, w_up, w_down)


def _dft_cos_sin(length):
    kn = np.outer(np.arange(length), np.arange(length)) % length
    ang = 2.0 * np.pi * kn / length
    return np.cos(ang), np.sin(ang)


def _chan_dft_matrix():
    c, s = _dft_cos_sin(FOURIER_GROUP_W)
    eye = np.eye(FOURIER_GROUPS)
    return jnp.asarray(np.concatenate([np.kron(eye, c), np.kron(eye, s)], axis=1), F32).astype(BF16)


def _ctx_dft_matrix():
    c, s = _dft_cos_sin(CTX_LEN)
    norm = (CTX_LEN * FOURIER_GROUP_W) ** -0.5
    return jnp.asarray(np.concatenate([c, -s], axis=1) * norm, F32).astype(BF16)


def _lat_dft_matrix():
    r = 64
    n = np.arange(SEQ)
    hi = 2.0 * np.pi * (np.outer(np.arange(r), n) % r) / r
    lo = 2.0 * np.pi * (np.outer(np.arange(r), n) % SEQ) / SEQ
    norm = (SEQ * FOURIER_GROUP_W) ** -0.5
    tab = lambda t: jnp.asarray(t, F32)
    hi_c = jnp.concatenate([tab(np.cos(hi) * norm)] * 2, axis=1)[:, None, :]
    hi_s = jnp.concatenate([tab(np.sin(hi) * norm)] * 2, axis=1)[:, None, :]
    lo_a = jnp.concatenate([tab(np.cos(lo)), tab(-np.sin(lo))], axis=1)[None, :, :]
    lo_b = jnp.concatenate([tab(-np.sin(lo)), tab(-np.cos(lo))], axis=1)[None, :, :]
    return (hi_c * lo_a + hi_s * lo_b).astype(BF16).reshape(SEQ, 2 * SEQ)


def _rope_tables():
    rows = SEQ // GRID_W
    row = jnp.repeat(jnp.arange(rows), GRID_W)
    col = jnp.tile(jnp.arange(GRID_W), rows)
    inv_freq = ROPE_THETA ** (-jnp.arange(AXIS_PAIRS, dtype=F32) / AXIS_PAIRS)
    ang = jnp.concatenate([row[:, None] * inv_freq, col[:, None] * inv_freq], axis=-1)
    cos, sin = jnp.cos(ang), jnp.sin(ang)
    zeros = jnp.zeros((SEQ, QK_NOPE - QK_ROPE), F32)
    ca = jnp.concatenate([cos, cos, zeros], axis=-1)
    sb = jnp.concatenate([-sin, sin, zeros], axis=-1)
    ca = jnp.concatenate([ca, ca, jnp.ones((CTX_ROWS, QK_NOPE), F32)], axis=0)
    sb = jnp.concatenate([sb, sb, jnp.zeros((CTX_ROWS, QK_NOPE), F32)], axis=0)
    return ca, sb


def _pack_w_uq(w_uq):
    w = w_uq.reshape(DEPTH, Q_RANK, MLA_HEADS, QK_HEAD)
    ra, rb = _rope_split(w[..., QK_NOPE:])
    return jnp.concatenate([w[..., :QK_NOPE], ra, rb], axis=-1).reshape(DEPTH, Q_RANK, MLA_HEADS * Q_EXT).astype(BF16)


def _pack_head_gain(g):
    ga, gb = _rope_split(g[:, QK_NOPE:])
    rows = jnp.stack([g[:, :QK_NOPE], ga, gb], axis=1)
    return jnp.concatenate([rows, jnp.zeros((DEPTH, 5, QK_NOPE), F32)], axis=1)


def kernel(x, c, ctx, c_ctx, w_ada, b_ada, norm_mix, norm_ffn, w_in, b_gate, q_a_norm, kv_a_norm, w_uq, w_ukv,
           q_norm, k_norm, w_f_out, w_mla_out, conv_w, w_conv_out, w_out, w_ffn_gate, w_ffn_up, w_ffn_down):
    x_all = jnp.concatenate([x.reshape(LAT_ROWS, D_MODEL), ctx.reshape(CTX_ROWS, D_MODEL)], axis=0)
    cond = jnp.concatenate([c, c_ctx[None, :], jnp.zeros((MOD_ROWS - BATCH - 1, D_MODEL), F32)], axis=0)
    bias_pack = jnp.concatenate([jnp.zeros((DEPTH, 1, PC_G), F32), b_gate[:, None, :]], axis=-1)
    wq_ext = _pack_w_uq(w_uq)
    gq = _pack_head_gain(q_norm)
    gk = _pack_head_gain(k_norm)
    conv_w8 = jnp.concatenate([conv_w, jnp.zeros((DEPTH, 5, CONV_W), F32)], axis=1)
    rope_ca, rope_sb = _rope_tables()
    w_chan = _chan_dft_matrix()
    dft_lat = _lat_dft_matrix()
    dft_ctx = _ctx_dft_matrix()
    bf = lambda w: w.astype(BF16)
    w_ukv_b, w_f_b, w_m_b, w_c_b, w_out_b = bf(w_ukv), bf(w_f_out), bf(w_mla_out), bf(w_conv_out), bf(w_out)
    w_g_b, w_u_b, w_d_b = bf(w_ffn_gate), bf(w_ffn_up), bf(w_ffn_down)
    gain_mix, gain_ffn = norm_mix[:, None, :], norm_ffn[:, None, :]
    qa, kva = q_a_norm[:, None, :], kv_a_norm[:, None, :]

    w_pack = _pack_w_in(jnp.swapaxes(w_in, 1, 2))
    mod = _ada(cond, w_ada, b_ada).reshape(DEPTH * MOD_ROWS * N_MOD, 1, D_MODEL)

    for l in range(DEPTH):
        last = l == DEPTH - 1
        n_tiles = LAT_TILES if last else N_TILES
        p = _inproj(x_all, gain_mix, mod, l, w_pack, bias_pack)
        q, k, v = _qkv(p, l, wq_ext, w_ukv_b, qa, kva, gq, gk, rope_ca, rope_sb)
        a_lat = _attention_lat(q, k, v)
        a_ctx = a_lat if last else _attention_ctx(q, k, v)
        uv = _chan_dft(p, w_chan, n_tiles)
        f_lat = _pos_dft(dft_lat, uv, SEQ, 0, 1024, 2048)
        f_ctx = f_lat if last else _pos_dft(dft_ctx, uv, CTX_LEN, LAT_ROWS, CTX_LEN, CTX_LEN)
        cmix = _short_conv(p, conv_w8, l, n_tiles)
        merged = _merge(f_lat, f_ctx, a_lat, a_ctx, cmix, p, l, w_f_b, w_m_b, w_c_b, n_tiles)
        x_all = _outproj(merged, w_out_b, x_all, mod, l, n_tiles)
        x_all = _ffn(x_all, gain_ffn, mod, l, w_g_b, w_u_b, w_d_b, n_tiles)
    return x_all.reshape(BATCH, SEQ, D_MODEL)
```

```python
import functools

import numpy as np
import jax
import jax.numpy as jnp
from jax import lax
from jax.experimental import pallas as pl
from jax.experimental.pallas import tpu as pltpu

F32 = jnp.float32
BF16 = jnp.bfloat16

D_MODEL = 2048
BATCH = 2
SEQ = 4096
DEPTH = 2
GRID_W = 64
CTX_LEN = 256
FOURIER_GROUPS = 4
FOURIER_GROUP_W = 128
FOURIER_W = FOURIER_GROUPS * FOURIER_GROUP_W
MLA_HEADS = 8
Q_RANK = 512
KV_RANK = 256
QK_NOPE = 128
QK_ROPE = 64
QK_HEAD = QK_NOPE + QK_ROPE
V_HEAD = 128
ROPE_THETA = 10000.0
AXIS_PAIRS = QK_ROPE // 4
CONV_W = 512
N_BRANCH = 3
D_FF = ((8 * D_MODEL // 3 + 255) // 256) * 256
N_MOD = 6
RMS_EPS = 1e-6

OFF_F = 0
OFF_CQ = OFF_F + FOURIER_W
OFF_CKV = OFF_CQ + Q_RANK
OFF_KR = OFF_CKV + KV_RANK
OFF_CX = OFF_KR + QK_ROPE
OFF_G = OFF_CX + 3 * CONV_W
N_IN = OFF_G + N_BRANCH * D_MODEL

LAT_ROWS = BATCH * SEQ
CTX_ROWS = BATCH * CTX_LEN
ROWS = LAT_ROWS + CTX_ROWS
KEYS = SEQ + CTX_LEN
TM = 512
N_TILES = ROWS // TM
LAT_TILES = LAT_ROWS // TM
TILES_PER_BATCH = SEQ // TM
TQ = 256
LAT_QTILES = SEQ // TQ
MOD_ROWS = 8

P_TN = 1536
P_CHUNK = 512
P_COLS = 6 * P_TN
PC_G = 0
N_GATE_TILES = N_BRANCH * D_MODEL // P_TN
PC_F = PC_G + N_BRANCH * D_MODEL
PC_CQ = PC_F + FOURIER_W
PC_CKV = PC_CQ + Q_RANK
PC_CX = PC_F + P_TN
HEAD_PAD = 256
Q_EXT = 384


def _rms(x, eps=RMS_EPS):
    return x * lax.rsqrt(jnp.mean(x * x, axis=-1, keepdims=True) + eps)


def _sigmoid(z):
    return 1.0 / (1.0 + jnp.exp(-z))


def _rope_split(w):
    half = QK_ROPE // 2
    pad = jnp.zeros(w.shape[:-1] + (QK_NOPE - QK_ROPE,), w.dtype)
    a = jnp.concatenate([w, pad], axis=-1)
    b = jnp.concatenate([w[..., half:], w[..., :half], pad], axis=-1)
    return a, b


ADA_TN = 1024


def _ada_kernel(a_ref, w_ref, b_ref, o_ref):
    a = a_ref[...]
    a = (a * _sigmoid(a)).astype(BF16)
    o_ref[...] = jnp.dot(a, w_ref[...].astype(BF16), preferred_element_type=F32) + b_ref[...]


def _ada(cond, w_ada, b_ada):
    n = N_MOD * D_MODEL
    return pl.pallas_call(
        _ada_kernel,
        grid=(DEPTH, n // ADA_TN),
        in_specs=[
            pl.BlockSpec((MOD_ROWS, D_MODEL), lambda l, j: (0, 0)),
            pl.BlockSpec((None, D_MODEL, ADA_TN), lambda l, j: (l, 0, j)),
            pl.BlockSpec((None, 1, ADA_TN), lambda l, j: (l, 0, j)),
        ],
        out_specs=pl.BlockSpec((None, MOD_ROWS, ADA_TN), lambda l, j: (l, 0, j)),
        out_shape=jax.ShapeDtypeStruct((DEPTH, MOD_ROWS, n), F32),
        name="ada",
    )(cond, w_ada, b_ada.reshape(DEPTH, 1, n))


def _mod_spec(layer, which, width=D_MODEL, col=False):
    base = layer * MOD_ROWS * N_MOD + which
    if col:
        return pl.BlockSpec((None, 1, width), lambda i, j: (base + (i // TILES_PER_BATCH) * N_MOD, 0, j))
    return pl.BlockSpec((None, 1, width), lambda i, j: (base + (i // TILES_PER_BATCH) * N_MOD, 0, 0))


PACK_STEP = P_TN - OFF_CX


def _pack_src_row(j):
    step = jnp.where(j < N_GATE_TILES, OFF_G // PACK_STEP + j * (P_TN // PACK_STEP),
                     jnp.where(j == N_GATE_TILES, 0, OFF_CX // PACK_STEP))
    return step * PACK_STEP


def _pack_kernel(w_ref, o_ref):
    j = pl.program_id(1)
    half = QK_ROPE // 2

    @pl.when(j == N_GATE_TILES)
    def _():
        o_ref[0:OFF_CX, :] = w_ref[0, 0:OFF_CX, :].astype(BF16)
        o_ref[OFF_CX:OFF_KR + QK_NOPE, :] = jnp.zeros((QK_NOPE - QK_ROPE, D_MODEL), BF16)
        o_ref[OFF_KR + QK_NOPE:OFF_KR + QK_NOPE + half, :] = w_ref[0, OFF_KR + half:OFF_CX, :].astype(BF16)
        o_ref[OFF_KR + QK_NOPE + half:OFF_KR + QK_NOPE + QK_ROPE, :] = w_ref[0, OFF_KR:OFF_KR + half, :].astype(BF16)
        o_ref[OFF_KR + QK_NOPE + QK_ROPE:P_TN, :] = jnp.zeros((QK_NOPE - QK_ROPE, D_MODEL), BF16)

    @pl.when(j != N_GATE_TILES)
    def _():
        o_ref[...] = w_ref[0].astype(BF16)


def _pack_w_in(w_in_t):
    return pl.pallas_call(
        _pack_kernel,
        grid=(DEPTH, P_COLS // P_TN),
        in_specs=[pl.BlockSpec((pl.Element(1), pl.Element(P_TN), pl.Element(D_MODEL)),
                               lambda l, j: (l, _pack_src_row(j), 0))],
        out_specs=pl.BlockSpec((None, P_TN, D_MODEL), lambda l, j: (l, j, 0)),
        out_shape=jax.ShapeDtypeStruct((DEPTH, P_COLS, D_MODEL), BF16),
        name="pack_w_in",
    )(w_in_t)


def _inproj_kernel(x_ref, gain_ref, shift_ref, scale_ref, w_ref, bias_ref, o_ref, h_ref):
    j = pl.program_id(1)

    @pl.when(j == 0)
    def _():
        y = _rms(x_ref[...]) * gain_ref[...]
        h_ref[...] = (y * (1.0 + scale_ref[...]) + shift_ref[...]).astype(BF16)

    is_gate = j < N_GATE_TILES
    h = h_ref[...]
    for c in range(P_TN // P_CHUNK):
        cols = pl.ds(c * P_CHUNK, P_CHUNK)
        z = lax.dot_general(h, w_ref[cols, :], (((1,), (1,)), ((), ())), preferred_element_type=F32)
        z = z + bias_ref[:, cols]
        o_ref[:, cols] = jnp.where(is_gate, _sigmoid(z), z).astype(BF16)


def _inproj(x_all, gain, mod, layer, w_pack, bias_pack):
    return pl.pallas_call(
        _inproj_kernel,
        grid=(N_TILES, P_COLS // P_TN),
        in_specs=[
            pl.BlockSpec((TM, D_MODEL), lambda i, j: (i, 0)),
            pl.BlockSpec((None, 1, D_MODEL), lambda i, j: (layer, 0, 0)),
            _mod_spec(layer, 0),
            _mod_spec(layer, 1),
            pl.BlockSpec((None, P_TN, D_MODEL), lambda i, j: (layer, j, 0)),
            pl.BlockSpec((None, 1, P_TN), lambda i, j: (layer, 0, j)),
        ],
        out_specs=pl.BlockSpec((TM, P_TN), lambda i, j: (i, j)),
        out_shape=jax.ShapeDtypeStruct((ROWS, P_COLS), BF16),
        scratch_shapes=[pltpu.VMEM((TM, D_MODEL), BF16)],
        compiler_params=pltpu.CompilerParams(dimension_semantics=("parallel", "arbitrary")),
        name="inproj",
    )(x_all, gain, mod, mod, w_pack, bias_pack)


def _qkv_kernel(cq_ref, ckv_ref, wq_ref, wkv_ref, qa_ref, kva_ref, gq_ref, gk_ref, ca_ref, sb_ref,
                q_ref, k_ref, v_ref):
    ca = ca_ref[...]
    sb = sb_ref[...]
    scale = QK_HEAD ** -0.5 * np.log2(np.e)
    ones_col = jnp.where(lax.broadcasted_iota(jnp.int32, (TQ, V_HEAD), 1) == 0, 1.0, 0.0).astype(BF16)

    cqn = (_rms(cq_ref[...].astype(F32)) * qa_ref[...]).astype(BF16)
    qe = jnp.dot(cqn, wq_ref[...], preferred_element_type=F32)
    gq = gq_ref[...]
    rot_a = gq[1:2] * ca
    rot_b = gq[2:3] * sb
    for h in range(MLA_HEADS):
        nope = qe[:, h * Q_EXT:h * Q_EXT + QK_NOPE]
        ra = qe[:, h * Q_EXT + QK_NOPE:h * Q_EXT + 2 * QK_NOPE]
        rb = qe[:, h * Q_EXT + 2 * QK_NOPE:(h + 1) * Q_EXT]
        ssq = jnp.sum(nope * nope, axis=-1, keepdims=True) + jnp.sum(ra * ra, axis=-1, keepdims=True)
        r = lax.rsqrt(ssq * (1.0 / QK_HEAD) + RMS_EPS) * scale
        q_ref[h, :, 0:QK_NOPE] = (nope * gq[0:1] * r).astype(BF16)
        q_ref[h, :, QK_NOPE:HEAD_PAD] = ((ra * rot_a + rb * rot_b) * r).astype(BF16)

    c2 = ckv_ref[...].astype(F32)
    ckvn = (_rms(c2[:, 0:KV_RANK]) * kva_ref[...]).astype(BF16)
    kv = jnp.dot(ckvn, wkv_ref[...], preferred_element_type=F32)
    kra = c2[:, KV_RANK:KV_RANK + QK_NOPE]
    krb = c2[:, KV_RANK + QK_NOPE:KV_RANK + 2 * QK_NOPE]
    gk = gk_ref[...]
    krot = kra * (gk[1:2] * ca) + krb * (gk[2:3] * sb)
    ssq_r = jnp.sum(kra * kra, axis=-1, keepdims=True)
    for h in range(MLA_HEADS):
        knope = kv[:, h * 2 * QK_NOPE:h * 2 * QK_NOPE + QK_NOPE]
        ssq = jnp.sum(knope * knope, axis=-1, keepdims=True) + ssq_r
        r = lax.rsqrt(ssq * (1.0 / QK_HEAD) + RMS_EPS)
        k_ref[0, h, :, 0:QK_NOPE] = (knope * gk[0:1] * r).astype(BF16)
        k_ref[0, h, :, QK_NOPE:HEAD_PAD] = (krot * r).astype(BF16)
        v_ref[0, h, :, 0:V_HEAD] = kv[:, h * 2 * QK_NOPE + QK_NOPE:(h + 1) * 2 * QK_NOPE].astype(BF16)
        v_ref[0, h, :, V_HEAD:2 * V_HEAD] = ones_col


def _kv_index(t):
    lat = t < BATCH * LAT_QTILES
    b = jnp.where(lat, t // LAT_QTILES, t - BATCH * LAT_QTILES)
    pos = jnp.where(lat, t % LAT_QTILES, LAT_QTILES)
    return b, pos


def _qkv(p, layer, wq_ext, w_ukv, qa, kva, gq, gk, rope_ca, rope_sb):
    def kv_map(t):
        b, pos = _kv_index(t)
        return (b, 0, pos, 0)

    def per_layer(shape):
        return pl.BlockSpec((None,) + shape, lambda t: (layer,) + (0,) * len(shape))

    return pl.pallas_call(
        _qkv_kernel,
        grid=(ROWS // TQ,),
        in_specs=[
            pl.BlockSpec((TQ, Q_RANK), lambda t: (t, PC_CQ // Q_RANK)),
            pl.BlockSpec((TQ, 512), lambda t: (t, PC_CKV // 512)),
            per_layer((Q_RANK, MLA_HEADS * Q_EXT)),
            per_layer((KV_RANK, MLA_HEADS * 2 * QK_NOPE)),
            per_layer((1, Q_RANK)),
            per_layer((1, KV_RANK)),
            per_layer((8, QK_NOPE)),
            per_layer((8, QK_NOPE)),
            pl.BlockSpec((TQ, QK_NOPE), lambda t: (t, 0)),
            pl.BlockSpec((TQ, QK_NOPE), lambda t: (t, 0)),
        ],
        out_specs=[
            pl.BlockSpec((MLA_HEADS, TQ, HEAD_PAD), lambda t: (0, t, 0)),
            pl.BlockSpec((1, MLA_HEADS, TQ, HEAD_PAD), kv_map),
            pl.BlockSpec((1, MLA_HEADS, TQ, 2 * V_HEAD), kv_map),
        ],
        out_shape=[
            jax.ShapeDtypeStruct((MLA_HEADS, ROWS, HEAD_PAD), BF16),
            jax.ShapeDtypeStruct((BATCH, MLA_HEADS, KEYS, HEAD_PAD), BF16),
            jax.ShapeDtypeStruct((BATCH, MLA_HEADS, KEYS, 2 * V_HEAD), BF16),
        ],
        name="qkv",
    )(p, p, wq_ext, w_ukv, qa, kva, gq, gk, rope_ca, rope_sb)


def _attn_kernel(q_ref, k_ref, v_ref, o_ref, *, n_sub):
    k = k_ref[...]
    v = v_ref[...]
    for t in range(n_sub):
        q = q_ref[t * TQ:(t + 1) * TQ, :]
        s = lax.dot_general(q, k, (((1,), (1,)), ((), ())), preferred_element_type=F32)
        e = jnp.exp2(s - jnp.max(s, axis=-1, keepdims=True)).astype(BF16)
        o = jnp.dot(e, v, preferred_element_type=F32)
        o_ref[t * TQ:(t + 1) * TQ, :] = (o[:, 0:V_HEAD] / o[:, V_HEAD:V_HEAD + 1]).astype(BF16)


ATT_SUB = 4


def _attention_lat(q, k, v):
    tq = ATT_SUB * TQ
    per_batch = SEQ // tq
    return pl.pallas_call(
        functools.partial(_attn_kernel, n_sub=ATT_SUB),
        grid=(BATCH, MLA_HEADS, per_batch),
        in_specs=[
            pl.BlockSpec((None, tq, HEAD_PAD), lambda b, h, r: (h, b * per_batch + r, 0)),
            pl.BlockSpec((None, None, KEYS, HEAD_PAD), lambda b, h, r: (b, h, 0, 0)),
            pl.BlockSpec((None, None, KEYS, 2 * V_HEAD), lambda b, h, r: (b, h, 0, 0)),
        ],
        out_specs=pl.BlockSpec((tq, V_HEAD), lambda b, h, r: (b * per_batch + r, h)),
        out_shape=jax.ShapeDtypeStruct((LAT_ROWS, MLA_HEADS * V_HEAD), BF16),
        compiler_params=pltpu.CompilerParams(dimension_semantics=("parallel", "parallel", "arbitrary")),
        name="attention_lat",
    )(q, k, v)


def _attention_ctx(q, k, v):
    return pl.pallas_call(
        functools.partial(_attn_kernel, n_sub=1),
        grid=(BATCH, MLA_HEADS),
        in_specs=[
            pl.BlockSpec((None, CTX_LEN, HEAD_PAD), lambda b, h: (h, LAT_ROWS // CTX_LEN + b, 0)),
            pl.BlockSpec((None, None, CTX_LEN, HEAD_PAD), lambda b, h: (b, h, SEQ // CTX_LEN, 0)),
            pl.BlockSpec((None, None, CTX_LEN, 2 * V_HEAD), lambda b, h: (b, h, SEQ // CTX_LEN, 0)),
        ],
        out_specs=pl.BlockSpec((CTX_LEN, V_HEAD), lambda b, h: (b, h)),
        out_shape=jax.ShapeDtypeStruct((CTX_ROWS, MLA_HEADS * V_HEAD), BF16),
        name="attention_ctx",
    )(q, k, v)


def _chan_dft_kernel(pf_ref, w_ref, o_ref):
    uv = jnp.dot(pf_ref[...], w_ref[...], preferred_element_type=F32)
    o_ref[0] = uv[:, 0:FOURIER_W].astype(BF16)
    o_ref[1] = uv[:, FOURIER_W:2 * FOURIER_W].astype(BF16)


def _chan_dft(p, w_chan, n_tiles):
    return pl.pallas_call(
        _chan_dft_kernel,
        grid=(n_tiles,),
        in_specs=[
            pl.BlockSpec((TM, FOURIER_W), lambda i: (i, PC_F // FOURIER_W)),
            pl.BlockSpec((FOURIER_W, 2 * FOURIER_W), lambda i: (0, 0)),
        ],
        out_specs=pl.BlockSpec((2, TM, FOURIER_W), lambda i: (0, i, 0)),
        out_shape=jax.ShapeDtypeStruct((2, n_tiles * TM, FOURIER_W), BF16),
        name="chan_dft",
    )(p, w_chan)


def _pos_dft_kernel(d_ref, uv0_ref, uv1_ref, o_ref, acc_ref):
    k = pl.program_id(1)

    @pl.when(k == 0)
    def _():
        acc_ref[...] = jnp.zeros_like(acc_ref)

    d = d_ref[...]
    acc_ref[0] += jnp.dot(d, uv0_ref[...], preferred_element_type=F32)
    acc_ref[1] += jnp.dot(d, uv1_ref[...], preferred_element_type=F32)

    @pl.when(k == pl.num_programs(1) - 1)
    def _():
        o_ref[...] = acc_ref[...].astype(BF16)


def _pos_dft(dft, uv, length, row0, tm, tk):
    nk_half = length // tk
    kb0 = row0 // tk

    def uv_spec(b):
        return pl.BlockSpec((None, tk, FOURIER_W),
                            lambda m, k: (k // nk_half, kb0 + b * nk_half + k % nk_half, 0))

    out = pl.pallas_call(
        _pos_dft_kernel,
        grid=(length // tm, 2 * nk_half),
        in_specs=[pl.BlockSpec((tm, tk), lambda m, k: (m, k)), uv_spec(0), uv_spec(1)],
        out_specs=pl.BlockSpec((BATCH, tm, FOURIER_W), lambda m, k: (0, m, 0)),
        out_shape=jax.ShapeDtypeStruct((BATCH, length, FOURIER_W), BF16),
        scratch_shapes=[pltpu.VMEM((BATCH, tm, FOURIER_W), F32)],
        compiler_params=pltpu.CompilerParams(dimension_semantics=("parallel", "arbitrary")),
        name="pos_dft_%d" % length,
    )(dft, uv, uv)
    return out.reshape(BATCH * length, FOURIER_W)


HALO = 16
STREAM_STARTS = (0, SEQ, LAT_ROWS, LAT_ROWS + CTX_LEN)


def _conv_kernel(cx_ref, cb_ref, cc_ref, pcx_ref, pcc_ref, ncx_ref, ncc_ref, w_ref, o_ref):
    i = pl.program_id(0)
    u = cc_ref[...].astype(F32) * cx_ref[...].astype(F32)
    row = lax.broadcasted_iota(jnp.int32, (TM, 1), 0)
    g = row + i * TM
    first = functools.reduce(jnp.logical_or, [g == s for s in STREAM_STARTS])
    last = functools.reduce(jnp.logical_or, [g == s - 1 for s in STREAM_STARTS[1:] + (ROWS,)])
    halo_prev = pcc_ref[HALO - 1:HALO, :].astype(F32) * pcx_ref[HALO - 1:HALO, :].astype(F32)
    halo_next = ncc_ref[0:1, :].astype(F32) * ncx_ref[0:1, :].astype(F32)
    up = jnp.where(row == 0, halo_prev, pltpu.roll(u, 1, 0))
    un = jnp.where(row == TM - 1, halo_next, pltpu.roll(u, TM - 1, 0))
    up = jnp.where(first, 0.0, up)
    un = jnp.where(last, 0.0, un)
    w = w_ref[...]
    y = up * w[0:1] + u * w[1:2] + un * w[2:3]
    o_ref[...] = (cb_ref[...].astype(F32) * y).astype(BF16)


def _short_conv(p, conv_w, layer, n_tiles):
    cb0 = PC_CX // CONV_W
    per = TM // HALO
    prev_map = lambda c: (lambda i: (jnp.maximum(i * per - 1, 0), c))
    next_map = lambda c: (lambda i: (jnp.minimum((i + 1) * per, ROWS // HALO - 1), c))
    return pl.pallas_call(
        _conv_kernel,
        grid=(n_tiles,),
        in_specs=[
            pl.BlockSpec((TM, CONV_W), lambda i: (i, cb0)),
            pl.BlockSpec((TM, CONV_W), lambda i: (i, cb0 + 1)),
            pl.BlockSpec((TM, CONV_W), lambda i: (i, cb0 + 2)),
            pl.BlockSpec((HALO, CONV_W), prev_map(cb0)),
            pl.BlockSpec((HALO, CONV_W), prev_map(cb0 + 2)),
            pl.BlockSpec((HALO, CONV_W), next_map(cb0)),
            pl.BlockSpec((HALO, CONV_W), next_map(cb0 + 2)),
            pl.BlockSpec((None, 8, CONV_W), lambda i: (layer, 0, 0)),
        ],
        out_specs=pl.BlockSpec((TM, CONV_W), lambda i: (i, 0)),
        out_shape=jax.ShapeDtypeStruct((n_tiles * TM, CONV_W), BF16),
        name="short_conv",
    )(p, p, p, p, p, p, p, conv_w)


MIX_CHUNK = 512


def _mixout_kernel(fl_ref, fc_ref, al_ref, ac_ref, c_ref, g0_ref, g1_ref, g2_ref, x_ref, gate_ref,
                   wf_ref, wm_ref, wc_ref, wo_ref, o_ref, m_ref):
    is_ctx = pl.program_id(0) == LAT_TILES
    f = jnp.where(is_ctx, fc_ref[...], fl_ref[...])
    a = jnp.where(is_ctx, ac_ref[...], al_ref[...])
    c = c_ref[...]
    for n in range(D_MODEL // MIX_CHUNK):
        cols = pl.ds(n * MIX_CHUNK, MIX_CHUNK)
        y = (g0_ref[:, cols].astype(F32) * jnp.dot(f, wf_ref[:, cols], preferred_element_type=F32)
             + g1_ref[:, cols].astype(F32) * jnp.dot(a, wm_ref[:, cols], preferred_element_type=F32)
             + g2_ref[:, cols].astype(F32) * jnp.dot(c, wc_ref[:, cols], preferred_element_type=F32))
        m_ref[:, cols] = y.astype(BF16)
    m = m_ref[...]
    for n in range(D_MODEL // MIX_CHUNK):
        cols = pl.ds(n * MIX_CHUNK, MIX_CHUNK)
        acc = jnp.dot(m, wo_ref[:, cols], preferred_element_type=F32)
        o_ref[:, cols] = x_ref[:, cols] + gate_ref[:, cols] * acc


def _mixout(f_lat, f_ctx, a_lat, a_ctx, cmix, p, x_all, mod, layer, w_f, w_m, w_c, w_out, n_tiles):
    gb = PC_G // D_MODEL
    lat_or_last = lambda i: (jnp.minimum(i, LAT_TILES - 1), 0)
    resident = lambda rows: pl.BlockSpec((None, rows, D_MODEL), lambda i: (layer, 0, 0),
                                         pipeline_mode=pl.Buffered(1))
    base = layer * MOD_ROWS * N_MOD + 2
    return pl.pallas_call(
        _mixout_kernel,
        grid=(n_tiles,),
        in_specs=[
            pl.BlockSpec((TM, FOURIER_W), lat_or_last),
            pl.BlockSpec((TM, FOURIER_W), lambda i: (0, 0)),
            pl.BlockSpec((TM, MLA_HEADS * V_HEAD), lat_or_last),
            pl.BlockSpec((TM, MLA_HEADS * V_HEAD), lambda i: (0, 0)),
            pl.BlockSpec((TM, CONV_W), lambda i: (i, 0)),
            pl.BlockSpec((TM, D_MODEL), lambda i: (i, gb)),
            pl.BlockSpec((TM, D_MODEL), lambda i: (i, gb + 1)),
            pl.BlockSpec((TM, D_MODEL), lambda i: (i, gb + 2)),
            pl.BlockSpec((TM, D_MODEL), lambda i: (i, 0)),
            pl.BlockSpec((None, 1, D_MODEL), lambda i: (base + (i // TILES_PER_BATCH) * N_MOD, 0, 0)),
            resident(FOURIER_W),
            resident(MLA_HEADS * V_HEAD),
            resident(CONV_W),
            resident(D_MODEL),
        ],
        out_specs=pl.BlockSpec((TM, D_MODEL), lambda i: (i, 0)),
        out_shape=jax.ShapeDtypeStruct((n_tiles * TM, D_MODEL), F32),
        scratch_shapes=[pltpu.VMEM((TM, D_MODEL), BF16)],
        name="mixout",
    )(f_lat, f_ctx, a_lat, a_ctx, cmix, p, p, p, x_all, mod, w_f, w_m, w_c, w_out)


FF_TN = 512


def _ffn_kernel(x_ref, gain_ref, shift_ref, scale_ref, gate_ref, wg_ref, wu_ref, wd_ref, o_ref,
                h_ref, acc_ref):
    f = pl.program_id(1)

    @pl.when(f == 0)
    def _():
        y = _rms(x_ref[...]) * gain_ref[...]
        h_ref[...] = (y * (1.0 + scale_ref[...]) + shift_ref[...]).astype(BF16)
        acc_ref[...] = jnp.zeros_like(acc_ref)

    h = h_ref[...]
    g = jnp.dot(h, wg_ref[...], preferred_element_type=F32)
    u = jnp.dot(h, wu_ref[...], preferred_element_type=F32)
    a = (g * _sigmoid(g) * u).astype(BF16)
    acc_ref[...] += jnp.dot(a, wd_ref[...], preferred_element_type=F32)

    @pl.when(f == pl.num_programs(1) - 1)
    def _():
        o_ref[...] = x_ref[...] + gate_ref[...] * acc_ref[...]


def _ffn(x_all, gain, mod, layer, w_gate, w_up, w_down, n_tiles):
    return pl.pallas_call(
        _ffn_kernel,
        grid=(n_tiles, D_FF // FF_TN),
        in_specs=[
            pl.BlockSpec((TM, D_MODEL), lambda i, f: (i, 0)),
            pl.BlockSpec((None, 1, D_MODEL), lambda i, f: (layer, 0, 0)),
            _mod_spec(layer, 3),
            _mod_spec(layer, 4),
            _mod_spec(layer, 5),
            pl.BlockSpec((None, D_MODEL, FF_TN), lambda i, f: (layer, 0, f)),
            pl.BlockSpec((None, D_MODEL, FF_TN), lambda i, f: (layer, 0, f)),
            pl.BlockSpec((None, FF_TN, D_MODEL), lambda i, f: (layer, f, 0)),
        ],
        out_specs=pl.BlockSpec((TM, D_MODEL), lambda i, f: (i, 0)),
        out_shape=jax.ShapeDtypeStruct((n_tiles * TM, D_MODEL), F32),
        scratch_shapes=[pltpu.VMEM((TM, D_MODEL), BF16), pltpu.VMEM((TM, D_MODEL), F32)],
        compiler_params=pltpu.CompilerParams(dimension_semantics=("parallel", "arbitrary")),
        name="ffn",
    )(x_all, gain, mod, mod, mod, w_gate, w_up, w_down)


def _dft_cos_sin(length):
    kn = np.outer(np.arange(length), np.arange(length)) % length
    ang = 2.0 * np.pi * kn / length
    return np.cos(ang), np.sin(ang)


def _chan_dft_matrix():
    c, s = _dft_cos_sin(FOURIER_GROUP_W)
    eye = np.eye(FOURIER_GROUPS)
    return jnp.asarray(np.concatenate([np.kron(eye, c), np.kron(eye, s)], axis=1), F32).astype(BF16)


def _ctx_dft_matrix():
    c, s = _dft_cos_sin(CTX_LEN)
    norm = (CTX_LEN * FOURIER_GROUP_W) ** -0.5
    return jnp.asarray(np.concatenate([c, -s], axis=1) * norm, F32).astype(BF16)


def _lat_dft_matrix():
    r = 64
    n = np.arange(SEQ)
    hi = 2.0 * np.pi * (np.outer(np.arange(r), n) % r) / r
    lo = 2.0 * np.pi * (np.outer(np.arange(r), n) % SEQ) / SEQ
    norm = (SEQ * FOURIER_GROUP_W) ** -0.5
    tab = lambda t: jnp.asarray(t, F32)
    hi_c = jnp.concatenate([tab(np.cos(hi) * norm)] * 2, axis=1)[:, None, :]
    hi_s = jnp.concatenate([tab(np.sin(hi) * norm)] * 2, axis=1)[:, None, :]
    lo_a = jnp.concatenate([tab(np.cos(lo)), tab(-np.sin(lo))], axis=1)[None, :, :]
    lo_b = jnp.concatenate([tab(-np.sin(lo)), tab(-np.cos(lo))], axis=1)[None, :, :]
    return (hi_c * lo_a + hi_s * lo_b).astype(BF16).reshape(SEQ, 2 * SEQ)


def _rope_tables():
    rows = SEQ // GRID_W
    row = jnp.repeat(jnp.arange(rows), GRID_W)
    col = jnp.tile(jnp.arange(GRID_W), rows)
    inv_freq = ROPE_THETA ** (-jnp.arange(AXIS_PAIRS, dtype=F32) / AXIS_PAIRS)
    ang = jnp.concatenate([row[:, None] * inv_freq, col[:, None] * inv_freq], axis=-1)
    cos, sin = jnp.cos(ang), jnp.sin(ang)
    zeros = jnp.zeros((SEQ, QK_NOPE - QK_ROPE), F32)
    ca = jnp.concatenate([cos, cos, zeros], axis=-1)
    sb = jnp.concatenate([-sin, sin, zeros], axis=-1)
    ca = jnp.concatenate([ca, ca, jnp.ones((CTX_ROWS, QK_NOPE), F32)], axis=0)
    sb = jnp.concatenate([sb, sb, jnp.zeros((CTX_ROWS, QK_NOPE), F32)], axis=0)
    return ca, sb


def _pack_w_uq(w_uq):
    w = w_uq.reshape(DEPTH, Q_RANK, MLA_HEADS, QK_HEAD)
    ra, rb = _rope_split(w[..., QK_NOPE:])
    return jnp.concatenate([w[..., :QK_NOPE], ra, rb], axis=-1).reshape(DEPTH, Q_RANK, MLA_HEADS * Q_EXT).astype(BF16)


def _pack_head_gain(g):
    ga, gb = _rope_split(g[:, QK_NOPE:])
    rows = jnp.stack([g[:, :QK_NOPE], ga, gb], axis=1)
    return jnp.concatenate([rows, jnp.zeros((DEPTH, 5, QK_NOPE), F32)], axis=1)


def kernel(x, c, ctx, c_ctx, w_ada, b_ada, norm_mix, norm_ffn, w_in, b_gate, q_a_norm, kv_a_norm, w_uq, w_ukv,
           q_norm, k_norm, w_f_out, w_mla_out, conv_w, w_conv_out, w_out, w_ffn_gate, w_ffn_up, w_ffn_down):
    x_all = jnp.concatenate([x.reshape(LAT_ROWS, D_MODEL), ctx.reshape(CTX_ROWS, D_MODEL)], axis=0)
    cond = jnp.concatenate([c, c_ctx[None, :], jnp.zeros((MOD_ROWS - BATCH - 1, D_MODEL), F32)], axis=0)
    bias_pack = jnp.concatenate([b_gate[:, None, :], jnp.zeros((DEPTH, 1, P_COLS - PC_F), F32)], axis=-1)
    wq_ext = _pack_w_uq(w_uq)
    gq = _pack_head_gain(q_norm)
    gk = _pack_head_gain(k_norm)
    conv_w8 = jnp.concatenate([conv_w, jnp.zeros((DEPTH, 5, CONV_W), F32)], axis=1)
    rope_ca, rope_sb = _rope_tables()
    w_chan = _chan_dft_matrix()
    dft_lat = _lat_dft_matrix()
    dft_ctx = _ctx_dft_matrix()
    bf = lambda w: w.astype(BF16)
    w_ukv_b, w_f_b, w_m_b, w_c_b, w_out_b = bf(w_ukv), bf(w_f_out), bf(w_mla_out), bf(w_conv_out), bf(w_out)
    w_g_b, w_u_b, w_d_b = bf(w_ffn_gate), bf(w_ffn_up), bf(w_ffn_down)
    gain_mix, gain_ffn = norm_mix[:, None, :], norm_ffn[:, None, :]
    qa, kva = q_a_norm[:, None, :], kv_a_norm[:, None, :]

    w_pack = _pack_w_in(jnp.swapaxes(w_in, 1, 2))
    mod = _ada(cond, w_ada, b_ada).reshape(DEPTH * MOD_ROWS * N_MOD, 1, D_MODEL)

    for l in range(DEPTH):
        last = l == DEPTH - 1
        n_tiles = LAT_TILES if last else N_TILES
        p = _inproj(x_all, gain_mix, mod, l, w_pack, bias_pack)
        q, k, v = _qkv(p, l, wq_ext, w_ukv_b, qa, kva, gq, gk, rope_ca, rope_sb)
        a_lat = _attention_lat(q, k, v)
        a_ctx = a_lat if last else _attention_ctx(q, k, v)
        uv = _chan_dft(p, w_chan, n_tiles)
        f_lat = _pos_dft(dft_lat, uv, SEQ, 0, 1024, 2048)
        f_ctx = f_lat if last else _pos_dft(dft_ctx, uv, CTX_LEN, LAT_ROWS, CTX_LEN, CTX_LEN)
        cmix = _short_conv(p, conv_w8, l, n_tiles)
        x_all = _mixout(f_lat, f_ctx, a_lat, a_ctx, cmix, p, x_all, mod, l, w_f_b, w_m_b, w_c_b, w_out_b, n_tiles)
        x_all = _ffn(x_all, gain_ffn, mod, l, w_g_b, w_u_b, w_d_b, n_tiles)
    return x_all.reshape(BATCH, SEQ, D_MODEL)
```

```python
import functools

import numpy as np
import jax
import jax.numpy as jnp
from jax import lax
from jax.experimental import pallas as pl
from jax.experimental.pallas import tpu as pltpu

F32 = jnp.float32
BF16 = jnp.bfloat16

D_MODEL = 2048
BATCH = 2
SEQ = 4096
DEPTH = 2
GRID_W = 64
CTX_LEN = 256
FOURIER_GROUPS = 4
FOURIER_GROUP_W = 128
FOURIER_W = FOURIER_GROUPS * FOURIER_GROUP_W
MLA_HEADS = 8
Q_RANK = 512
KV_RANK = 256
QK_NOPE = 128
QK_ROPE = 64
QK_HEAD = QK_NOPE + QK_ROPE
V_HEAD = 128
ROPE_THETA = 10000.0
AXIS_PAIRS = QK_ROPE // 4
CONV_W = 512
N_BRANCH = 3
D_FF = ((8 * D_MODEL // 3 + 255) // 256) * 256
N_MOD = 6
RMS_EPS = 1e-6

OFF_F = 0
OFF_CQ = OFF_F + FOURIER_W
OFF_CKV = OFF_CQ + Q_RANK
OFF_KR = OFF_CKV + KV_RANK
OFF_CX = OFF_KR + QK_ROPE
OFF_G = OFF_CX + 3 * CONV_W
N_IN = OFF_G + N_BRANCH * D_MODEL

LAT_ROWS = BATCH * SEQ
CTX_ROWS = BATCH * CTX_LEN
ROWS = LAT_ROWS + CTX_ROWS
KEYS = SEQ + CTX_LEN
TM = 512
N_TILES = ROWS // TM
LAT_TILES = LAT_ROWS // TM
TILES_PER_BATCH = SEQ // TM
TQ = 256
LAT_QTILES = SEQ // TQ
MOD_ROWS = 8

P_TN = 1536
P_CHUNK = 512
P_COLS = 6 * P_TN
PC_G = 0
N_GATE_TILES = N_BRANCH * D_MODEL // P_TN
PC_F = PC_G + N_BRANCH * D_MODEL
PC_CQ = PC_F + FOURIER_W
PC_CKV = PC_CQ + Q_RANK
PC_CX = PC_F + P_TN
HEAD_PAD = 256
Q_EXT = 384


def _rms(x, eps=RMS_EPS):
    return x * lax.rsqrt(jnp.mean(x * x, axis=-1, keepdims=True) + eps)


def _sigmoid(z):
    return 1.0 / (1.0 + jnp.exp(-z))


def _rope_split(w):
    half = QK_ROPE // 2
    pad = jnp.zeros(w.shape[:-1] + (QK_NOPE - QK_ROPE,), w.dtype)
    a = jnp.concatenate([w, pad], axis=-1)
    b = jnp.concatenate([w[..., half:], w[..., :half], pad], axis=-1)
    return a, b


ADA_TN = 1024


def _ada_kernel(a_ref, w_ref, b_ref, o_ref):
    a = a_ref[...]
    a = (a * _sigmoid(a)).astype(BF16)
    o_ref[...] = jnp.dot(a, w_ref[...].astype(BF16), preferred_element_type=F32) + b_ref[...]


def _ada(cond, w_ada, b_ada):
    n = N_MOD * D_MODEL
    return pl.pallas_call(
        _ada_kernel,
        grid=(DEPTH, n // ADA_TN),
        in_specs=[
            pl.BlockSpec((MOD_ROWS, D_MODEL), lambda l, j: (0, 0)),
            pl.BlockSpec((None, D_MODEL, ADA_TN), lambda l, j: (l, 0, j)),
            pl.BlockSpec((None, 1, ADA_TN), lambda l, j: (l, 0, j)),
        ],
        out_specs=pl.BlockSpec((None, MOD_ROWS, ADA_TN), lambda l, j: (l, 0, j)),
        out_shape=jax.ShapeDtypeStruct((DEPTH, MOD_ROWS, n), F32),
        name="ada",
    )(cond, w_ada, b_ada.reshape(DEPTH, 1, n))


def _mod_spec(layer, which):
    base = layer * MOD_ROWS * N_MOD + which
    return pl.BlockSpec((None, 1, D_MODEL), lambda i, *_: (base + (i // TILES_PER_BATCH) * N_MOD, 0, 0))


PACK_STEP = P_TN - OFF_CX


def _pack_src_row(j):
    step = jnp.where(j < N_GATE_TILES, OFF_G // PACK_STEP + j * (P_TN // PACK_STEP),
                     jnp.where(j == N_GATE_TILES, 0, OFF_CX // PACK_STEP))
    return step * PACK_STEP


def _pack_kernel(w_ref, o_ref):
    j = pl.program_id(1)
    half = QK_ROPE // 2

    @pl.when(j == N_GATE_TILES)
    def _():
        o_ref[0:OFF_CX, :] = w_ref[0, 0:OFF_CX, :].astype(BF16)
        o_ref[OFF_CX:OFF_KR + QK_NOPE, :] = jnp.zeros((QK_NOPE - QK_ROPE, D_MODEL), BF16)
        o_ref[OFF_KR + QK_NOPE:OFF_KR + QK_NOPE + half, :] = w_ref[0, OFF_KR + half:OFF_CX, :].astype(BF16)
        o_ref[OFF_KR + QK_NOPE + half:OFF_KR + QK_NOPE + QK_ROPE, :] = w_ref[0, OFF_KR:OFF_KR + half, :].astype(BF16)
        o_ref[OFF_KR + QK_NOPE + QK_ROPE:P_TN, :] = jnp.zeros((QK_NOPE - QK_ROPE, D_MODEL), BF16)

    @pl.when(j != N_GATE_TILES)
    def _():
        o_ref[...] = w_ref[0].astype(BF16)


def _pack_w_in(w_in_t):
    return pl.pallas_call(
        _pack_kernel,
        grid=(DEPTH, P_COLS // P_TN),
        in_specs=[pl.BlockSpec((pl.Element(1), pl.Element(P_TN), pl.Element(D_MODEL)),
                               lambda l, j: (l, _pack_src_row(j), 0))],
        out_specs=pl.BlockSpec((None, P_TN, D_MODEL), lambda l, j: (l, j, 0)),
        out_shape=jax.ShapeDtypeStruct((DEPTH, P_COLS, D_MODEL), BF16),
        name="pack_w_in",
    )(w_in_t)


def _rows_specs(ctx_block):
    return [pl.BlockSpec((TM, D_MODEL), lambda i: (jnp.minimum(i, LAT_TILES - 1), 0)),
            pl.BlockSpec((TM, D_MODEL), lambda i: (ctx_block, 0))]


def _prenorm_kernel(xl_ref, xc_ref, gain_ref, shift_ref, scale_ref, h_ref, *rows_ref):
    x = jnp.where(pl.program_id(0) == LAT_TILES, xc_ref[...], xl_ref[...])
    y = _rms(x) * gain_ref[...]
    h_ref[...] = (y * (1.0 + scale_ref[...]) + shift_ref[...]).astype(BF16)
    for r in rows_ref:
        r[...] = x


def _prenorm(x_lat, x_ctx, ctx_block, gain, mod, layer, emit_rows):
    tile = pl.BlockSpec((TM, D_MODEL), lambda i: (i, 0))
    return pl.pallas_call(
        _prenorm_kernel,
        grid=(N_TILES,),
        in_specs=_rows_specs(ctx_block) + [
            pl.BlockSpec((None, 1, D_MODEL), lambda i: (layer, 0, 0)), _mod_spec(layer, 0), _mod_spec(layer, 1)],
        out_specs=[tile] + [tile] * emit_rows,
        out_shape=[jax.ShapeDtypeStruct((ROWS, D_MODEL), BF16)]
        + [jax.ShapeDtypeStruct((ROWS, D_MODEL), F32)] * emit_rows,
        name="prenorm",
    )(x_lat, x_ctx, gain, mod, mod)


def _inproj_kernel(h_ref, w_ref, bias_ref, o_ref):
    is_gate = pl.program_id(0) < N_GATE_TILES
    h = h_ref[...]
    for c in range(P_TN // P_CHUNK):
        cols = pl.ds(c * P_CHUNK, P_CHUNK)
        z = lax.dot_general(h, w_ref[cols, :], (((1,), (1,)), ((), ())), preferred_element_type=F32)
        z = z + bias_ref[:, cols]
        o_ref[:, cols] = jnp.where(is_gate, _sigmoid(z), z).astype(BF16)


def _inproj(h, layer, w_pack, bias_pack):
    return pl.pallas_call(
        _inproj_kernel,
        grid=(P_COLS // P_TN, N_TILES),
        in_specs=[
            pl.BlockSpec((TM, D_MODEL), lambda j, i: (i, 0)),
            pl.BlockSpec((None, P_TN, D_MODEL), lambda j, i: (layer, j, 0)),
            pl.BlockSpec((None, 1, P_TN), lambda j, i: (layer, 0, j)),
        ],
        out_specs=pl.BlockSpec((TM, P_TN), lambda j, i: (i, j)),
        out_shape=jax.ShapeDtypeStruct((ROWS, P_COLS), BF16),
        name="inproj",
    )(h, w_pack, bias_pack)


def _qkv_kernel(cq_ref, ckv_ref, wq_ref, wkv_ref, qa_ref, kva_ref, gq_ref, gk_ref, ca_ref, sb_ref,
                q_ref, k_ref, v_ref):
    ca = ca_ref[...]
    sb = sb_ref[...]
    scale = QK_HEAD ** -0.5 * np.log2(np.e)
    ones_col = jnp.where(lax.broadcasted_iota(jnp.int32, (TQ, V_HEAD), 1) == 0, 1.0, 0.0).astype(BF16)

    cqn = (_rms(cq_ref[...].astype(F32)) * qa_ref[...]).astype(BF16)
    qe = jnp.dot(cqn, wq_ref[...], preferred_element_type=F32)
    gq = gq_ref[...]
    rot_a = gq[1:2] * ca
    rot_b = gq[2:3] * sb
    for h in range(MLA_HEADS):
        nope = qe[:, h * Q_EXT:h * Q_EXT + QK_NOPE]
        ra = qe[:, h * Q_EXT + QK_NOPE:h * Q_EXT + 2 * QK_NOPE]
        rb = qe[:, h * Q_EXT + 2 * QK_NOPE:(h + 1) * Q_EXT]
        ssq = jnp.sum(nope * nope, axis=-1, keepdims=True) + jnp.sum(ra * ra, axis=-1, keepdims=True)
        r = lax.rsqrt(ssq * (1.0 / QK_HEAD) + RMS_EPS) * scale
        q_ref[h, :, 0:QK_NOPE] = (nope * gq[0:1] * r).astype(BF16)
        q_ref[h, :, QK_NOPE:HEAD_PAD] = ((ra * rot_a + rb * rot_b) * r).astype(BF16)

    c2 = ckv_ref[...].astype(F32)
    ckvn = (_rms(c2[:, 0:KV_RANK]) * kva_ref[...]).astype(BF16)
    kv = jnp.dot(ckvn, wkv_ref[...], preferred_element_type=F32)
    kra = c2[:, KV_RANK:KV_RANK + QK_NOPE]
    krb = c2[:, KV_RANK + QK_NOPE:KV_RANK + 2 * QK_NOPE]
    gk = gk_ref[...]
    krot = kra * (gk[1:2] * ca) + krb * (gk[2:3] * sb)
    ssq_r = jnp.sum(kra * kra, axis=-1, keepdims=True)
    for h in range(MLA_HEADS):
        knope = kv[:, h * 2 * QK_NOPE:h * 2 * QK_NOPE + QK_NOPE]
        ssq = jnp.sum(knope * knope, axis=-1, keepdims=True) + ssq_r
        r = lax.rsqrt(ssq * (1.0 / QK_HEAD) + RMS_EPS)
        k_ref[0, h, :, 0:QK_NOPE] = (knope * gk[0:1] * r).astype(BF16)
        k_ref[0, h, :, QK_NOPE:HEAD_PAD] = (krot * r).astype(BF16)
        v_ref[0, h, :, 0:V_HEAD] = kv[:, h * 2 * QK_NOPE + QK_NOPE:(h + 1) * 2 * QK_NOPE].astype(BF16)
        v_ref[0, h, :, V_HEAD:2 * V_HEAD] = ones_col


def _kv_index(t):
    lat = t < BATCH * LAT_QTILES
    b = jnp.where(lat, t // LAT_QTILES, t - BATCH * LAT_QTILES)
    pos = jnp.where(lat, t % LAT_QTILES, LAT_QTILES)
    return b, pos


def _qkv(p, layer, wq_ext, w_ukv, qa, kva, gq, gk, rope_ca, rope_sb):
    def kv_map(t):
        b, pos = _kv_index(t)
        return (b, 0, pos, 0)

    def per_layer(shape):
        return pl.BlockSpec((None,) + shape, lambda t: (layer,) + (0,) * len(shape))

    return pl.pallas_call(
        _qkv_kernel,
        grid=(ROWS // TQ,),
        in_specs=[
            pl.BlockSpec((TQ, Q_RANK), lambda t: (t, PC_CQ // Q_RANK)),
            pl.BlockSpec((TQ, 512), lambda t: (t, PC_CKV // 512)),
            per_layer((Q_RANK, MLA_HEADS * Q_EXT)),
            per_layer((KV_RANK, MLA_HEADS * 2 * QK_NOPE)),
            per_layer((1, Q_RANK)),
            per_layer((1, KV_RANK)),
            per_layer((8, QK_NOPE)),
            per_layer((8, QK_NOPE)),
            pl.BlockSpec((TQ, QK_NOPE), lambda t: (t, 0)),
            pl.BlockSpec((TQ, QK_NOPE), lambda t: (t, 0)),
        ],
        out_specs=[
            pl.BlockSpec((MLA_HEADS, TQ, HEAD_PAD), lambda t: (0, t, 0)),
            pl.BlockSpec((1, MLA_HEADS, TQ, HEAD_PAD), kv_map),
            pl.BlockSpec((1, MLA_HEADS, TQ, 2 * V_HEAD), kv_map),
        ],
        out_shape=[
            jax.ShapeDtypeStruct((MLA_HEADS, ROWS, HEAD_PAD), BF16),
            jax.ShapeDtypeStruct((BATCH, MLA_HEADS, KEYS, HEAD_PAD), BF16),
            jax.ShapeDtypeStruct((BATCH, MLA_HEADS, KEYS, 2 * V_HEAD), BF16),
        ],
        name="qkv",
    )(p, p, wq_ext, w_ukv, qa, kva, gq, gk, rope_ca, rope_sb)


def _attn_kernel(q_ref, k_ref, v_ref, o_ref, *, n_sub):
    k = k_ref[...]
    v = v_ref[...]
    for t in range(n_sub):
        q = q_ref[t * TQ:(t + 1) * TQ, :]
        s = lax.dot_general(q, k, (((1,), (1,)), ((), ())), preferred_element_type=F32)
        e = jnp.exp2(s - jnp.max(s, axis=-1, keepdims=True)).astype(BF16)
        o = jnp.dot(e, v, preferred_element_type=F32)
        o_ref[t * TQ:(t + 1) * TQ, :] = (o[:, 0:V_HEAD] / o[:, V_HEAD:V_HEAD + 1]).astype(BF16)


ATT_SUB = 8


def _attention_lat(q, k, v):
    tq = ATT_SUB * TQ
    per_batch = SEQ // tq
    return pl.pallas_call(
        functools.partial(_attn_kernel, n_sub=ATT_SUB),
        grid=(BATCH, MLA_HEADS, per_batch),
        in_specs=[
            pl.BlockSpec((None, tq, HEAD_PAD), lambda b, h, r: (h, b * per_batch + r, 0)),
            pl.BlockSpec((None, None, KEYS, HEAD_PAD), lambda b, h, r: (b, h, 0, 0)),
            pl.BlockSpec((None, None, KEYS, 2 * V_HEAD), lambda b, h, r: (b, h, 0, 0)),
        ],
        out_specs=pl.BlockSpec((tq, V_HEAD), lambda b, h, r: (b * per_batch + r, h)),
        out_shape=jax.ShapeDtypeStruct((LAT_ROWS, MLA_HEADS * V_HEAD), BF16),
        compiler_params=pltpu.CompilerParams(dimension_semantics=("parallel", "parallel", "arbitrary")),
        name="attention_lat",
    )(q, k, v)


def _attention_ctx(q, k, v):
    return pl.pallas_call(
        functools.partial(_attn_kernel, n_sub=1),
        grid=(BATCH, MLA_HEADS),
        in_specs=[
            pl.BlockSpec((None, CTX_LEN, HEAD_PAD), lambda b, h: (h, LAT_ROWS // CTX_LEN + b, 0)),
            pl.BlockSpec((None, None, CTX_LEN, HEAD_PAD), lambda b, h: (b, h, SEQ // CTX_LEN, 0)),
            pl.BlockSpec((None, None, CTX_LEN, 2 * V_HEAD), lambda b, h: (b, h, SEQ // CTX_LEN, 0)),
        ],
        out_specs=pl.BlockSpec((CTX_LEN, V_HEAD), lambda b, h: (b, h)),
        out_shape=jax.ShapeDtypeStruct((CTX_ROWS, MLA_HEADS * V_HEAD), BF16),
        name="attention_ctx",
    )(q, k, v)


def _chan_dft_kernel(pf_ref, w_ref, o_ref):
    uv = jnp.dot(pf_ref[...], w_ref[...], preferred_element_type=F32)
    o_ref[0] = uv[:, 0:FOURIER_W].astype(BF16)
    o_ref[1] = uv[:, FOURIER_W:2 * FOURIER_W].astype(BF16)


def _chan_dft(p, w_chan, n_tiles):
    return pl.pallas_call(
        _chan_dft_kernel,
        grid=(n_tiles,),
        in_specs=[
            pl.BlockSpec((TM, FOURIER_W), lambda i: (i, PC_F // FOURIER_W)),
            pl.BlockSpec((FOURIER_W, 2 * FOURIER_W), lambda i: (0, 0)),
        ],
        out_specs=pl.BlockSpec((2, TM, FOURIER_W), lambda i: (0, i, 0)),
        out_shape=jax.ShapeDtypeStruct((2, n_tiles * TM, FOURIER_W), BF16),
        name="chan_dft",
    )(p, w_chan)


def _pos_dft_kernel(d_ref, uv0_ref, uv1_ref, o_ref, acc_ref):
    k = pl.program_id(1)

    @pl.when(k == 0)
    def _():
        acc_ref[...] = jnp.zeros_like(acc_ref)

    d = d_ref[...]
    acc_ref[0] += jnp.dot(d, uv0_ref[...], preferred_element_type=F32)
    acc_ref[1] += jnp.dot(d, uv1_ref[...], preferred_element_type=F32)

    @pl.when(k == pl.num_programs(1) - 1)
    def _():
        o_ref[...] = acc_ref[...].astype(BF16)


def _pos_dft(dft, uv, length, row0, tm, tk):
    nk_half = length // tk
    kb0 = row0 // tk

    def uv_spec(b):
        return pl.BlockSpec((None, tk, FOURIER_W),
                            lambda m, k: (k // nk_half, kb0 + b * nk_half + k % nk_half, 0))

    out = pl.pallas_call(
        _pos_dft_kernel,
        grid=(length // tm, 2 * nk_half),
        in_specs=[pl.BlockSpec((tm, tk), lambda m, k: (m, k)), uv_spec(0), uv_spec(1)],
        out_specs=pl.BlockSpec((BATCH, tm, FOURIER_W), lambda m, k: (0, m, 0)),
        out_shape=jax.ShapeDtypeStruct((BATCH, length, FOURIER_W), BF16),
        scratch_shapes=[pltpu.VMEM((BATCH, tm, FOURIER_W), F32)],
        compiler_params=pltpu.CompilerParams(dimension_semantics=("parallel", "arbitrary")),
        name="pos_dft_%d" % length,
    )(dft, uv, uv)
    return out.reshape(BATCH * length, FOURIER_W)


HALO = 16
STREAM_STARTS = (0, SEQ, LAT_ROWS, LAT_ROWS + CTX_LEN)


def _conv_kernel(cx_ref, cb_ref, cc_ref, pcx_ref, pcc_ref, ncx_ref, ncc_ref, w_ref, o_ref):
    i = pl.program_id(0)
    u = cc_ref[...].astype(F32) * cx_ref[...].astype(F32)
    row = lax.broadcasted_iota(jnp.int32, (TM, 1), 0)
    g = row + i * TM
    first = functools.reduce(jnp.logical_or, [g == s for s in STREAM_STARTS])
    last = functools.reduce(jnp.logical_or, [g == s - 1 for s in STREAM_STARTS[1:] + (ROWS,)])
    halo_prev = pcc_ref[HALO - 1:HALO, :].astype(F32) * pcx_ref[HALO - 1:HALO, :].astype(F32)
    halo_next = ncc_ref[0:1, :].astype(F32) * ncx_ref[0:1, :].astype(F32)
    up = jnp.where(row == 0, halo_prev, pltpu.roll(u, 1, 0))
    un = jnp.where(row == TM - 1, halo_next, pltpu.roll(u, TM - 1, 0))
    up = jnp.where(first, 0.0, up)
    un = jnp.where(last, 0.0, un)
    w = w_ref[...]
    y = up * w[0:1] + u * w[1:2] + un * w[2:3]
    o_ref[...] = (cb_ref[...].astype(F32) * y).astype(BF16)


def _short_conv(p, conv_w, layer, n_tiles):
    cb0 = PC_CX // CONV_W
    per = TM // HALO
    prev_map = lambda c: (lambda i: (jnp.maximum(i * per - 1, 0), c))
    next_map = lambda c: (lambda i: (jnp.minimum((i + 1) * per, ROWS // HALO - 1), c))
    return pl.pallas_call(
        _conv_kernel,
        grid=(n_tiles,),
        in_specs=[
            pl.BlockSpec((TM, CONV_W), lambda i: (i, cb0)),
            pl.BlockSpec((TM, CONV_W), lambda i: (i, cb0 + 1)),
            pl.BlockSpec((TM, CONV_W), lambda i: (i, cb0 + 2)),
            pl.BlockSpec((HALO, CONV_W), prev_map(cb0)),
            pl.BlockSpec((HALO, CONV_W), prev_map(cb0 + 2)),
            pl.BlockSpec((HALO, CONV_W), next_map(cb0)),
            pl.BlockSpec((HALO, CONV_W), next_map(cb0 + 2)),
            pl.BlockSpec((None, 8, CONV_W), lambda i: (layer, 0, 0)),
        ],
        out_specs=pl.BlockSpec((TM, CONV_W), lambda i: (i, 0)),
        out_shape=jax.ShapeDtypeStruct((n_tiles * TM, CONV_W), BF16),
        name="short_conv",
    )(p, p, p, p, p, p, p, conv_w)


MIX_CHUNK = 512


def _mixout_kernel(fl_ref, fc_ref, al_ref, ac_ref, c_ref, g0_ref, g1_ref, g2_ref, x_ref, gate_ref,
                   wf_ref, wm_ref, wc_ref, wo_ref, o_ref, m_ref):
    is_ctx = pl.program_id(0) == LAT_TILES
    f = jnp.where(is_ctx, fc_ref[...], fl_ref[...])
    a = jnp.where(is_ctx, ac_ref[...], al_ref[...])
    c = c_ref[...]
    for n in range(D_MODEL // MIX_CHUNK):
        cols = pl.ds(n * MIX_CHUNK, MIX_CHUNK)
        y = (g0_ref[:, cols].astype(F32) * jnp.dot(f, wf_ref[:, cols], preferred_element_type=F32)
             + g1_ref[:, cols].astype(F32) * jnp.dot(a, wm_ref[:, cols], preferred_element_type=F32)
             + g2_ref[:, cols].astype(F32) * jnp.dot(c, wc_ref[:, cols], preferred_element_type=F32))
        m_ref[:, cols] = y.astype(BF16)
    m = m_ref[...]
    for n in range(D_MODEL // MIX_CHUNK):
        cols = pl.ds(n * MIX_CHUNK, MIX_CHUNK)
        acc = jnp.dot(m, wo_ref[:, cols], preferred_element_type=F32)
        o_ref[:, cols] = x_ref[:, cols] + gate_ref[:, cols] * acc


def _mixout(f_lat, f_ctx, a_lat, a_ctx, cmix, p, x_all, mod, layer, w_f, w_m, w_c, w_out, n_tiles):
    gb = PC_G // D_MODEL
    lat_or_last = lambda i: (jnp.minimum(i, LAT_TILES - 1), 0)
    resident = lambda rows: pl.BlockSpec((None, rows, D_MODEL), lambda i: (layer, 0, 0),
                                         pipeline_mode=pl.Buffered(1))
    return pl.pallas_call(
        _mixout_kernel,
        grid=(n_tiles,),
        in_specs=[
            pl.BlockSpec((TM, FOURIER_W), lat_or_last),
            pl.BlockSpec((TM, FOURIER_W), lambda i: (0, 0)),
            pl.BlockSpec((TM, MLA_HEADS * V_HEAD), lat_or_last),
            pl.BlockSpec((TM, MLA_HEADS * V_HEAD), lambda i: (0, 0)),
            pl.BlockSpec((TM, CONV_W), lambda i: (i, 0)),
            pl.BlockSpec((TM, D_MODEL), lambda i: (i, gb)),
            pl.BlockSpec((TM, D_MODEL), lambda i: (i, gb + 1)),
            pl.BlockSpec((TM, D_MODEL), lambda i: (i, gb + 2)),
            pl.BlockSpec((TM, D_MODEL), lambda i: (i, 0)),
            _mod_spec(layer, 2),
            resident(FOURIER_W),
            resident(MLA_HEADS * V_HEAD),
            resident(CONV_W),
            resident(D_MODEL),
        ],
        out_specs=pl.BlockSpec((TM, D_MODEL), lambda i: (i, 0)),
        out_shape=jax.ShapeDtypeStruct((n_tiles * TM, D_MODEL), F32),
        scratch_shapes=[pltpu.VMEM((TM, D_MODEL), BF16)],
        name="mixout",
    )(f_lat, f_ctx, a_lat, a_ctx, cmix, p, p, p, x_all, mod, w_f, w_m, w_c, w_out)


FF_TN = 512


def _ffn_kernel(x_ref, gain_ref, shift_ref, scale_ref, gate_ref, wg_ref, wu_ref, wd_ref, o_ref,
                h_ref, acc_ref):
    f = pl.program_id(1)

    @pl.when(f == 0)
    def _():
        y = _rms(x_ref[...]) * gain_ref[...]
        h_ref[...] = (y * (1.0 + scale_ref[...]) + shift_ref[...]).astype(BF16)
        acc_ref[...] = jnp.zeros_like(acc_ref)

    h = h_ref[...]
    g = jnp.dot(h, wg_ref[...], preferred_element_type=F32)
    u = jnp.dot(h, wu_ref[...], preferred_element_type=F32)
    a = (g * _sigmoid(g) * u).astype(BF16)
    acc_ref[...] += jnp.dot(a, wd_ref[...], preferred_element_type=F32)

    @pl.when(f == pl.num_programs(1) - 1)
    def _():
        o_ref[...] = x_ref[...] + gate_ref[...] * acc_ref[...]


def _ffn(x_all, gain, mod, layer, w_gate, w_up, w_down, n_tiles):
    return pl.pallas_call(
        _ffn_kernel,
        grid=(n_tiles, D_FF // FF_TN),
        in_specs=[
            pl.BlockSpec((TM, D_MODEL), lambda i, f: (i, 0)),
            pl.BlockSpec((None, 1, D_MODEL), lambda i, f: (layer, 0, 0)),
            _mod_spec(layer, 3),
            _mod_spec(layer, 4),
            _mod_spec(layer, 5),
            pl.BlockSpec((None, D_MODEL, FF_TN), lambda i, f: (layer, 0, f)),
            pl.BlockSpec((None, D_MODEL, FF_TN), lambda i, f: (layer, 0, f)),
            pl.BlockSpec((None, FF_TN, D_MODEL), lambda i, f: (layer, f, 0)),
        ],
        out_specs=pl.BlockSpec((TM, D_MODEL), lambda i, f: (i, 0)),
        out_shape=jax.ShapeDtypeStruct((n_tiles * TM, D_MODEL), F32),
        scratch_shapes=[pltpu.VMEM((TM, D_MODEL), BF16), pltpu.VMEM((TM, D_MODEL), F32)],
        compiler_params=pltpu.CompilerParams(dimension_semantics=("parallel", "arbitrary")),
        name="ffn",
    )(x_all, gain, mod, mod, mod, w_gate, w_up, w_down)


def _dft_cos_sin(length):
    kn = np.outer(np.arange(length), np.arange(length)) % length
    ang = 2.0 * np.pi * kn / length
    return np.cos(ang), np.sin(ang)


def _chan_dft_matrix():
    c, s = _dft_cos_sin(FOURIER_GROUP_W)
    eye = np.eye(FOURIER_GROUPS)
    return jnp.asarray(np.concatenate([np.kron(eye, c), np.kron(eye, s)], axis=1), F32).astype(BF16)


def _ctx_dft_matrix():
    c, s = _dft_cos_sin(CTX_LEN)
    norm = (CTX_LEN * FOURIER_GROUP_W) ** -0.5
    return jnp.asarray(np.concatenate([c, -s], axis=1) * norm, F32).astype(BF16)


def _lat_dft_matrix():
    r = 64
    n = np.arange(SEQ)
    hi = 2.0 * np.pi * (np.outer(np.arange(r), n) % r) / r
    lo = 2.0 * np.pi * (np.outer(np.arange(r), n) % SEQ) / SEQ
    norm = (SEQ * FOURIER_GROUP_W) ** -0.5
    tab = lambda t: jnp.asarray(t, F32)
    hi_c = jnp.concatenate([tab(np.cos(hi) * norm)] * 2, axis=1)[:, None, :]
    hi_s = jnp.concatenate([tab(np.sin(hi) * norm)] * 2, axis=1)[:, None, :]
    lo_a = jnp.concatenate([tab(np.cos(lo)), tab(-np.sin(lo))], axis=1)[None, :, :]
    lo_b = jnp.concatenate([tab(-np.sin(lo)), tab(-np.cos(lo))], axis=1)[None, :, :]
    return (hi_c * lo_a + hi_s * lo_b).astype(BF16).reshape(SEQ, 2 * SEQ)


def _rope_tables():
    rows = SEQ // GRID_W
    row = jnp.repeat(jnp.arange(rows), GRID_W)
    col = jnp.tile(jnp.arange(GRID_W), rows)
    inv_freq = ROPE_THETA ** (-jnp.arange(AXIS_PAIRS, dtype=F32) / AXIS_PAIRS)
    ang = jnp.concatenate([row[:, None] * inv_freq, col[:, None] * inv_freq], axis=-1)
    cos, sin = jnp.cos(ang), jnp.sin(ang)
    zeros = jnp.zeros((SEQ, QK_NOPE - QK_ROPE), F32)
    ca = jnp.concatenate([cos, cos, zeros], axis=-1)
    sb = jnp.concatenate([-sin, sin, zeros], axis=-1)
    ca = jnp.concatenate([ca, ca, jnp.ones((CTX_ROWS, QK_NOPE), F32)], axis=0)
    sb = jnp.concatenate([sb, sb, jnp.zeros((CTX_ROWS, QK_NOPE), F32)], axis=0)
    return ca, sb


def _pack_w_uq(w_uq):
    w = w_uq.reshape(DEPTH, Q_RANK, MLA_HEADS, QK_HEAD)
    ra, rb = _rope_split(w[..., QK_NOPE:])
    return jnp.concatenate([w[..., :QK_NOPE], ra, rb], axis=-1).reshape(DEPTH, Q_RANK, MLA_HEADS * Q_EXT).astype(BF16)


def _pack_head_gain(g):
    ga, gb = _rope_split(g[:, QK_NOPE:])
    rows = jnp.stack([g[:, :QK_NOPE], ga, gb], axis=1)
    return jnp.concatenate([rows, jnp.zeros((DEPTH, 5, QK_NOPE), F32)], axis=1)


def kernel(x, c, ctx, c_ctx, w_ada, b_ada, norm_mix, norm_ffn, w_in, b_gate, q_a_norm, kv_a_norm, w_uq, w_ukv,
           q_norm, k_norm, w_f_out, w_mla_out, conv_w, w_conv_out, w_out, w_ffn_gate, w_ffn_up, w_ffn_down):
    cond = jnp.concatenate([c, c_ctx[None, :], jnp.zeros((MOD_ROWS - BATCH - 1, D_MODEL), F32)], axis=0)
    bias_pack = jnp.concatenate([b_gate[:, None, :], jnp.zeros((DEPTH, 1, P_COLS - PC_F), F32)], axis=-1)
    wq_ext = _pack_w_uq(w_uq)
    gq = _pack_head_gain(q_norm)
    gk = _pack_head_gain(k_norm)
    conv_w8 = jnp.concatenate([conv_w, jnp.zeros((DEPTH, 5, CONV_W), F32)], axis=1)
    rope_ca, rope_sb = _rope_tables()
    w_chan = _chan_dft_matrix()
    dft_lat = _lat_dft_matrix()
    dft_ctx = _ctx_dft_matrix()
    bf = lambda w: w.astype(BF16)
    w_ukv_b, w_f_b, w_m_b, w_c_b, w_out_b = bf(w_ukv), bf(w_f_out), bf(w_mla_out), bf(w_conv_out), bf(w_out)
    w_g_b, w_u_b, w_d_b = bf(w_ffn_gate), bf(w_ffn_up), bf(w_ffn_down)
    gain_mix, gain_ffn = norm_mix[:, None, :], norm_ffn[:, None, :]
    qa, kva = q_a_norm[:, None, :], kv_a_norm[:, None, :]

    w_pack = _pack_w_in(jnp.swapaxes(w_in, 1, 2))
    mod = _ada(cond, w_ada, b_ada).reshape(DEPTH * MOD_ROWS * N_MOD, 1, D_MODEL)

    x_all = None
    for l in range(DEPTH):
        last = l == DEPTH - 1
        n_tiles = LAT_TILES if last else N_TILES
        if l == 0:
            h, x_all = _prenorm(x.reshape(LAT_ROWS, D_MODEL), ctx.reshape(CTX_ROWS, D_MODEL), 0,
                                gain_mix, mod, l, emit_rows=True)
        else:
            h, = _prenorm(x_all, x_all, LAT_TILES, gain_mix, mod, l, emit_rows=False)
        p = _inproj(h, l, w_pack, bias_pack)
        q, k, v = _qkv(p, l, wq_ext, w_ukv_b, qa, kva, gq, gk, rope_ca, rope_sb)
        a_lat = _attention_lat(q, k, v)
        a_ctx = a_lat if last else _attention_ctx(q, k, v)
        uv = _chan_dft(p, w_chan, n_tiles)
        f_lat = _pos_dft(dft_lat, uv, SEQ, 0, 1024, 2048)
        f_ctx = f_lat if last else _pos_dft(dft_ctx, uv, CTX_LEN, LAT_ROWS, CTX_LEN, CTX_LEN)
        cmix = _short_conv(p, conv_w8, l, n_tiles)
        x_all = _mixout(f_lat, f_ctx, a_lat, a_ctx, cmix, p, x_all, mod, l, w_f_b, w_m_b, w_c_b, w_out_b, n_tiles)
        x_all = _ffn(x_all, gain_ffn, mod, l, w_g_b, w_u_b, w_d_b, n_tiles)
    return x_all.reshape(BATCH, SEQ, D_MODEL)
```

```python
import functools

import numpy as np
import jax
import jax.numpy as jnp
from jax import lax
from jax.experimental import pallas as pl
from jax.experimental.pallas import tpu as pltpu

F32 = jnp.float32
BF16 = jnp.bfloat16

D_MODEL = 2048
BATCH = 2
SEQ = 4096
DEPTH = 2
GRID_W = 64
CTX_LEN = 256
FOURIER_GROUPS = 4
FOURIER_GROUP_W = 128
FOURIER_W = FOURIER_GROUPS * FOURIER_GROUP_W
MLA_HEADS = 8
Q_RANK = 512
KV_RANK = 256
QK_NOPE = 128
QK_ROPE = 64
QK_HEAD = QK_NOPE + QK_ROPE
V_HEAD = 128
ROPE_THETA = 10000.0
AXIS_PAIRS = QK_ROPE // 4
CONV_W = 512
N_BRANCH = 3
D_FF = ((8 * D_MODEL // 3 + 255) // 256) * 256
N_MOD = 6
RMS_EPS = 1e-6

OFF_F = 0
OFF_CQ = OFF_F + FOURIER_W
OFF_CKV = OFF_CQ + Q_RANK
OFF_KR = OFF_CKV + KV_RANK
OFF_CX = OFF_KR + QK_ROPE
OFF_G = OFF_CX + 3 * CONV_W
N_IN = OFF_G + N_BRANCH * D_MODEL

LAT_ROWS = BATCH * SEQ
CTX_ROWS = BATCH * CTX_LEN
ROWS = LAT_ROWS + CTX_ROWS
KEYS = SEQ + CTX_LEN
TM = 512
N_TILES = ROWS // TM
LAT_TILES = LAT_ROWS // TM
TILES_PER_BATCH = SEQ // TM
TQ = 256
LAT_QTILES = SEQ // TQ
MOD_ROWS = 8

P_TN = 1536
P_CHUNK = 512
P_COLS = 6 * P_TN
PC_G = 0
N_GATE_TILES = N_BRANCH * D_MODEL // P_TN
PC_F = PC_G + N_BRANCH * D_MODEL
PC_CQ = PC_F + FOURIER_W
PC_CKV = PC_CQ + Q_RANK
PC_CX = PC_F + P_TN
HEAD_PAD = 256
Q_EXT = 384


def _rms(x, eps=RMS_EPS):
    return x * lax.rsqrt(jnp.mean(x * x, axis=-1, keepdims=True) + eps)


def _sigmoid(z):
    return 1.0 / (1.0 + jnp.exp(-z))


def _rope_split(w):
    half = QK_ROPE // 2
    pad = jnp.zeros(w.shape[:-1] + (QK_NOPE - QK_ROPE,), w.dtype)
    a = jnp.concatenate([w, pad], axis=-1)
    b = jnp.concatenate([w[..., half:], w[..., :half], pad], axis=-1)
    return a, b


ADA_TN = 1024


def _ada_kernel(a_ref, w_ref, b_ref, o_ref):
    a = a_ref[...]
    a = (a * _sigmoid(a)).astype(BF16)
    o_ref[...] = jnp.dot(a, w_ref[...].astype(BF16), preferred_element_type=F32) + b_ref[...]


def _ada(cond, w_ada, b_ada):
    n = N_MOD * D_MODEL
    return pl.pallas_call(
        _ada_kernel,
        grid=(DEPTH, n // ADA_TN),
        in_specs=[
            pl.BlockSpec((MOD_ROWS, D_MODEL), lambda l, j: (0, 0)),
            pl.BlockSpec((None, D_MODEL, ADA_TN), lambda l, j: (l, 0, j)),
            pl.BlockSpec((None, 1, ADA_TN), lambda l, j: (l, 0, j)),
        ],
        out_specs=pl.BlockSpec((None, MOD_ROWS, ADA_TN), lambda l, j: (l, 0, j)),
        out_shape=jax.ShapeDtypeStruct((DEPTH, MOD_ROWS, n), F32),
        name="ada",
    )(cond, w_ada, b_ada.reshape(DEPTH, 1, n))


def _mod_spec(layer, which):
    base = layer * MOD_ROWS * N_MOD + which
    return pl.BlockSpec((None, 1, D_MODEL), lambda i, *_: (base + (i // TILES_PER_BATCH) * N_MOD, 0, 0))


PACK_STEP = P_TN - OFF_CX


def _pack_src_row(j):
    step = jnp.where(j < N_GATE_TILES, OFF_G // PACK_STEP + j * (P_TN // PACK_STEP),
                     jnp.where(j == N_GATE_TILES, 0, OFF_CX // PACK_STEP))
    return step * PACK_STEP


def _pack_tile(j, w_ref, o_ref):
    half = QK_ROPE // 2

    @pl.when(j == N_GATE_TILES)
    def _():
        o_ref[0:OFF_CX, :] = w_ref[0, 0:OFF_CX, :].astype(BF16)
        o_ref[OFF_CX:OFF_KR + QK_NOPE, :] = jnp.zeros((QK_NOPE - QK_ROPE, D_MODEL), BF16)
        o_ref[OFF_KR + QK_NOPE:OFF_KR + QK_NOPE + half, :] = w_ref[0, OFF_KR + half:OFF_CX, :].astype(BF16)
        o_ref[OFF_KR + QK_NOPE + half:OFF_KR + QK_NOPE + QK_ROPE, :] = w_ref[0, OFF_KR:OFF_KR + half, :].astype(BF16)
        o_ref[OFF_KR + QK_NOPE + QK_ROPE:P_TN, :] = jnp.zeros((QK_NOPE - QK_ROPE, D_MODEL), BF16)

    @pl.when(j != N_GATE_TILES)
    def _():
        o_ref[...] = w_ref[0].astype(BF16)


def _rows_specs(ctx_block):
    return [pl.BlockSpec((TM, D_MODEL), lambda i: (jnp.minimum(i, LAT_TILES - 1), 0)),
            pl.BlockSpec((TM, D_MODEL), lambda i: (ctx_block, 0))]


def _prenorm_kernel(xl_ref, xc_ref, gain_ref, shift_ref, scale_ref, h_ref, *rows_ref):
    x = jnp.where(pl.program_id(0) == LAT_TILES, xc_ref[...], xl_ref[...])
    y = _rms(x) * gain_ref[...]
    h_ref[...] = (y * (1.0 + scale_ref[...]) + shift_ref[...]).astype(BF16)
    for r in rows_ref:
        r[...] = x


def _prenorm(x_lat, x_ctx, ctx_block, gain, mod, layer, emit_rows):
    tile = pl.BlockSpec((TM, D_MODEL), lambda i: (i, 0))
    return pl.pallas_call(
        _prenorm_kernel,
        grid=(N_TILES,),
        in_specs=_rows_specs(ctx_block) + [
            pl.BlockSpec((None, 1, D_MODEL), lambda i: (layer, 0, 0)), _mod_spec(layer, 0), _mod_spec(layer, 1)],
        out_specs=[tile] + [tile] * emit_rows,
        out_shape=[jax.ShapeDtypeStruct((ROWS, D_MODEL), BF16)]
        + [jax.ShapeDtypeStruct((ROWS, D_MODEL), F32)] * emit_rows,
        name="prenorm",
    )(x_lat, x_ctx, gain, mod, mod)


def _inproj_kernel(h_ref, w_ref, bias_ref, o_ref, wb_ref):
    j = pl.program_id(0)

    @pl.when(pl.program_id(1) == 0)
    def _():
        _pack_tile(j, w_ref, wb_ref)

    is_gate = j < N_GATE_TILES
    h = h_ref[...]
    for c in range(P_TN // P_CHUNK):
        cols = pl.ds(c * P_CHUNK, P_CHUNK)
        z = lax.dot_general(h, wb_ref[cols, :], (((1,), (1,)), ((), ())), preferred_element_type=F32)
        z = z + bias_ref[:, cols]
        o_ref[:, cols] = jnp.where(is_gate, _sigmoid(z), z).astype(BF16)


def _inproj(h, layer, w_in_t, bias_pack):
    return pl.pallas_call(
        _inproj_kernel,
        grid=(P_COLS // P_TN, N_TILES),
        in_specs=[
            pl.BlockSpec((TM, D_MODEL), lambda j, i: (i, 0)),
            pl.BlockSpec((pl.Element(1), pl.Element(P_TN), pl.Element(D_MODEL)),
                         lambda j, i: (layer, _pack_src_row(j), 0)),
            pl.BlockSpec((None, 1, P_TN), lambda j, i: (layer, 0, j)),
        ],
        out_specs=pl.BlockSpec((TM, P_TN), lambda j, i: (i, j)),
        out_shape=jax.ShapeDtypeStruct((ROWS, P_COLS), BF16),
        scratch_shapes=[pltpu.VMEM((P_TN, D_MODEL), BF16)],
        compiler_params=pltpu.CompilerParams(dimension_semantics=("arbitrary", "arbitrary")),
        name="inproj",
    )(h, w_in_t, bias_pack)


def _qkv_kernel(cq_ref, ckv_ref, wq_ref, wkv_ref, qa_ref, kva_ref, gq_ref, gk_ref, ca_ref, sb_ref,
                q_ref, k_ref, v_ref):
    ca = ca_ref[...]
    sb = sb_ref[...]
    scale = QK_HEAD ** -0.5 * np.log2(np.e)
    ones_col = jnp.where(lax.broadcasted_iota(jnp.int32, (TQ, V_HEAD), 1) == 0, 1.0, 0.0).astype(BF16)

    cqn = (_rms(cq_ref[...].astype(F32)) * qa_ref[...]).astype(BF16)
    qe = jnp.dot(cqn, wq_ref[...], preferred_element_type=F32)
    gq = gq_ref[...]
    rot_a = gq[1:2] * ca
    rot_b = gq[2:3] * sb
    for h in range(MLA_HEADS):
        nope = qe[:, h * Q_EXT:h * Q_EXT + QK_NOPE]
        ra = qe[:, h * Q_EXT + QK_NOPE:h * Q_EXT + 2 * QK_NOPE]
        rb = qe[:, h * Q_EXT + 2 * QK_NOPE:(h + 1) * Q_EXT]
        ssq = jnp.sum(nope * nope, axis=-1, keepdims=True) + jnp.sum(ra * ra, axis=-1, keepdims=True)
        r = lax.rsqrt(ssq * (1.0 / QK_HEAD) + RMS_EPS) * scale
        q_ref[h, :, 0:QK_NOPE] = (nope * gq[0:1] * r).astype(BF16)
        q_ref[h, :, QK_NOPE:HEAD_PAD] = ((ra * rot_a + rb * rot_b) * r).astype(BF16)

    c2 = ckv_ref[...].astype(F32)
    ckvn = (_rms(c2[:, 0:KV_RANK]) * kva_ref[...]).astype(BF16)
    kv = jnp.dot(ckvn, wkv_ref[...], preferred_element_type=F32)
    kra = c2[:, KV_RANK:KV_RANK + QK_NOPE]
    krb = c2[:, KV_RANK + QK_NOPE:KV_RANK + 2 * QK_NOPE]
    gk = gk_ref[...]
    krot = kra * (gk[1:2] * ca) + krb * (gk[2:3] * sb)
    ssq_r = jnp.sum(kra * kra, axis=-1, keepdims=True)
    for h in range(MLA_HEADS):
        knope = kv[:, h * 2 * QK_NOPE:h * 2 * QK_NOPE + QK_NOPE]
        ssq = jnp.sum(knope * knope, axis=-1, keepdims=True) + ssq_r
        r = lax.rsqrt(ssq * (1.0 / QK_HEAD) + RMS_EPS)
        k_ref[0, h, :, 0:QK_NOPE] = (knope * gk[0:1] * r).astype(BF16)
        k_ref[0, h, :, QK_NOPE:HEAD_PAD] = (krot * r).astype(BF16)
        v_ref[0, h, :, 0:V_HEAD] = kv[:, h * 2 * QK_NOPE + QK_NOPE:(h + 1) * 2 * QK_NOPE].astype(BF16)
        v_ref[0, h, :, V_HEAD:2 * V_HEAD] = ones_col


def _kv_index(t):
    lat = t < BATCH * LAT_QTILES
    b = jnp.where(lat, t // LAT_QTILES, t - BATCH * LAT_QTILES)
    pos = jnp.where(lat, t % LAT_QTILES, LAT_QTILES)
    return b, pos


def _qkv(p, layer, wq_ext, w_ukv, qa, kva, gq, gk, rope_ca, rope_sb):
    def kv_map(t):
        b, pos = _kv_index(t)
        return (b, 0, pos, 0)

    def per_layer(shape):
        return pl.BlockSpec((None,) + shape, lambda t: (layer,) + (0,) * len(shape))

    return pl.pallas_call(
        _qkv_kernel,
        grid=(ROWS // TQ,),
        in_specs=[
            pl.BlockSpec((TQ, Q_RANK), lambda t: (t, PC_CQ // Q_RANK)),
            pl.BlockSpec((TQ, 512), lambda t: (t, PC_CKV // 512)),
            per_layer((Q_RANK, MLA_HEADS * Q_EXT)),
            per_layer((KV_RANK, MLA_HEADS * 2 * QK_NOPE)),
            per_layer((1, Q_RANK)),
            per_layer((1, KV_RANK)),
            per_layer((8, QK_NOPE)),
            per_layer((8, QK_NOPE)),
            pl.BlockSpec((TQ, QK_NOPE), lambda t: (t, 0)),
            pl.BlockSpec((TQ, QK_NOPE), lambda t: (t, 0)),
        ],
        out_specs=[
            pl.BlockSpec((MLA_HEADS, TQ, HEAD_PAD), lambda t: (0, t, 0)),
            pl.BlockSpec((1, MLA_HEADS, TQ, HEAD_PAD), kv_map),
            pl.BlockSpec((1, MLA_HEADS, TQ, 2 * V_HEAD), kv_map),
        ],
        out_shape=[
            jax.ShapeDtypeStruct((MLA_HEADS, ROWS, HEAD_PAD), BF16),
            jax.ShapeDtypeStruct((BATCH, MLA_HEADS, KEYS, HEAD_PAD), BF16),
            jax.ShapeDtypeStruct((BATCH, MLA_HEADS, KEYS, 2 * V_HEAD), BF16),
        ],
        name="qkv",
    )(p, p, wq_ext, w_ukv, qa, kva, gq, gk, rope_ca, rope_sb)


def _attn_kernel(q_ref, k_ref, v_ref, o_ref, *, n_sub):
    k = k_ref[...]
    v = v_ref[...]
    for t in range(n_sub):
        q = q_ref[t * TQ:(t + 1) * TQ, :]
        s = lax.dot_general(q, k, (((1,), (1,)), ((), ())), preferred_element_type=F32)
        e = jnp.exp2(s - jnp.max(s, axis=-1, keepdims=True)).astype(BF16)
        o = jnp.dot(e, v, preferred_element_type=F32)
        o_ref[t * TQ:(t + 1) * TQ, :] = (o[:, 0:V_HEAD] / o[:, V_HEAD:V_HEAD + 1]).astype(BF16)


ATT_SUB = 8


def _attention_lat(q, k, v):
    tq = ATT_SUB * TQ
    per_batch = SEQ // tq
    return pl.pallas_call(
        functools.partial(_attn_kernel, n_sub=ATT_SUB),
        grid=(BATCH, MLA_HEADS, per_batch),
        in_specs=[
            pl.BlockSpec((None, tq, HEAD_PAD), lambda b, h, r: (h, b * per_batch + r, 0)),
            pl.BlockSpec((None, None, KEYS, HEAD_PAD), lambda b, h, r: (b, h, 0, 0)),
            pl.BlockSpec((None, None, KEYS, 2 * V_HEAD), lambda b, h, r: (b, h, 0, 0)),
        ],
        out_specs=pl.BlockSpec((tq, V_HEAD), lambda b, h, r: (b * per_batch + r, h)),
        out_shape=jax.ShapeDtypeStruct((LAT_ROWS, MLA_HEADS * V_HEAD), BF16),
        compiler_params=pltpu.CompilerParams(dimension_semantics=("parallel", "parallel", "arbitrary")),
        name="attention_lat",
    )(q, k, v)


def _attention_ctx(q, k, v):
    return pl.pallas_call(
        functools.partial(_attn_kernel, n_sub=1),
        grid=(BATCH, MLA_HEADS),
        in_specs=[
            pl.BlockSpec((None, CTX_LEN, HEAD_PAD), lambda b, h: (h, LAT_ROWS // CTX_LEN + b, 0)),
            pl.BlockSpec((None, None, CTX_LEN, HEAD_PAD), lambda b, h: (b, h, SEQ // CTX_LEN, 0)),
            pl.BlockSpec((None, None, CTX_LEN, 2 * V_HEAD), lambda b, h: (b, h, SEQ // CTX_LEN, 0)),
        ],
        out_specs=pl.BlockSpec((CTX_LEN, V_HEAD), lambda b, h: (b, h)),
        out_shape=jax.ShapeDtypeStruct((CTX_ROWS, MLA_HEADS * V_HEAD), BF16),
        name="attention_ctx",
    )(q, k, v)


def _chan_dft_kernel(pf_ref, w_ref, o_ref):
    uv = jnp.dot(pf_ref[...], w_ref[...], preferred_element_type=F32)
    o_ref[0] = uv[:, 0:FOURIER_W].astype(BF16)
    o_ref[1] = uv[:, FOURIER_W:2 * FOURIER_W].astype(BF16)


def _chan_dft(p, w_chan, n_tiles):
    return pl.pallas_call(
        _chan_dft_kernel,
        grid=(n_tiles,),
        in_specs=[
            pl.BlockSpec((TM, FOURIER_W), lambda i: (i, PC_F // FOURIER_W)),
            pl.BlockSpec((FOURIER_W, 2 * FOURIER_W), lambda i: (0, 0)),
        ],
        out_specs=pl.BlockSpec((2, TM, FOURIER_W), lambda i: (0, i, 0)),
        out_shape=jax.ShapeDtypeStruct((2, n_tiles * TM, FOURIER_W), BF16),
        name="chan_dft",
    )(p, w_chan)


def _pos_dft_kernel(d_ref, uv0_ref, uv1_ref, o_ref, acc_ref):
    k = pl.program_id(1)

    @pl.when(k == 0)
    def _():
        acc_ref[...] = jnp.zeros_like(acc_ref)

    d = d_ref[...]
    acc_ref[0] += jnp.dot(d, uv0_ref[...], preferred_element_type=F32)
    acc_ref[1] += jnp.dot(d, uv1_ref[...], preferred_element_type=F32)

    @pl.when(k == pl.num_programs(1) - 1)
    def _():
        o_ref[...] = acc_ref[...].astype(BF16)


def _pos_dft(dft, uv, length, row0, tm, tk):
    nk_half = length // tk
    kb0 = row0 // tk

    def uv_spec(b):
        return pl.BlockSpec((None, tk, FOURIER_W),
                            lambda m, k: (k // nk_half, kb0 + b * nk_half + k % nk_half, 0))

    out = pl.pallas_call(
        _pos_dft_kernel,
        grid=(length // tm, 2 * nk_half),
        in_specs=[pl.BlockSpec((tm, tk), lambda m, k: (m, k)), uv_spec(0), uv_spec(1)],
        out_specs=pl.BlockSpec((BATCH, tm, FOURIER_W), lambda m, k: (0, m, 0)),
        out_shape=jax.ShapeDtypeStruct((BATCH, length, FOURIER_W), BF16),
        scratch_shapes=[pltpu.VMEM((BATCH, tm, FOURIER_W), F32)],
        compiler_params=pltpu.CompilerParams(dimension_semantics=("parallel", "arbitrary")),
        name="pos_dft_%d" % length,
    )(dft, uv, uv)
    return out.reshape(BATCH * length, FOURIER_W)


HALO = 16
STREAM_STARTS = (0, SEQ, LAT_ROWS, LAT_ROWS + CTX_LEN)


def _conv_kernel(cx_ref, cb_ref, cc_ref, pcx_ref, pcc_ref, ncx_ref, ncc_ref, w_ref, o_ref):
    i = pl.program_id(0)
    u = cc_ref[...].astype(F32) * cx_ref[...].astype(F32)
    row = lax.broadcasted_iota(jnp.int32, (TM, 1), 0)
    g = row + i * TM
    first = functools.reduce(jnp.logical_or, [g == s for s in STREAM_STARTS])
    last = functools.reduce(jnp.logical_or, [g == s - 1 for s in STREAM_STARTS[1:] + (ROWS,)])
    halo_prev = pcc_ref[HALO - 1:HALO, :].astype(F32) * pcx_ref[HALO - 1:HALO, :].astype(F32)
    halo_next = ncc_ref[0:1, :].astype(F32) * ncx_ref[0:1, :].astype(F32)
    up = jnp.where(row == 0, halo_prev, pltpu.roll(u, 1, 0))
    un = jnp.where(row == TM - 1, halo_next, pltpu.roll(u, TM - 1, 0))
    up = jnp.where(first, 0.0, up)
    un = jnp.where(last, 0.0, un)
    w = w_ref[...]
    y = up * w[0:1] + u * w[1:2] + un * w[2:3]
    o_ref[...] = (cb_ref[...].astype(F32) * y).astype(BF16)


def _short_conv(p, conv_w, layer, n_tiles):
    cb0 = PC_CX // CONV_W
    per = TM // HALO
    prev_map = lambda c: (lambda i: (jnp.maximum(i * per - 1, 0), c))
    next_map = lambda c: (lambda i: (jnp.minimum((i + 1) * per, ROWS // HALO - 1), c))
    return pl.pallas_call(
        _conv_kernel,
        grid=(n_tiles,),
        in_specs=[
            pl.BlockSpec((TM, CONV_W), lambda i: (i, cb0)),
            pl.BlockSpec((TM, CONV_W), lambda i: (i, cb0 + 1)),
            pl.BlockSpec((TM, CONV_W), lambda i: (i, cb0 + 2)),
            pl.BlockSpec((HALO, CONV_W), prev_map(cb0)),
            pl.BlockSpec((HALO, CONV_W), prev_map(cb0 + 2)),
            pl.BlockSpec((HALO, CONV_W), next_map(cb0)),
            pl.BlockSpec((HALO, CONV_W), next_map(cb0 + 2)),
            pl.BlockSpec((None, 8, CONV_W), lambda i: (layer, 0, 0)),
        ],
        out_specs=pl.BlockSpec((TM, CONV_W), lambda i: (i, 0)),
        out_shape=jax.ShapeDtypeStruct((n_tiles * TM, CONV_W), BF16),
        name="short_conv",
    )(p, p, p, p, p, p, p, conv_w)


MIX_CHUNK = 512


def _mixout_kernel(fl_ref, fc_ref, al_ref, ac_ref, c_ref, g0_ref, g1_ref, g2_ref, x_ref, gate_ref,
                   wf_ref, wm_ref, wc_ref, wo_ref, o_ref, m_ref):
    is_ctx = pl.program_id(0) == LAT_TILES
    f = jnp.where(is_ctx, fc_ref[...], fl_ref[...])
    a = jnp.where(is_ctx, ac_ref[...], al_ref[...])
    c = c_ref[...]
    for n in range(D_MODEL // MIX_CHUNK):
        cols = pl.ds(n * MIX_CHUNK, MIX_CHUNK)
        y = (g0_ref[:, cols].astype(F32) * jnp.dot(f, wf_ref[:, cols], preferred_element_type=F32)
             + g1_ref[:, cols].astype(F32) * jnp.dot(a, wm_ref[:, cols], preferred_element_type=F32)
             + g2_ref[:, cols].astype(F32) * jnp.dot(c, wc_ref[:, cols], preferred_element_type=F32))
        m_ref[:, cols] = y.astype(BF16)
    m = m_ref[...]
    for n in range(D_MODEL // MIX_CHUNK):
        cols = pl.ds(n * MIX_CHUNK, MIX_CHUNK)
        acc = jnp.dot(m, wo_ref[:, cols], preferred_element_type=F32)
        o_ref[:, cols] = x_ref[:, cols] + gate_ref[:, cols] * acc


def _mixout(f_lat, f_ctx, a_lat, a_ctx, cmix, p, x_all, mod, layer, w_f, w_m, w_c, w_out, n_tiles):
    gb = PC_G // D_MODEL
    lat_or_last = lambda i: (jnp.minimum(i, LAT_TILES - 1), 0)
    resident = lambda rows: pl.BlockSpec((None, rows, D_MODEL), lambda i: (layer, 0, 0),
                                         pipeline_mode=pl.Buffered(1))
    return pl.pallas_call(
        _mixout_kernel,
        grid=(n_tiles,),
        in_specs=[
            pl.BlockSpec((TM, FOURIER_W), lat_or_last),
            pl.BlockSpec((TM, FOURIER_W), lambda i: (0, 0)),
            pl.BlockSpec((TM, MLA_HEADS * V_HEAD), lat_or_last),
            pl.BlockSpec((TM, MLA_HEADS * V_HEAD), lambda i: (0, 0)),
            pl.BlockSpec((TM, CONV_W), lambda i: (i, 0)),
            pl.BlockSpec((TM, D_MODEL), lambda i: (i, gb)),
            pl.BlockSpec((TM, D_MODEL), lambda i: (i, gb + 1)),
            pl.BlockSpec((TM, D_MODEL), lambda i: (i, gb + 2)),
            pl.BlockSpec((TM, D_MODEL), lambda i: (i, 0)),
            _mod_spec(layer, 2),
            resident(FOURIER_W),
            resident(MLA_HEADS * V_HEAD),
            resident(CONV_W),
            resident(D_MODEL),
        ],
        out_specs=pl.BlockSpec((TM, D_MODEL), lambda i: (i, 0)),
        out_shape=jax.ShapeDtypeStruct((n_tiles * TM, D_MODEL), F32),
        scratch_shapes=[pltpu.VMEM((TM, D_MODEL), BF16)],
        name="mixout",
    )(f_lat, f_ctx, a_lat, a_ctx, cmix, p, p, p, x_all, mod, w_f, w_m, w_c, w_out)


FF_TN = 512


def _ffn_kernel(x_ref, gain_ref, shift_ref, scale_ref, gate_ref, wg_ref, wu_ref, wd_ref, o_ref,
                h_ref, acc_ref):
    f = pl.program_id(1)

    @pl.when(f == 0)
    def _():
        y = _rms(x_ref[...]) * gain_ref[...]
        h_ref[...] = (y * (1.0 + scale_ref[...]) + shift_ref[...]).astype(BF16)
        acc_ref[...] = jnp.zeros_like(acc_ref)

    h = h_ref[...]
    g = jnp.dot(h, wg_ref[...], preferred_element_type=F32)
    u = jnp.dot(h, wu_ref[...], preferred_element_type=F32)
    a = (g * _sigmoid(g) * u).astype(BF16)
    acc_ref[...] += jnp.dot(a, wd_ref[...], preferred_element_type=F32)

    @pl.when(f == pl.num_programs(1) - 1)
    def _():
        o_ref[...] = x_ref[...] + gate_ref[...] * acc_ref[...]


def _ffn(x_all, gain, mod, layer, w_gate, w_up, w_down, n_tiles):
    return pl.pallas_call(
        _ffn_kernel,
        grid=(n_tiles, D_FF // FF_TN),
        in_specs=[
            pl.BlockSpec((TM, D_MODEL), lambda i, f: (i, 0)),
            pl.BlockSpec((None, 1, D_MODEL), lambda i, f: (layer, 0, 0)),
            _mod_spec(layer, 3),
            _mod_spec(layer, 4),
            _mod_spec(layer, 5),
            pl.BlockSpec((None, D_MODEL, FF_TN), lambda i, f: (layer, 0, f)),
            pl.BlockSpec((None, D_MODEL, FF_TN), lambda i, f: (layer, 0, f)),
            pl.BlockSpec((None, FF_TN, D_MODEL), lambda i, f: (layer, f, 0)),
        ],
        out_specs=pl.BlockSpec((TM, D_MODEL), lambda i, f: (i, 0)),
        out_shape=jax.ShapeDtypeStruct((n_tiles * TM, D_MODEL), F32),
        scratch_shapes=[pltpu.VMEM((TM, D_MODEL), BF16), pltpu.VMEM((TM, D_MODEL), F32)],
        compiler_params=pltpu.CompilerParams(dimension_semantics=("parallel", "arbitrary")),
        name="ffn",
    )(x_all, gain, mod, mod, mod, w_gate, w_up, w_down)


def _dft_cos_sin(length):
    kn = np.outer(np.arange(length), np.arange(length)) % length
    ang = 2.0 * np.pi * kn / length
    return np.cos(ang), np.sin(ang)


def _chan_dft_matrix():
    c, s = _dft_cos_sin(FOURIER_GROUP_W)
    eye = np.eye(FOURIER_GROUPS)
    return jnp.asarray(np.concatenate([np.kron(eye, c), np.kron(eye, s)], axis=1), F32).astype(BF16)


def _ctx_dft_matrix():
    c, s = _dft_cos_sin(CTX_LEN)
    norm = (CTX_LEN * FOURIER_GROUP_W) ** -0.5
    return jnp.asarray(np.concatenate([c, -s], axis=1) * norm, F32).astype(BF16)


def _lat_dft_matrix():
    r = 64
    n = np.arange(SEQ)
    hi = 2.0 * np.pi * (np.outer(np.arange(r), n) % r) / r
    lo = 2.0 * np.pi * (np.outer(np.arange(r), n) % SEQ) / SEQ
    norm = (SEQ * FOURIER_GROUP_W) ** -0.5
    tab = lambda t: jnp.asarray(t, F32)
    hi_c = jnp.concatenate([tab(np.cos(hi) * norm)] * 2, axis=1)[:, None, :]
    hi_s = jnp.concatenate([tab(np.sin(hi) * norm)] * 2, axis=1)[:, None, :]
    lo_a = jnp.concatenate([tab(np.cos(lo)), tab(-np.sin(lo))], axis=1)[None, :, :]
    lo_b = jnp.concatenate([tab(-np.sin(lo)), tab(-np.cos(lo))], axis=1)[None, :, :]
    return (hi_c * lo_a + hi_s * lo_b).astype(BF16).reshape(SEQ, 2 * SEQ)


def _rope_tables():
    rows = SEQ // GRID_W
    row = jnp.repeat(jnp.arange(rows), GRID_W)
    col = jnp.tile(jnp.arange(GRID_W), rows)
    inv_freq = ROPE_THETA ** (-jnp.arange(AXIS_PAIRS, dtype=F32) / AXIS_PAIRS)
    ang = jnp.concatenate([row[:, None] * inv_freq, col[:, None] * inv_freq], axis=-1)
    cos, sin = jnp.cos(ang), jnp.sin(ang)
    zeros = jnp.zeros((SEQ, QK_NOPE - QK_ROPE), F32)
    ca = jnp.concatenate([cos, cos, zeros], axis=-1)
    sb = jnp.concatenate([-sin, sin, zeros], axis=-1)
    ca = jnp.concatenate([ca, ca, jnp.ones((CTX_ROWS, QK_NOPE), F32)], axis=0)
    sb = jnp.concatenate([sb, sb, jnp.zeros((CTX_ROWS, QK_NOPE), F32)], axis=0)
    return ca, sb


def _pack_w_uq(w_uq):
    w = w_uq.reshape(DEPTH, Q_RANK, MLA_HEADS, QK_HEAD)
    ra, rb = _rope_split(w[..., QK_NOPE:])
    return jnp.concatenate([w[..., :QK_NOPE], ra, rb], axis=-1).reshape(DEPTH, Q_RANK, MLA_HEADS * Q_EXT).astype(BF16)


def _pack_head_gain(g):
    ga, gb = _rope_split(g[:, QK_NOPE:])
    rows = jnp.stack([g[:, :QK_NOPE], ga, gb], axis=1)
    return jnp.concatenate([rows, jnp.zeros((DEPTH, 5, QK_NOPE), F32)], axis=1)


def kernel(x, c, ctx, c_ctx, w_ada, b_ada, norm_mix, norm_ffn, w_in, b_gate, q_a_norm, kv_a_norm, w_uq, w_ukv,
           q_norm, k_norm, w_f_out, w_mla_out, conv_w, w_conv_out, w_out, w_ffn_gate, w_ffn_up, w_ffn_down):
    cond = jnp.concatenate([c, c_ctx[None, :], jnp.zeros((MOD_ROWS - BATCH - 1, D_MODEL), F32)], axis=0)
    bias_pack = jnp.concatenate([b_gate[:, None, :], jnp.zeros((DEPTH, 1, P_COLS - PC_F), F32)], axis=-1)
    wq_ext = _pack_w_uq(w_uq)
    gq = _pack_head_gain(q_norm)
    gk = _pack_head_gain(k_norm)
    conv_w8 = jnp.concatenate([conv_w, jnp.zeros((DEPTH, 5, CONV_W), F32)], axis=1)
    rope_ca, rope_sb = _rope_tables()
    w_chan = _chan_dft_matrix()
    dft_lat = _lat_dft_matrix()
    dft_ctx = _ctx_dft_matrix()
    bf = lambda w: w.astype(BF16)
    w_ukv_b, w_f_b, w_m_b, w_c_b, w_out_b = bf(w_ukv), bf(w_f_out), bf(w_mla_out), bf(w_conv_out), bf(w_out)
    w_g_b, w_u_b, w_d_b = bf(w_ffn_gate), bf(w_ffn_up), bf(w_ffn_down)
    gain_mix, gain_ffn = norm_mix[:, None, :], norm_ffn[:, None, :]
    qa, kva = q_a_norm[:, None, :], kv_a_norm[:, None, :]

    w_in_t = jnp.swapaxes(w_in, 1, 2)
    mod = _ada(cond, w_ada, b_ada).reshape(DEPTH * MOD_ROWS * N_MOD, 1, D_MODEL)

    x_all = None
    for l in range(DEPTH):
        last = l == DEPTH - 1
        n_tiles = LAT_TILES if last else N_TILES
        if l == 0:
            h, x_all = _prenorm(x.reshape(LAT_ROWS, D_MODEL), ctx.reshape(CTX_ROWS, D_MODEL), 0,
                                gain_mix, mod, l, emit_rows=True)
        else:
            h, = _prenorm(x_all, x_all, LAT_TILES, gain_mix, mod, l, emit_rows=False)
        p = _inproj(h, l, w_in_t, bias_pack)
        q, k, v = _qkv(p, l, wq_ext, w_ukv_b, qa, kva, gq, gk, rope_ca, rope_sb)
        a_lat = _attention_lat(q, k, v)
        a_ctx = a_lat if last else _attention_ctx(q, k, v)
        uv = _chan_dft(p, w_chan, n_tiles)
        f_lat = _pos_dft(dft_lat, uv, SEQ, 0, 1024, 2048)
        f_ctx = f_lat if last else _pos_dft(dft_ctx, uv, CTX_LEN, LAT_ROWS, CTX_LEN, CTX_LEN)
        cmix = _short_conv(p, conv_w8, l, n_tiles)
        x_all = _mixout(f_lat, f_ctx, a_lat, a_ctx, cmix, p, x_all, mod, l, w_f_b, w_m_b, w_c_b, w_out_b, n_tiles)
        x_all = _ffn(x_all, gain_ffn, mod, l, w_g_b, w_u_b, w_d_b, n_tiles)
    return x_all.reshape(BATCH, SEQ, D_MODEL)
```

```python
import functools

import numpy as np
import jax
import jax.numpy as jnp
from jax import lax
from jax.experimental import pallas as pl
from jax.experimental.pallas import tpu as pltpu

F32 = jnp.float32
BF16 = jnp.bfloat16

D_MODEL = 2048
BATCH = 2
SEQ = 4096
DEPTH = 2
GRID_W = 64
CTX_LEN = 256
FOURIER_GROUPS = 4
FOURIER_GROUP_W = 128
FOURIER_W = FOURIER_GROUPS * FOURIER_GROUP_W
MLA_HEADS = 8
Q_RANK = 512
KV_RANK = 256
QK_NOPE = 128
QK_ROPE = 64
QK_HEAD = QK_NOPE + QK_ROPE
V_HEAD = 128
ROPE_THETA = 10000.0
AXIS_PAIRS = QK_ROPE // 4
CONV_W = 512
N_BRANCH = 3
D_FF = ((8 * D_MODEL // 3 + 255) // 256) * 256
N_MOD = 6
RMS_EPS = 1e-6

OFF_F = 0
OFF_CQ = OFF_F + FOURIER_W
OFF_CKV = OFF_CQ + Q_RANK
OFF_KR = OFF_CKV + KV_RANK
OFF_CX = OFF_KR + QK_ROPE
OFF_G = OFF_CX + 3 * CONV_W
N_IN = OFF_G + N_BRANCH * D_MODEL

LAT_ROWS = BATCH * SEQ
CTX_ROWS = BATCH * CTX_LEN
ROWS = LAT_ROWS + CTX_ROWS
KEYS = SEQ + CTX_LEN
TM = 512
N_TILES = ROWS // TM
LAT_TILES = LAT_ROWS // TM
TILES_PER_BATCH = SEQ // TM
TQ = 256
LAT_QTILES = SEQ // TQ
MOD_ROWS = 8

P_TN = 1536
P_CHUNK = 512
P_COLS = 6 * P_TN
PC_G = 0
N_GATE_TILES = N_BRANCH * D_MODEL // P_TN
PC_F = PC_G + N_BRANCH * D_MODEL
PC_CQ = PC_F + FOURIER_W
PC_CKV = PC_CQ + Q_RANK
PC_CX = PC_F + P_TN
HEAD_PAD = 256
Q_EXT = 384


def _rms(x, eps=RMS_EPS):
    return x * lax.rsqrt(jnp.mean(x * x, axis=-1, keepdims=True) + eps)


def _sigmoid(z):
    return 1.0 / (1.0 + jnp.exp(-z))


def _rope_split(w):
    half = QK_ROPE // 2
    pad = jnp.zeros(w.shape[:-1] + (QK_NOPE - QK_ROPE,), w.dtype)
    a = jnp.concatenate([w, pad], axis=-1)
    b = jnp.concatenate([w[..., half:], w[..., :half], pad], axis=-1)
    return a, b


ADA_TN = 1024


def _ada_kernel(a_ref, w_ref, b_ref, o_ref):
    a = a_ref[...]
    a = (a * _sigmoid(a)).astype(BF16)
    o_ref[...] = jnp.dot(a, w_ref[...].astype(BF16), preferred_element_type=F32) + b_ref[...]


def _ada(cond, w_ada, b_ada):
    n = N_MOD * D_MODEL
    return pl.pallas_call(
        _ada_kernel,
        grid=(DEPTH, n // ADA_TN),
        in_specs=[
            pl.BlockSpec((MOD_ROWS, D_MODEL), lambda l, j: (0, 0)),
            pl.BlockSpec((None, D_MODEL, ADA_TN), lambda l, j: (l, 0, j)),
            pl.BlockSpec((None, 1, ADA_TN), lambda l, j: (l, 0, j)),
        ],
        out_specs=pl.BlockSpec((None, MOD_ROWS, ADA_TN), lambda l, j: (l, 0, j)),
        out_shape=jax.ShapeDtypeStruct((DEPTH, MOD_ROWS, n), F32),
        name="ada",
    )(cond, w_ada, b_ada.reshape(DEPTH, 1, n))


def _mod_spec(layer, which):
    base = layer * MOD_ROWS * N_MOD + which
    return pl.BlockSpec((None, 1, D_MODEL), lambda i, *_: (base + (i // TILES_PER_BATCH) * N_MOD, 0, 0))


PACK_STEP = P_TN - OFF_CX


def _pack_src_row(j):
    step = jnp.where(j < N_GATE_TILES, OFF_G // PACK_STEP + j * (P_TN // PACK_STEP),
                     jnp.where(j == N_GATE_TILES, 0, OFF_CX // PACK_STEP))
    return step * PACK_STEP


def _pack_tile(j, w_ref, o_ref):
    half = QK_ROPE // 2

    @pl.when(j == N_GATE_TILES)
    def _():
        o_ref[0:OFF_CX, :] = w_ref[0, 0:OFF_CX, :].astype(BF16)
        o_ref[OFF_CX:OFF_KR + QK_NOPE, :] = jnp.zeros((QK_NOPE - QK_ROPE, D_MODEL), BF16)
        o_ref[OFF_KR + QK_NOPE:OFF_KR + QK_NOPE + half, :] = w_ref[0, OFF_KR + half:OFF_CX, :].astype(BF16)
        o_ref[OFF_KR + QK_NOPE + half:OFF_KR + QK_NOPE + QK_ROPE, :] = w_ref[0, OFF_KR:OFF_KR + half, :].astype(BF16)
        o_ref[OFF_KR + QK_NOPE + QK_ROPE:P_TN, :] = jnp.zeros((QK_NOPE - QK_ROPE, D_MODEL), BF16)

    @pl.when(j != N_GATE_TILES)
    def _():
        o_ref[...] = w_ref[0].astype(BF16)


def _rows_specs(ctx_block):
    return [pl.BlockSpec((TM, D_MODEL), lambda i: (jnp.minimum(i, LAT_TILES - 1), 0)),
            pl.BlockSpec((TM, D_MODEL), lambda i: (ctx_block, 0))]


def _prenorm_kernel(xl_ref, xc_ref, gain_ref, shift_ref, scale_ref, h_ref, *rows_ref):
    x = jnp.where(pl.program_id(0) == LAT_TILES, xc_ref[...], xl_ref[...])
    y = _rms(x) * gain_ref[...]
    h_ref[...] = (y * (1.0 + scale_ref[...]) + shift_ref[...]).astype(BF16)
    for r in rows_ref:
        r[...] = x


def _prenorm(x_lat, x_ctx, ctx_block, gain, mod, layer, emit_rows):
    tile = pl.BlockSpec((TM, D_MODEL), lambda i: (i, 0))
    return pl.pallas_call(
        _prenorm_kernel,
        grid=(N_TILES,),
        in_specs=_rows_specs(ctx_block) + [
            pl.BlockSpec((None, 1, D_MODEL), lambda i: (layer, 0, 0)), _mod_spec(layer, 0), _mod_spec(layer, 1)],
        out_specs=[tile] + [tile] * emit_rows,
        out_shape=[jax.ShapeDtypeStruct((ROWS, D_MODEL), BF16)]
        + [jax.ShapeDtypeStruct((ROWS, D_MODEL), F32)] * emit_rows,
        name="prenorm",
    )(x_lat, x_ctx, gain, mod, mod)


IN_TM = ROWS // 8


def _inproj_kernel(h_ref, w_ref, bias_ref, o_ref, wb_ref):
    j = pl.program_id(0)

    @pl.when(pl.program_id(1) == 0)
    def _():
        _pack_tile(j, w_ref, wb_ref)

    is_gate = j < N_GATE_TILES
    h = h_ref[...]
    for c in range(P_TN // P_CHUNK):
        cols = pl.ds(c * P_CHUNK, P_CHUNK)
        z = lax.dot_general(h, wb_ref[cols, :], (((1,), (1,)), ((), ())), preferred_element_type=F32)
        z = z + bias_ref[:, cols]
        o_ref[:, cols] = jnp.where(is_gate, _sigmoid(z), z).astype(BF16)


def _inproj(h, layer, w_in_t, bias_pack):
    return pl.pallas_call(
        _inproj_kernel,
        grid=(P_COLS // P_TN, ROWS // IN_TM),
        in_specs=[
            pl.BlockSpec((IN_TM, D_MODEL), lambda j, i: (i, 0)),
            pl.BlockSpec((pl.Element(1), pl.Element(P_TN), pl.Element(D_MODEL)),
                         lambda j, i: (layer, _pack_src_row(j), 0)),
            pl.BlockSpec((None, 1, P_TN), lambda j, i: (layer, 0, j)),
        ],
        out_specs=pl.BlockSpec((IN_TM, P_TN), lambda j, i: (i, j)),
        out_shape=jax.ShapeDtypeStruct((ROWS, P_COLS), BF16),
        scratch_shapes=[pltpu.VMEM((P_TN, D_MODEL), BF16)],
        compiler_params=pltpu.CompilerParams(dimension_semantics=("arbitrary", "arbitrary")),
        name="inproj",
    )(h, w_in_t, bias_pack)


def _qkv_kernel(cq_ref, ckv_ref, wq_ref, wkv_ref, qa_ref, kva_ref, gq_ref, gk_ref, ca_ref, sb_ref,
                q_ref, k_ref, v_ref):
    ca = ca_ref[...]
    sb = sb_ref[...]
    scale = QK_HEAD ** -0.5 * np.log2(np.e)
    ones_col = jnp.where(lax.broadcasted_iota(jnp.int32, (TQ, V_HEAD), 1) == 0, 1.0, 0.0).astype(BF16)

    cqn = (_rms(cq_ref[...].astype(F32)) * qa_ref[...]).astype(BF16)
    qe = jnp.dot(cqn, wq_ref[...], preferred_element_type=F32)
    gq = gq_ref[...]
    rot_a = gq[1:2] * ca
    rot_b = gq[2:3] * sb
    for h in range(MLA_HEADS):
        nope = qe[:, h * Q_EXT:h * Q_EXT + QK_NOPE]
        ra = qe[:, h * Q_EXT + QK_NOPE:h * Q_EXT + 2 * QK_NOPE]
        rb = qe[:, h * Q_EXT + 2 * QK_NOPE:(h + 1) * Q_EXT]
        ssq = jnp.sum(nope * nope, axis=-1, keepdims=True) + jnp.sum(ra * ra, axis=-1, keepdims=True)
        r = lax.rsqrt(ssq * (1.0 / QK_HEAD) + RMS_EPS) * scale
        q_ref[h, :, 0:QK_NOPE] = (nope * gq[0:1] * r).astype(BF16)
        q_ref[h, :, QK_NOPE:HEAD_PAD] = ((ra * rot_a + rb * rot_b) * r).astype(BF16)

    c2 = ckv_ref[...].astype(F32)
    ckvn = (_rms(c2[:, 0:KV_RANK]) * kva_ref[...]).astype(BF16)
    kv = jnp.dot(ckvn, wkv_ref[...], preferred_element_type=F32)
    kra = c2[:, KV_RANK:KV_RANK + QK_NOPE]
    krb = c2[:, KV_RANK + QK_NOPE:KV_RANK + 2 * QK_NOPE]
    gk = gk_ref[...]
    krot = kra * (gk[1:2] * ca) + krb * (gk[2:3] * sb)
    ssq_r = jnp.sum(kra * kra, axis=-1, keepdims=True)
    for h in range(MLA_HEADS):
        knope = kv[:, h * 2 * QK_NOPE:h * 2 * QK_NOPE + QK_NOPE]
        ssq = jnp.sum(knope * knope, axis=-1, keepdims=True) + ssq_r
        r = lax.rsqrt(ssq * (1.0 / QK_HEAD) + RMS_EPS)
        k_ref[0, h, :, 0:QK_NOPE] = (knope * gk[0:1] * r).astype(BF16)
        k_ref[0, h, :, QK_NOPE:HEAD_PAD] = (krot * r).astype(BF16)
        v_ref[0, h, :, 0:V_HEAD] = kv[:, h * 2 * QK_NOPE + QK_NOPE:(h + 1) * 2 * QK_NOPE].astype(BF16)
        v_ref[0, h, :, V_HEAD:2 * V_HEAD] = ones_col


def _kv_index(t):
    lat = t < BATCH * LAT_QTILES
    b = jnp.where(lat, t // LAT_QTILES, t - BATCH * LAT_QTILES)
    pos = jnp.where(lat, t % LAT_QTILES, LAT_QTILES)
    return b, pos


def _qkv(p, layer, wq_ext, w_ukv, qa, kva, gq, gk, rope_ca, rope_sb):
    def kv_map(t):
        b, pos = _kv_index(t)
        return (b, 0, pos, 0)

    def per_layer(shape):
        return pl.BlockSpec((None,) + shape, lambda t: (layer,) + (0,) * len(shape))

    return pl.pallas_call(
        _qkv_kernel,
        grid=(ROWS // TQ,),
        in_specs=[
            pl.BlockSpec((TQ, Q_RANK), lambda t: (t, PC_CQ // Q_RANK)),
            pl.BlockSpec((TQ, 512), lambda t: (t, PC_CKV // 512)),
            per_layer((Q_RANK, MLA_HEADS * Q_EXT)),
            per_layer((KV_RANK, MLA_HEADS * 2 * QK_NOPE)),
            per_layer((1, Q_RANK)),
            per_layer((1, KV_RANK)),
            per_layer((8, QK_NOPE)),
            per_layer((8, QK_NOPE)),
            pl.BlockSpec((TQ, QK_NOPE), lambda t: (t, 0)),
            pl.BlockSpec((TQ, QK_NOPE), lambda t: (t, 0)),
        ],
        out_specs=[
            pl.BlockSpec((MLA_HEADS, TQ, HEAD_PAD), lambda t: (0, t, 0)),
            pl.BlockSpec((1, MLA_HEADS, TQ, HEAD_PAD), kv_map),
            pl.BlockSpec((1, MLA_HEADS, TQ, 2 * V_HEAD), kv_map),
        ],
        out_shape=[
            jax.ShapeDtypeStruct((MLA_HEADS, ROWS, HEAD_PAD), BF16),
            jax.ShapeDtypeStruct((BATCH, MLA_HEADS, KEYS, HEAD_PAD), BF16),
            jax.ShapeDtypeStruct((BATCH, MLA_HEADS, KEYS, 2 * V_HEAD), BF16),
        ],
        name="qkv",
    )(p, p, wq_ext, w_ukv, qa, kva, gq, gk, rope_ca, rope_sb)


def _attn_kernel(q_ref, k_ref, v_ref, *refs, n_sub):
    n_cast = (len(refs) - 1) // 2
    o_ref = refs[n_cast]
    for src, dst in zip(refs[:n_cast], refs[n_cast + 1:]):
        dst[...] = src[...].astype(BF16)
    k = k_ref[...]
    v = v_ref[...]
    for t in range(n_sub):
        q = q_ref[t * TQ:(t + 1) * TQ, :]
        s = lax.dot_general(q, k, (((1,), (1,)), ((), ())), preferred_element_type=F32)
        e = jnp.exp2(s - jnp.max(s, axis=-1, keepdims=True)).astype(BF16)
        o = jnp.dot(e, v, preferred_element_type=F32)
        o_ref[t * TQ:(t + 1) * TQ, :] = (o[:, 0:V_HEAD] / o[:, V_HEAD:V_HEAD + 1]).astype(BF16)


ATT_SUB = 8


def _attention_lat(q, k, v, layer, weights):
    tq = ATT_SUB * TQ
    per_batch = SEQ // tq
    n_steps = BATCH * MLA_HEADS * per_batch
    step = lambda b, h, r: (b * MLA_HEADS + h) * per_batch + r
    rows = [w.shape[1] // n_steps for w in weights]
    return pl.pallas_call(
        functools.partial(_attn_kernel, n_sub=ATT_SUB),
        grid=(BATCH, MLA_HEADS, per_batch),
        in_specs=[
            pl.BlockSpec((None, tq, HEAD_PAD), lambda b, h, r: (h, b * per_batch + r, 0)),
            pl.BlockSpec((None, None, KEYS, HEAD_PAD), lambda b, h, r: (b, h, 0, 0)),
            pl.BlockSpec((None, None, KEYS, 2 * V_HEAD), lambda b, h, r: (b, h, 0, 0)),
        ] + [pl.BlockSpec((None, n, w.shape[2]), lambda b, h, r: (layer, step(b, h, r), 0))
             for n, w in zip(rows, weights)],
        out_specs=[pl.BlockSpec((tq, V_HEAD), lambda b, h, r: (b * per_batch + r, h))]
        + [pl.BlockSpec((n, w.shape[2]), lambda b, h, r: (step(b, h, r), 0)) for n, w in zip(rows, weights)],
        out_shape=[jax.ShapeDtypeStruct((LAT_ROWS, MLA_HEADS * V_HEAD), BF16)]
        + [jax.ShapeDtypeStruct(w.shape[1:], BF16) for w in weights],
        compiler_params=pltpu.CompilerParams(dimension_semantics=("arbitrary", "arbitrary", "arbitrary")),
        name="attention_lat",
    )(q, k, v, *weights)


def _attention_ctx(q, k, v):
    return pl.pallas_call(
        functools.partial(_attn_kernel, n_sub=1),
        grid=(BATCH, MLA_HEADS),
        in_specs=[
            pl.BlockSpec((None, CTX_LEN, HEAD_PAD), lambda b, h: (h, LAT_ROWS // CTX_LEN + b, 0)),
            pl.BlockSpec((None, None, CTX_LEN, HEAD_PAD), lambda b, h: (b, h, SEQ // CTX_LEN, 0)),
            pl.BlockSpec((None, None, CTX_LEN, 2 * V_HEAD), lambda b, h: (b, h, SEQ // CTX_LEN, 0)),
        ],
        out_specs=pl.BlockSpec((CTX_LEN, V_HEAD), lambda b, h: (b, h)),
        out_shape=jax.ShapeDtypeStruct((CTX_ROWS, MLA_HEADS * V_HEAD), BF16),
        name="attention_ctx",
    )(q, k, v)


def _chan_dft_kernel(pf_ref, w_ref, o_ref):
    uv = jnp.dot(pf_ref[...], w_ref[...], preferred_element_type=F32)
    o_ref[0] = uv[:, 0:FOURIER_W].astype(BF16)
    o_ref[1] = uv[:, FOURIER_W:2 * FOURIER_W].astype(BF16)


def _chan_dft(p, w_chan, n_tiles):
    return pl.pallas_call(
        _chan_dft_kernel,
        grid=(n_tiles,),
        in_specs=[
            pl.BlockSpec((TM, FOURIER_W), lambda i: (i, PC_F // FOURIER_W)),
            pl.BlockSpec((FOURIER_W, 2 * FOURIER_W), lambda i: (0, 0)),
        ],
        out_specs=pl.BlockSpec((2, TM, FOURIER_W), lambda i: (0, i, 0)),
        out_shape=jax.ShapeDtypeStruct((2, n_tiles * TM, FOURIER_W), BF16),
        name="chan_dft",
    )(p, w_chan)


def _pos_dft_kernel(d_ref, uv0_ref, uv1_ref, o_ref, acc_ref):
    k = pl.program_id(1)

    @pl.when(k == 0)
    def _():
        acc_ref[...] = jnp.zeros_like(acc_ref)

    d = d_ref[...]
    acc_ref[0] += jnp.dot(d, uv0_ref[...], preferred_element_type=F32)
    acc_ref[1] += jnp.dot(d, uv1_ref[...], preferred_element_type=F32)

    @pl.when(k == pl.num_programs(1) - 1)
    def _():
        o_ref[...] = acc_ref[...].astype(BF16)


def _pos_dft(dft, uv, length, row0, tm, tk):
    nk_half = length // tk
    kb0 = row0 // tk

    def uv_spec(b):
        return pl.BlockSpec((None, tk, FOURIER_W),
                            lambda m, k: (k // nk_half, kb0 + b * nk_half + k % nk_half, 0))

    out = pl.pallas_call(
        _pos_dft_kernel,
        grid=(length // tm, 2 * nk_half),
        in_specs=[pl.BlockSpec((tm, tk), lambda m, k: (m, k)), uv_spec(0), uv_spec(1)],
        out_specs=pl.BlockSpec((BATCH, tm, FOURIER_W), lambda m, k: (0, m, 0)),
        out_shape=jax.ShapeDtypeStruct((BATCH, length, FOURIER_W), BF16),
        scratch_shapes=[pltpu.VMEM((BATCH, tm, FOURIER_W), F32)],
        compiler_params=pltpu.CompilerParams(dimension_semantics=("parallel", "arbitrary")),
        name="pos_dft_%d" % length,
    )(dft, uv, uv)
    return out.reshape(BATCH * length, FOURIER_W)


HALO = 16
STREAM_STARTS = (0, SEQ, LAT_ROWS, LAT_ROWS + CTX_LEN)


def _conv_kernel(cx_ref, cb_ref, cc_ref, pcx_ref, pcc_ref, ncx_ref, ncc_ref, w_ref, o_ref):
    i = pl.program_id(0)
    u = cc_ref[...].astype(F32) * cx_ref[...].astype(F32)
    row = lax.broadcasted_iota(jnp.int32, (TM, 1), 0)
    g = row + i * TM
    first = functools.reduce(jnp.logical_or, [g == s for s in STREAM_STARTS])
    last = functools.reduce(jnp.logical_or, [g == s - 1 for s in STREAM_STARTS[1:] + (ROWS,)])
    halo_prev = pcc_ref[HALO - 1:HALO, :].astype(F32) * pcx_ref[HALO - 1:HALO, :].astype(F32)
    halo_next = ncc_ref[0:1, :].astype(F32) * ncx_ref[0:1, :].astype(F32)
    up = jnp.where(row == 0, halo_prev, pltpu.roll(u, 1, 0))
    un = jnp.where(row == TM - 1, halo_next, pltpu.roll(u, TM - 1, 0))
    up = jnp.where(first, 0.0, up)
    un = jnp.where(last, 0.0, un)
    w = w_ref[...]
    y = up * w[0:1] + u * w[1:2] + un * w[2:3]
    o_ref[...] = (cb_ref[...].astype(F32) * y).astype(BF16)


def _short_conv(p, conv_w, layer, n_tiles):
    cb0 = PC_CX // CONV_W
    per = TM // HALO
    prev_map = lambda c: (lambda i: (jnp.maximum(i * per - 1, 0), c))
    next_map = lambda c: (lambda i: (jnp.minimum((i + 1) * per, ROWS // HALO - 1), c))
    return pl.pallas_call(
        _conv_kernel,
        grid=(n_tiles,),
        in_specs=[
            pl.BlockSpec((TM, CONV_W), lambda i: (i, cb0)),
            pl.BlockSpec((TM, CONV_W), lambda i: (i, cb0 + 1)),
            pl.BlockSpec((TM, CONV_W), lambda i: (i, cb0 + 2)),
            pl.BlockSpec((HALO, CONV_W), prev_map(cb0)),
            pl.BlockSpec((HALO, CONV_W), prev_map(cb0 + 2)),
            pl.BlockSpec((HALO, CONV_W), next_map(cb0)),
            pl.BlockSpec((HALO, CONV_W), next_map(cb0 + 2)),
            pl.BlockSpec((None, 8, CONV_W), lambda i: (layer, 0, 0)),
        ],
        out_specs=pl.BlockSpec((TM, CONV_W), lambda i: (i, 0)),
        out_shape=jax.ShapeDtypeStruct((n_tiles * TM, CONV_W), BF16),
        name="short_conv",
    )(p, p, p, p, p, p, p, conv_w)


MIX_CHUNK = 512


def _mixout_kernel(fl_ref, fc_ref, al_ref, ac_ref, c_ref, g0_ref, g1_ref, g2_ref, x_ref, gate_ref,
                   wf_ref, wm_ref, wc_ref, wo_ref, o_ref, m_ref):
    is_ctx = pl.program_id(0) == LAT_TILES
    f = jnp.where(is_ctx, fc_ref[...], fl_ref[...])
    a = jnp.where(is_ctx, ac_ref[...], al_ref[...])
    c = c_ref[...]
    for n in range(D_MODEL // MIX_CHUNK):
        cols = pl.ds(n * MIX_CHUNK, MIX_CHUNK)
        y = (g0_ref[:, cols].astype(F32) * jnp.dot(f, wf_ref[:, cols], preferred_element_type=F32)
             + g1_ref[:, cols].astype(F32) * jnp.dot(a, wm_ref[:, cols], preferred_element_type=F32)
             + g2_ref[:, cols].astype(F32) * jnp.dot(c, wc_ref[:, cols], preferred_element_type=F32))
        m_ref[:, cols] = y.astype(BF16)
    m = m_ref[...]
    for n in range(D_MODEL // MIX_CHUNK):
        cols = pl.ds(n * MIX_CHUNK, MIX_CHUNK)
        acc = jnp.dot(m, wo_ref[:, cols], preferred_element_type=F32)
        o_ref[:, cols] = x_ref[:, cols] + gate_ref[:, cols] * acc


def _mixout(f_lat, f_ctx, a_lat, a_ctx, cmix, p, x_all, mod, layer, w_f, w_m, w_c, w_out, n_tiles):
    gb = PC_G // D_MODEL
    lat_or_last = lambda i: (jnp.minimum(i, LAT_TILES - 1), 0)
    resident = lambda rows: pl.BlockSpec((rows, D_MODEL), lambda i: (0, 0), pipeline_mode=pl.Buffered(1))
    return pl.pallas_call(
        _mixout_kernel,
        grid=(n_tiles,),
        in_specs=[
            pl.BlockSpec((TM, FOURIER_W), lat_or_last),
            pl.BlockSpec((TM, FOURIER_W), lambda i: (0, 0)),
            pl.BlockSpec((TM, MLA_HEADS * V_HEAD), lat_or_last),
            pl.BlockSpec((TM, MLA_HEADS * V_HEAD), lambda i: (0, 0)),
            pl.BlockSpec((TM, CONV_W), lambda i: (i, 0)),
            pl.BlockSpec((TM, D_MODEL), lambda i: (i, gb)),
            pl.BlockSpec((TM, D_MODEL), lambda i: (i, gb + 1)),
            pl.BlockSpec((TM, D_MODEL), lambda i: (i, gb + 2)),
            pl.BlockSpec((TM, D_MODEL), lambda i: (i, 0)),
            _mod_spec(layer, 2),
            resident(FOURIER_W),
            resident(MLA_HEADS * V_HEAD),
            resident(CONV_W),
            resident(D_MODEL),
        ],
        out_specs=pl.BlockSpec((TM, D_MODEL), lambda i: (i, 0)),
        out_shape=jax.ShapeDtypeStruct((n_tiles * TM, D_MODEL), F32),
        scratch_shapes=[pltpu.VMEM((TM, D_MODEL), BF16)],
        name="mixout",
    )(f_lat, f_ctx, a_lat, a_ctx, cmix, p, p, p, x_all, mod, w_f, w_m, w_c, w_out)


FF_TN = 512


def _ffn_kernel(x_ref, gain_ref, shift_ref, scale_ref, gate_ref, wg_ref, wu_ref, wd_ref, o_ref,
                h_ref, acc_ref):
    f = pl.program_id(1)

    @pl.when(f == 0)
    def _():
        y = _rms(x_ref[...]) * gain_ref[...]
        h_ref[...] = (y * (1.0 + scale_ref[...]) + shift_ref[...]).astype(BF16)
        acc_ref[...] = jnp.zeros_like(acc_ref)

    h = h_ref[...]
    g = jnp.dot(h, wg_ref[...], preferred_element_type=F32)
    u = jnp.dot(h, wu_ref[...], preferred_element_type=F32)
    a = (g * _sigmoid(g) * u).astype(BF16)
    acc_ref[...] += jnp.dot(a, wd_ref[...], preferred_element_type=F32)

    @pl.when(f == pl.num_programs(1) - 1)
    def _():
        o_ref[...] = x_ref[...] + gate_ref[...] * acc_ref[...]


def _ffn(x_all, gain, mod, layer, w_gate, w_up, w_down, n_tiles):
    return pl.pallas_call(
        _ffn_kernel,
        grid=(n_tiles, D_FF // FF_TN),
        in_specs=[
            pl.BlockSpec((TM, D_MODEL), lambda i, f: (i, 0)),
            pl.BlockSpec((None, 1, D_MODEL), lambda i, f: (layer, 0, 0)),
            _mod_spec(layer, 3),
            _mod_spec(layer, 4),
            _mod_spec(layer, 5),
            pl.BlockSpec((D_MODEL, FF_TN), lambda i, f: (0, f)),
            pl.BlockSpec((D_MODEL, FF_TN), lambda i, f: (0, f)),
            pl.BlockSpec((FF_TN, D_MODEL), lambda i, f: (f, 0)),
        ],
        out_specs=pl.BlockSpec((TM, D_MODEL), lambda i, f: (i, 0)),
        out_shape=jax.ShapeDtypeStruct((n_tiles * TM, D_MODEL), F32),
        scratch_shapes=[pltpu.VMEM((TM, D_MODEL), BF16), pltpu.VMEM((TM, D_MODEL), F32)],
        compiler_params=pltpu.CompilerParams(dimension_semantics=("parallel", "arbitrary")),
        name="ffn",
    )(x_all, gain, mod, mod, mod, w_gate, w_up, w_down)


def _dft_cos_sin(length):
    kn = np.outer(np.arange(length), np.arange(length)) % length
    ang = 2.0 * np.pi * kn / length
    return np.cos(ang), np.sin(ang)


def _chan_dft_matrix():
    c, s = _dft_cos_sin(FOURIER_GROUP_W)
    eye = np.eye(FOURIER_GROUPS)
    return jnp.asarray(np.concatenate([np.kron(eye, c), np.kron(eye, s)], axis=1), F32).astype(BF16)


def _ctx_dft_matrix():
    c, s = _dft_cos_sin(CTX_LEN)
    norm = (CTX_LEN * FOURIER_GROUP_W) ** -0.5
    return jnp.asarray(np.concatenate([c, -s], axis=1) * norm, F32).astype(BF16)


def _lat_dft_matrix():
    r = 64
    n = np.arange(SEQ)
    hi = 2.0 * np.pi * (np.outer(np.arange(r), n) % r) / r
    lo = 2.0 * np.pi * (np.outer(np.arange(r), n) % SEQ) / SEQ
    norm = (SEQ * FOURIER_GROUP_W) ** -0.5
    tab = lambda t: jnp.asarray(t, F32)
    hi_c = jnp.concatenate([tab(np.cos(hi) * norm)] * 2, axis=1)[:, None, :]
    hi_s = jnp.concatenate([tab(np.sin(hi) * norm)] * 2, axis=1)[:, None, :]
    lo_a = jnp.concatenate([tab(np.cos(lo)), tab(-np.sin(lo))], axis=1)[None, :, :]
    lo_b = jnp.concatenate([tab(-np.sin(lo)), tab(-np.cos(lo))], axis=1)[None, :, :]
    return (hi_c * lo_a + hi_s * lo_b).astype(BF16).reshape(SEQ, 2 * SEQ)


def _rope_tables():
    rows = SEQ // GRID_W
    row = jnp.repeat(jnp.arange(rows), GRID_W)
    col = jnp.tile(jnp.arange(GRID_W), rows)
    inv_freq = ROPE_THETA ** (-jnp.arange(AXIS_PAIRS, dtype=F32) / AXIS_PAIRS)
    ang = jnp.concatenate([row[:, None] * inv_freq, col[:, None] * inv_freq], axis=-1)
    cos, sin = jnp.cos(ang), jnp.sin(ang)
    zeros = jnp.zeros((SEQ, QK_NOPE - QK_ROPE), F32)
    ca = jnp.concatenate([cos, cos, zeros], axis=-1)
    sb = jnp.concatenate([-sin, sin, zeros], axis=-1)
    ca = jnp.concatenate([ca, ca, jnp.ones((CTX_ROWS, QK_NOPE), F32)], axis=0)
    sb = jnp.concatenate([sb, sb, jnp.zeros((CTX_ROWS, QK_NOPE), F32)], axis=0)
    return ca, sb


def _pack_w_uq(w_uq):
    w = w_uq.reshape(DEPTH, Q_RANK, MLA_HEADS, QK_HEAD)
    ra, rb = _rope_split(w[..., QK_NOPE:])
    return jnp.concatenate([w[..., :QK_NOPE], ra, rb], axis=-1).reshape(DEPTH, Q_RANK, MLA_HEADS * Q_EXT).astype(BF16)


def _pack_head_gain(g):
    ga, gb = _rope_split(g[:, QK_NOPE:])
    rows = jnp.stack([g[:, :QK_NOPE], ga, gb], axis=1)
    return jnp.concatenate([rows, jnp.zeros((DEPTH, 5, QK_NOPE), F32)], axis=1)


def kernel(x, c, ctx, c_ctx, w_ada, b_ada, norm_mix, norm_ffn, w_in, b_gate, q_a_norm, kv_a_norm, w_uq, w_ukv,
           q_norm, k_norm, w_f_out, w_mla_out, conv_w, w_conv_out, w_out, w_ffn_gate, w_ffn_up, w_ffn_down):
    cond = jnp.concatenate([c, c_ctx[None, :], jnp.zeros((MOD_ROWS - BATCH - 1, D_MODEL), F32)], axis=0)
    bias_pack = jnp.concatenate([b_gate[:, None, :], jnp.zeros((DEPTH, 1, P_COLS - PC_F), F32)], axis=-1)
    wq_ext = _pack_w_uq(w_uq)
    gq = _pack_head_gain(q_norm)
    gk = _pack_head_gain(k_norm)
    conv_w8 = jnp.concatenate([conv_w, jnp.zeros((DEPTH, 5, CONV_W), F32)], axis=1)
    rope_ca, rope_sb = _rope_tables()
    w_chan = _chan_dft_matrix()
    dft_lat = _lat_dft_matrix()
    dft_ctx = _ctx_dft_matrix()
    w_ukv_b = w_ukv.astype(BF16)
    late_weights = [w_f_out, w_mla_out, w_conv_out, w_out, w_ffn_gate, w_ffn_up, w_ffn_down]
    gain_mix, gain_ffn = norm_mix[:, None, :], norm_ffn[:, None, :]
    qa, kva = q_a_norm[:, None, :], kv_a_norm[:, None, :]

    w_in_t = jnp.swapaxes(w_in, 1, 2)
    mod = _ada(cond, w_ada, b_ada).reshape(DEPTH * MOD_ROWS * N_MOD, 1, D_MODEL)

    x_all = None
    for l in range(DEPTH):
        last = l == DEPTH - 1
        n_tiles = LAT_TILES if last else N_TILES
        if l == 0:
            h, x_all = _prenorm(x.reshape(LAT_ROWS, D_MODEL), ctx.reshape(CTX_ROWS, D_MODEL), 0,
                                gain_mix, mod, l, emit_rows=True)
        else:
            h, = _prenorm(x_all, x_all, LAT_TILES, gain_mix, mod, l, emit_rows=False)
        p = _inproj(h, l, w_in_t, bias_pack)
        q, k, v = _qkv(p, l, wq_ext, w_ukv_b, qa, kva, gq, gk, rope_ca, rope_sb)
        a_lat, w_f_b, w_m_b, w_c_b, w_out_b, w_g_b, w_u_b, w_d_b = _attention_lat(q, k, v, l, late_weights)
        a_ctx = a_lat if last else _attention_ctx(q, k, v)
        uv = _chan_dft(p, w_chan, n_tiles)
        f_lat = _pos_dft(dft_lat, uv, SEQ, 0, 1024, 2048)
        f_ctx = f_lat if last else _pos_dft(dft_ctx, uv, CTX_LEN, LAT_ROWS, CTX_LEN, CTX_LEN)
        cmix = _short_conv(p, conv_w8, l, n_tiles)
        x_all = _mixout(f_lat, f_ctx, a_lat, a_ctx, cmix, p, x_all, mod, l, w_f_b, w_m_b, w_c_b, w_out_b, n_tiles)
        x_all = _ffn(x_all, gain_ffn, mod, l, w_g_b, w_u_b, w_d_b, n_tiles)
    return x_all.reshape(BATCH, SEQ, D_MODEL)
```

```python
import functools

import numpy as np
import jax
import jax.numpy as jnp
from jax import lax
from jax.experimental import pallas as pl
from jax.experimental.pallas import tpu as pltpu

F32 = jnp.float32
BF16 = jnp.bfloat16

D_MODEL = 2048
BATCH = 2
SEQ = 4096
DEPTH = 2
GRID_W = 64
CTX_LEN = 256
FOURIER_GROUPS = 4
FOURIER_GROUP_W = 128
FOURIER_W = FOURIER_GROUPS * FOURIER_GROUP_W
MLA_HEADS = 8
Q_RANK = 512
KV_RANK = 256
QK_NOPE = 128
QK_ROPE = 64
QK_HEAD = QK_NOPE + QK_ROPE
V_HEAD = 128
ROPE_THETA = 10000.0
AXIS_PAIRS = QK_ROPE // 4
CONV_W = 512
N_BRANCH = 3
D_FF = ((8 * D_MODEL // 3 + 255) // 256) * 256
N_MOD = 6
RMS_EPS = 1e-6

OFF_F = 0
OFF_CQ = OFF_F + FOURIER_W
OFF_CKV = OFF_CQ + Q_RANK
OFF_KR = OFF_CKV + KV_RANK
OFF_CX = OFF_KR + QK_ROPE
OFF_G = OFF_CX + 3 * CONV_W
N_IN = OFF_G + N_BRANCH * D_MODEL

LAT_ROWS = BATCH * SEQ
CTX_ROWS = BATCH * CTX_LEN
ROWS = LAT_ROWS + CTX_ROWS
KEYS = SEQ + CTX_LEN
TM = 512
N_TILES = ROWS // TM
LAT_TILES = LAT_ROWS // TM
TILES_PER_BATCH = SEQ // TM
TQ = 256
LAT_QTILES = SEQ // TQ
MOD_ROWS = 8

P_TN = 1536
P_CHUNK = 512
P_COLS = 6 * P_TN
PC_G = 0
N_GATE_TILES = N_BRANCH * D_MODEL // P_TN
PC_F = PC_G + N_BRANCH * D_MODEL
PC_CQ = PC_F + FOURIER_W
PC_CKV = PC_CQ + Q_RANK
PC_CX = PC_F + P_TN
HEAD_PAD = 256
Q_EXT = 384


def _rms(x, eps=RMS_EPS):
    return x * lax.rsqrt(jnp.mean(x * x, axis=-1, keepdims=True) + eps)


def _sigmoid(z):
    return 1.0 / (1.0 + jnp.exp(-z))


def _rope_split(w):
    half = QK_ROPE // 2
    pad = jnp.zeros(w.shape[:-1] + (QK_NOPE - QK_ROPE,), w.dtype)
    a = jnp.concatenate([w, pad], axis=-1)
    b = jnp.concatenate([w[..., half:], w[..., :half], pad], axis=-1)
    return a, b


ADA_TN = 1024


def _ada_kernel(a_ref, w_ref, b_ref, o_ref):
    a = a_ref[...]
    a = (a * _sigmoid(a)).astype(BF16)
    o_ref[...] = jnp.dot(a, w_ref[...].astype(BF16), preferred_element_type=F32) + b_ref[...]


def _ada(cond, w_ada, b_ada):
    n = N_MOD * D_MODEL
    return pl.pallas_call(
        _ada_kernel,
        grid=(DEPTH, n // ADA_TN),
        in_specs=[
            pl.BlockSpec((MOD_ROWS, D_MODEL), lambda l, j: (0, 0)),
            pl.BlockSpec((None, D_MODEL, ADA_TN), lambda l, j: (l, 0, j)),
            pl.BlockSpec((None, 1, ADA_TN), lambda l, j: (l, 0, j)),
        ],
        out_specs=pl.BlockSpec((None, MOD_ROWS, ADA_TN), lambda l, j: (l, 0, j)),
        out_shape=jax.ShapeDtypeStruct((DEPTH, MOD_ROWS, n), F32),
        name="ada",
    )(cond, w_ada, b_ada.reshape(DEPTH, 1, n))


def _mod_spec(layer, which):
    base = layer * MOD_ROWS * N_MOD + which
    return pl.BlockSpec((None, 1, D_MODEL), lambda i, *_: (base + (i // TILES_PER_BATCH) * N_MOD, 0, 0))


PACK_STEP = P_TN - OFF_CX


def _pack_src_row(j):
    step = jnp.where(j < N_GATE_TILES, OFF_G // PACK_STEP + j * (P_TN // PACK_STEP),
                     jnp.where(j == N_GATE_TILES, 0, OFF_CX // PACK_STEP))
    return step * PACK_STEP


def _pack_tile(j, w_ref, o_ref):
    half = QK_ROPE // 2

    @pl.when(j == N_GATE_TILES)
    def _():
        o_ref[0:OFF_CX, :] = w_ref[0, 0:OFF_CX, :].astype(BF16)
        o_ref[OFF_CX:OFF_KR + QK_NOPE, :] = jnp.zeros((QK_NOPE - QK_ROPE, D_MODEL), BF16)
        o_ref[OFF_KR + QK_NOPE:OFF_KR + QK_NOPE + half, :] = w_ref[0, OFF_KR + half:OFF_CX, :].astype(BF16)
        o_ref[OFF_KR + QK_NOPE + half:OFF_KR + QK_NOPE + QK_ROPE, :] = w_ref[0, OFF_KR:OFF_KR + half, :].astype(BF16)
        o_ref[OFF_KR + QK_NOPE + QK_ROPE:P_TN, :] = jnp.zeros((QK_NOPE - QK_ROPE, D_MODEL), BF16)

    @pl.when(j != N_GATE_TILES)
    def _():
        o_ref[...] = w_ref[0].astype(BF16)


def _rows_specs(ctx_block):
    return [pl.BlockSpec((TM, D_MODEL), lambda i: (jnp.minimum(i, LAT_TILES - 1), 0)),
            pl.BlockSpec((TM, D_MODEL), lambda i: (ctx_block, 0))]


def _prenorm_kernel(xl_ref, xc_ref, gain_ref, shift_ref, scale_ref, h_ref, *rows_ref):
    x = jnp.where(pl.program_id(0) == LAT_TILES, xc_ref[...], xl_ref[...])
    y = _rms(x) * gain_ref[...]
    h_ref[...] = (y * (1.0 + scale_ref[...]) + shift_ref[...]).astype(BF16)
    for r in rows_ref:
        r[...] = x


def _prenorm(x_lat, x_ctx, ctx_block, gain, mod, layer, emit_rows):
    tile = pl.BlockSpec((TM, D_MODEL), lambda i: (i, 0))
    return pl.pallas_call(
        _prenorm_kernel,
        grid=(N_TILES,),
        in_specs=_rows_specs(ctx_block) + [
            pl.BlockSpec((None, 1, D_MODEL), lambda i: (layer, 0, 0)), _mod_spec(layer, 0), _mod_spec(layer, 1)],
        out_specs=[tile] + [tile] * emit_rows,
        out_shape=[jax.ShapeDtypeStruct((ROWS, D_MODEL), BF16)]
        + [jax.ShapeDtypeStruct((ROWS, D_MODEL), F32)] * emit_rows,
        name="prenorm",
    )(x_lat, x_ctx, gain, mod, mod)


IN_TM = ROWS // 8


def _inproj_kernel(h_ref, w_ref, bias_ref, o_ref, wb_ref):
    j = pl.program_id(0)

    @pl.when(pl.program_id(1) == 0)
    def _():
        _pack_tile(j, w_ref, wb_ref)

    is_gate = j < N_GATE_TILES
    h = h_ref[...]
    for c in range(P_TN // P_CHUNK):
        cols = pl.ds(c * P_CHUNK, P_CHUNK)
        z = lax.dot_general(h, wb_ref[cols, :], (((1,), (1,)), ((), ())), preferred_element_type=F32)
        z = z + bias_ref[:, cols]
        o_ref[:, cols] = jnp.where(is_gate, _sigmoid(z), z).astype(BF16)


def _inproj(h, layer, w_in_t, bias_pack):
    return pl.pallas_call(
        _inproj_kernel,
        grid=(P_COLS // P_TN, ROWS // IN_TM),
        in_specs=[
            pl.BlockSpec((IN_TM, D_MODEL), lambda j, i: (i, 0)),
            pl.BlockSpec((pl.Element(1), pl.Element(P_TN), pl.Element(D_MODEL)),
                         lambda j, i: (layer, _pack_src_row(j), 0)),
            pl.BlockSpec((None, 1, P_TN), lambda j, i: (layer, 0, j)),
        ],
        out_specs=pl.BlockSpec((IN_TM, P_TN), lambda j, i: (i, j)),
        out_shape=jax.ShapeDtypeStruct((ROWS, P_COLS), BF16),
        scratch_shapes=[pltpu.VMEM((P_TN, D_MODEL), BF16)],
        compiler_params=pltpu.CompilerParams(dimension_semantics=("arbitrary", "arbitrary")),
        name="inproj",
    )(h, w_in_t, bias_pack)


def _qkv_kernel(cq_ref, ckv_ref, wq_ref, wkv_ref, qa_ref, kva_ref, gq_ref, gk_ref, ca_ref, sb_ref,
                q_ref, k_ref, v_ref):
    ca = ca_ref[...]
    sb = sb_ref[...]
    scale = QK_HEAD ** -0.5 * np.log2(np.e)
    ones_col = jnp.where(lax.broadcasted_iota(jnp.int32, (TQ, V_HEAD), 1) == 0, 1.0, 0.0).astype(BF16)

    cqn = (_rms(cq_ref[...].astype(F32)) * qa_ref[...]).astype(BF16)
    qe = jnp.dot(cqn, wq_ref[...], preferred_element_type=F32)
    gq = gq_ref[...]
    rot_a = gq[1:2] * ca
    rot_b = gq[2:3] * sb
    for h in range(MLA_HEADS):
        nope = qe[:, h * Q_EXT:h * Q_EXT + QK_NOPE]
        ra = qe[:, h * Q_EXT + QK_NOPE:h * Q_EXT + 2 * QK_NOPE]
        rb = qe[:, h * Q_EXT + 2 * QK_NOPE:(h + 1) * Q_EXT]
        ssq = jnp.sum(nope * nope, axis=-1, keepdims=True) + jnp.sum(ra * ra, axis=-1, keepdims=True)
        r = lax.rsqrt(ssq * (1.0 / QK_HEAD) + RMS_EPS) * scale
        q_ref[h, :, 0:QK_NOPE] = (nope * gq[0:1] * r).astype(BF16)
        q_ref[h, :, QK_NOPE:HEAD_PAD] = ((ra * rot_a + rb * rot_b) * r).astype(BF16)

    c2 = ckv_ref[...].astype(F32)
    ckvn = (_rms(c2[:, 0:KV_RANK]) * kva_ref[...]).astype(BF16)
    kv = jnp.dot(ckvn, wkv_ref[...], preferred_element_type=F32)
    kra = c2[:, KV_RANK:KV_RANK + QK_NOPE]
    krb = c2[:, KV_RANK + QK_NOPE:KV_RANK + 2 * QK_NOPE]
    gk = gk_ref[...]
    krot = kra * (gk[1:2] * ca) + krb * (gk[2:3] * sb)
    ssq_r = jnp.sum(kra * kra, axis=-1, keepdims=True)
    for h in range(MLA_HEADS):
        knope = kv[:, h * 2 * QK_NOPE:h * 2 * QK_NOPE + QK_NOPE]
        ssq = jnp.sum(knope * knope, axis=-1, keepdims=True) + ssq_r
        r = lax.rsqrt(ssq * (1.0 / QK_HEAD) + RMS_EPS)
        k_ref[0, h, :, 0:QK_NOPE] = (knope * gk[0:1] * r).astype(BF16)
        k_ref[0, h, :, QK_NOPE:HEAD_PAD] = (krot * r).astype(BF16)
        v_ref[0, h, :, 0:V_HEAD] = kv[:, h * 2 * QK_NOPE + QK_NOPE:(h + 1) * 2 * QK_NOPE].astype(BF16)
        v_ref[0, h, :, V_HEAD:2 * V_HEAD] = ones_col


def _kv_index(t):
    lat = t < BATCH * LAT_QTILES
    b = jnp.where(lat, t // LAT_QTILES, t - BATCH * LAT_QTILES)
    pos = jnp.where(lat, t % LAT_QTILES, LAT_QTILES)
    return b, pos


def _qkv(p, layer, wq_ext, w_ukv, qa, kva, gq, gk, rope_ca, rope_sb):
    def kv_map(t):
        b, pos = _kv_index(t)
        return (b, 0, pos, 0)

    def per_layer(shape):
        return pl.BlockSpec((None,) + shape, lambda t: (layer,) + (0,) * len(shape))

    return pl.pallas_call(
        _qkv_kernel,
        grid=(ROWS // TQ,),
        in_specs=[
            pl.BlockSpec((TQ, Q_RANK), lambda t: (t, PC_CQ // Q_RANK)),
            pl.BlockSpec((TQ, 512), lambda t: (t, PC_CKV // 512)),
            per_layer((Q_RANK, MLA_HEADS * Q_EXT)),
            per_layer((KV_RANK, MLA_HEADS * 2 * QK_NOPE)),
            per_layer((1, Q_RANK)),
            per_layer((1, KV_RANK)),
            per_layer((8, QK_NOPE)),
            per_layer((8, QK_NOPE)),
            pl.BlockSpec((TQ, QK_NOPE), lambda t: (t, 0)),
            pl.BlockSpec((TQ, QK_NOPE), lambda t: (t, 0)),
        ],
        out_specs=[
            pl.BlockSpec((MLA_HEADS, TQ, HEAD_PAD), lambda t: (0, t, 0)),
            pl.BlockSpec((1, MLA_HEADS, TQ, HEAD_PAD), kv_map),
            pl.BlockSpec((1, MLA_HEADS, TQ, 2 * V_HEAD), kv_map),
        ],
        out_shape=[
            jax.ShapeDtypeStruct((MLA_HEADS, ROWS, HEAD_PAD), BF16),
            jax.ShapeDtypeStruct((BATCH, MLA_HEADS, KEYS, HEAD_PAD), BF16),
            jax.ShapeDtypeStruct((BATCH, MLA_HEADS, KEYS, 2 * V_HEAD), BF16),
        ],
        name="qkv",
    )(p, p, wq_ext, w_ukv, qa, kva, gq, gk, rope_ca, rope_sb)


HALO = 16
STREAM_STARTS = (0, SEQ, LAT_ROWS, LAT_ROWS + CTX_LEN)
N_SIDE_IN = 10


def _conv_slice(row0, cx_ref, cb_ref, cc_ref, pcx_ref, pcc_ref, ncx_ref, ncc_ref, w_ref, o_ref):
    n = cx_ref.shape[0]
    u = cc_ref[...].astype(F32) * cx_ref[...].astype(F32)
    row = lax.broadcasted_iota(jnp.int32, (n, 1), 0)
    g = row + row0
    first = functools.reduce(jnp.logical_or, [g == s for s in STREAM_STARTS])
    last = functools.reduce(jnp.logical_or, [g == s - 1 for s in STREAM_STARTS[1:] + (ROWS,)])
    halo_prev = pcc_ref[HALO - 1:HALO, :].astype(F32) * pcx_ref[HALO - 1:HALO, :].astype(F32)
    halo_next = ncc_ref[0:1, :].astype(F32) * ncx_ref[0:1, :].astype(F32)
    up = jnp.where(row == 0, halo_prev, pltpu.roll(u, 1, 0))
    un = jnp.where(row == n - 1, halo_next, pltpu.roll(u, n - 1, 0))
    up = jnp.where(first, 0.0, up)
    un = jnp.where(last, 0.0, un)
    w = w_ref[...]
    y = up * w[0:1] + u * w[1:2] + un * w[2:3]
    o_ref[...] = (cb_ref[...].astype(F32) * y).astype(BF16)


def _chan_dft_slice(pf_ref, w_ref, o_ref):
    uv = jnp.dot(pf_ref[...], w_ref[...], preferred_element_type=F32)
    o_ref[0] = uv[:, 0:FOURIER_W].astype(BF16)
    o_ref[1] = uv[:, FOURIER_W:2 * FOURIER_W].astype(BF16)


def _side_specs(step, rows, layer):
    cb0 = PC_CX // CONV_W
    per = rows // HALO
    at = lambda c: (lambda *g: (step(*g), c))
    prev = lambda c: (lambda *g: (jnp.maximum(step(*g) * per - 1, 0), c))
    nxt = lambda c: (lambda *g: (jnp.minimum((step(*g) + 1) * per, ROWS // HALO - 1), c))
    in_specs = [
        pl.BlockSpec((rows, CONV_W), at(cb0)),
        pl.BlockSpec((rows, CONV_W), at(cb0 + 1)),
        pl.BlockSpec((rows, CONV_W), at(cb0 + 2)),
        pl.BlockSpec((HALO, CONV_W), prev(cb0)),
        pl.BlockSpec((HALO, CONV_W), prev(cb0 + 2)),
        pl.BlockSpec((HALO, CONV_W), nxt(cb0)),
        pl.BlockSpec((HALO, CONV_W), nxt(cb0 + 2)),
        pl.BlockSpec((None, 8, CONV_W), lambda *g: (layer, 0, 0)),
        pl.BlockSpec((rows, FOURIER_W), at(PC_F // FOURIER_W)),
        pl.BlockSpec((FOURIER_W, 2 * FOURIER_W), lambda *g: (0, 0)),
    ]
    out_specs = [pl.BlockSpec((rows, CONV_W), at(0)),
                 pl.BlockSpec((2, rows, FOURIER_W), lambda *g: (0, step(*g), 0))]
    return in_specs, out_specs


def _attn_kernel(*refs, n_sub, n_cast, side_rows, step):
    q_ref, k_ref, v_ref = refs[:3]
    n_side = N_SIDE_IN if side_rows else 0
    side_in = refs[3:3 + n_side]
    cast_src = refs[3 + n_side:3 + n_side + n_cast]
    o_ref = refs[3 + n_side + n_cast]
    side_out = refs[4 + n_side + n_cast:len(refs) - n_cast]
    cast_dst = refs[len(refs) - n_cast:]
    for src, dst in zip(cast_src, cast_dst):
        dst[...] = src[...].astype(BF16)
    if side_rows:
        row0 = step(pl.program_id(0), pl.program_id(1), pl.program_id(2)) * side_rows
        _conv_slice(row0, *side_in[:8], side_out[0])
        _chan_dft_slice(side_in[8], side_in[9], side_out[1])
    k = k_ref[...]
    v = v_ref[...]
    for t in range(n_sub):
        q = q_ref[t * TQ:(t + 1) * TQ, :]
        s = lax.dot_general(q, k, (((1,), (1,)), ((), ())), preferred_element_type=F32)
        e = jnp.exp2(s - jnp.max(s, axis=-1, keepdims=True)).astype(BF16)
        o = jnp.dot(e, v, preferred_element_type=F32)
        o_ref[t * TQ:(t + 1) * TQ, :] = (o[:, 0:V_HEAD] / o[:, V_HEAD:V_HEAD + 1]).astype(BF16)


ATT_SUB = 8


def _attention_lat(q, k, v, p, conv_w, w_chan, layer, weights, side_total):
    tq = ATT_SUB * TQ
    per_batch = SEQ // tq
    n_steps = BATCH * MLA_HEADS * per_batch
    step = lambda b, h, r: (b * MLA_HEADS + h) * per_batch + r
    side_rows = side_total // n_steps
    side_in, side_out = _side_specs(step, side_rows, layer)
    rows = [w.shape[1] // n_steps for w in weights]
    return pl.pallas_call(
        functools.partial(_attn_kernel, n_sub=ATT_SUB, n_cast=len(weights), side_rows=side_rows, step=step),
        grid=(BATCH, MLA_HEADS, per_batch),
        in_specs=[
            pl.BlockSpec((None, tq, HEAD_PAD), lambda b, h, r: (h, b * per_batch + r, 0)),
            pl.BlockSpec((None, None, KEYS, HEAD_PAD), lambda b, h, r: (b, h, 0, 0)),
            pl.BlockSpec((None, None, KEYS, 2 * V_HEAD), lambda b, h, r: (b, h, 0, 0)),
        ] + side_in
        + [pl.BlockSpec((None, n, w.shape[2]), lambda b, h, r: (layer, step(b, h, r), 0))
           for n, w in zip(rows, weights)],
        out_specs=[pl.BlockSpec((tq, V_HEAD), lambda b, h, r: (b * per_batch + r, h))] + side_out
        + [pl.BlockSpec((n, w.shape[2]), lambda b, h, r: (step(b, h, r), 0)) for n, w in zip(rows, weights)],
        out_shape=[jax.ShapeDtypeStruct((LAT_ROWS, MLA_HEADS * V_HEAD), BF16),
                   jax.ShapeDtypeStruct((side_total, CONV_W), BF16),
                   jax.ShapeDtypeStruct((2, side_total, FOURIER_W), BF16)]
        + [jax.ShapeDtypeStruct(w.shape[1:], BF16) for w in weights],
        compiler_params=pltpu.CompilerParams(dimension_semantics=("arbitrary", "arbitrary", "arbitrary")),
        name="attention_lat",
    )(q, k, v, *([p] * 7), conv_w, p, w_chan, *weights)


def _attention_ctx(q, k, v):
    return pl.pallas_call(
        functools.partial(_attn_kernel, n_sub=1, n_cast=0, side_rows=0, step=None),
        grid=(BATCH, MLA_HEADS),
        in_specs=[
            pl.BlockSpec((None, CTX_LEN, HEAD_PAD), lambda b, h: (h, LAT_ROWS // CTX_LEN + b, 0)),
            pl.BlockSpec((None, None, CTX_LEN, HEAD_PAD), lambda b, h: (b, h, SEQ // CTX_LEN, 0)),
            pl.BlockSpec((None, None, CTX_LEN, 2 * V_HEAD), lambda b, h: (b, h, SEQ // CTX_LEN, 0)),
        ],
        out_specs=pl.BlockSpec((CTX_LEN, V_HEAD), lambda b, h: (b, h)),
        out_shape=jax.ShapeDtypeStruct((CTX_ROWS, MLA_HEADS * V_HEAD), BF16),
        name="attention_ctx",
    )(q, k, v)


def _pos_dft_kernel(d_ref, uv0_ref, uv1_ref, o_ref, acc_ref):
    k = pl.program_id(1)

    @pl.when(k == 0)
    def _():
        acc_ref[...] = jnp.zeros_like(acc_ref)

    d = d_ref[...]
    acc_ref[0] += jnp.dot(d, uv0_ref[...], preferred_element_type=F32)
    acc_ref[1] += jnp.dot(d, uv1_ref[...], preferred_element_type=F32)

    @pl.when(k == pl.num_programs(1) - 1)
    def _():
        o_ref[...] = acc_ref[...].astype(BF16)


def _pos_dft(dft, uv, length, row0, tm, tk):
    nk_half = length // tk
    kb0 = row0 // tk

    def uv_spec(b):
        return pl.BlockSpec((None, tk, FOURIER_W),
                            lambda m, k: (k // nk_half, kb0 + b * nk_half + k % nk_half, 0))

    out = pl.pallas_call(
        _pos_dft_kernel,
        grid=(length // tm, 2 * nk_half),
        in_specs=[pl.BlockSpec((tm, tk), lambda m, k: (m, k)), uv_spec(0), uv_spec(1)],
        out_specs=pl.BlockSpec((BATCH, tm, FOURIER_W), lambda m, k: (0, m, 0)),
        out_shape=jax.ShapeDtypeStruct((BATCH, length, FOURIER_W), BF16),
        scratch_shapes=[pltpu.VMEM((BATCH, tm, FOURIER_W), F32)],
        compiler_params=pltpu.CompilerParams(dimension_semantics=("parallel", "arbitrary")),
        name="pos_dft_%d" % length,
    )(dft, uv, uv)
    return out.reshape(BATCH * length, FOURIER_W)


MIX_CHUNK = 512


def _mixout_kernel(fl_ref, fc_ref, al_ref, ac_ref, c_ref, g0_ref, g1_ref, g2_ref, x_ref, gate_ref,
                   wf_ref, wm_ref, wc_ref, wo_ref, o_ref, m_ref):
    is_ctx = pl.program_id(0) == LAT_TILES
    f = jnp.where(is_ctx, fc_ref[...], fl_ref[...])
    a = jnp.where(is_ctx, ac_ref[...], al_ref[...])
    c = c_ref[...]
    for n in range(D_MODEL // MIX_CHUNK):
        cols = pl.ds(n * MIX_CHUNK, MIX_CHUNK)
        y = (g0_ref[:, cols].astype(F32) * jnp.dot(f, wf_ref[:, cols], preferred_element_type=F32)
             + g1_ref[:, cols].astype(F32) * jnp.dot(a, wm_ref[:, cols], preferred_element_type=F32)
             + g2_ref[:, cols].astype(F32) * jnp.dot(c, wc_ref[:, cols], preferred_element_type=F32))
        m_ref[:, cols] = y.astype(BF16)
    m = m_ref[...]
    for n in range(D_MODEL // MIX_CHUNK):
        cols = pl.ds(n * MIX_CHUNK, MIX_CHUNK)
        acc = jnp.dot(m, wo_ref[:, cols], preferred_element_type=F32)
        o_ref[:, cols] = x_ref[:, cols] + gate_ref[:, cols] * acc


def _mixout(f_lat, f_ctx, a_lat, a_ctx, cmix, p, x_all, mod, layer, w_f, w_m, w_c, w_out, n_tiles):
    gb = PC_G // D_MODEL
    lat_or_last = lambda i: (jnp.minimum(i, LAT_TILES - 1), 0)
    resident = lambda rows: pl.BlockSpec((rows, D_MODEL), lambda i: (0, 0), pipeline_mode=pl.Buffered(1))
    return pl.pallas_call(
        _mixout_kernel,
        grid=(n_tiles,),
        in_specs=[
            pl.BlockSpec((TM, FOURIER_W), lat_or_last),
            pl.BlockSpec((TM, FOURIER_W), lambda i: (0, 0)),
            pl.BlockSpec((TM, MLA_HEADS * V_HEAD), lat_or_last),
            pl.BlockSpec((TM, MLA_HEADS * V_HEAD), lambda i: (0, 0)),
            pl.BlockSpec((TM, CONV_W), lambda i: (i, 0)),
            pl.BlockSpec((TM, D_MODEL), lambda i: (i, gb)),
            pl.BlockSpec((TM, D_MODEL), lambda i: (i, gb + 1)),
            pl.BlockSpec((TM, D_MODEL), lambda i: (i, gb + 2)),
            pl.BlockSpec((TM, D_MODEL), lambda i: (i, 0)),
            _mod_spec(layer, 2),
            resident(FOURIER_W),
            resident(MLA_HEADS * V_HEAD),
            resident(CONV_W),
            resident(D_MODEL),
        ],
        out_specs=pl.BlockSpec((TM, D_MODEL), lambda i: (i, 0)),
        out_shape=jax.ShapeDtypeStruct((n_tiles * TM, D_MODEL), F32),
        scratch_shapes=[pltpu.VMEM((TM, D_MODEL), BF16)],
        name="mixout",
    )(f_lat, f_ctx, a_lat, a_ctx, cmix, p, p, p, x_all, mod, w_f, w_m, w_c, w_out)


FF_TN = 512


def _ffn_kernel(x_ref, gain_ref, shift_ref, scale_ref, gate_ref, wg_ref, wu_ref, wd_ref, o_ref,
                h_ref, acc_ref):
    f = pl.program_id(1)

    @pl.when(f == 0)
    def _():
        y = _rms(x_ref[...]) * gain_ref[...]
        h_ref[...] = (y * (1.0 + scale_ref[...]) + shift_ref[...]).astype(BF16)
        acc_ref[...] = jnp.zeros_like(acc_ref)

    h = h_ref[...]
    g = jnp.dot(h, wg_ref[...], preferred_element_type=F32)
    u = jnp.dot(h, wu_ref[...], preferred_element_type=F32)
    a = (g * _sigmoid(g) * u).astype(BF16)
    acc_ref[...] += jnp.dot(a, wd_ref[...], preferred_element_type=F32)

    @pl.when(f == pl.num_programs(1) - 1)
    def _():
        o_ref[...] = x_ref[...] + gate_ref[...] * acc_ref[...]


def _ffn(x_all, gain, mod, layer, w_gate, w_up, w_down, n_tiles):
    return pl.pallas_call(
        _ffn_kernel,
        grid=(n_tiles, D_FF // FF_TN),
        in_specs=[
            pl.BlockSpec((TM, D_MODEL), lambda i, f: (i, 0)),
            pl.BlockSpec((None, 1, D_MODEL), lambda i, f: (layer, 0, 0)),
            _mod_spec(layer, 3),
            _mod_spec(layer, 4),
            _mod_spec(layer, 5),
            pl.BlockSpec((D_MODEL, FF_TN), lambda i, f: (0, f)),
            pl.BlockSpec((D_MODEL, FF_TN), lambda i, f: (0, f)),
            pl.BlockSpec((FF_TN, D_MODEL), lambda i, f: (f, 0)),
        ],
        out_specs=pl.BlockSpec((TM, D_MODEL), lambda i, f: (i, 0)),
        out_shape=jax.ShapeDtypeStruct((n_tiles * TM, D_MODEL), F32),
        scratch_shapes=[pltpu.VMEM((TM, D_MODEL), BF16), pltpu.VMEM((TM, D_MODEL), F32)],
        compiler_params=pltpu.CompilerParams(dimension_semantics=("parallel", "arbitrary")),
        name="ffn",
    )(x_all, gain, mod, mod, mod, w_gate, w_up, w_down)


def _dft_cos_sin(length):
    kn = np.outer(np.arange(length), np.arange(length)) % length
    ang = 2.0 * np.pi * kn / length
    return np.cos(ang), np.sin(ang)


def _chan_dft_matrix():
    c, s = _dft_cos_sin(FOURIER_GROUP_W)
    eye = np.eye(FOURIER_GROUPS)
    return jnp.asarray(np.concatenate([np.kron(eye, c), np.kron(eye, s)], axis=1), F32).astype(BF16)


def _ctx_dft_matrix():
    c, s = _dft_cos_sin(CTX_LEN)
    norm = (CTX_LEN * FOURIER_GROUP_W) ** -0.5
    return jnp.asarray(np.concatenate([c, -s], axis=1) * norm, F32).astype(BF16)


def _lat_dft_matrix():
    r = 64
    n = np.arange(SEQ)
    hi = 2.0 * np.pi * (np.outer(np.arange(r), n) % r) / r
    lo = 2.0 * np.pi * (np.outer(np.arange(r), n) % SEQ) / SEQ
    norm = (SEQ * FOURIER_GROUP_W) ** -0.5
    tab = lambda t: jnp.asarray(t, F32)
    hi_c = jnp.concatenate([tab(np.cos(hi) * norm)] * 2, axis=1)[:, None, :]
    hi_s = jnp.concatenate([tab(np.sin(hi) * norm)] * 2, axis=1)[:, None, :]
    lo_a = jnp.concatenate([tab(np.cos(lo)), tab(-np.sin(lo))], axis=1)[None, :, :]
    lo_b = jnp.concatenate([tab(-np.sin(lo)), tab(-np.cos(lo))], axis=1)[None, :, :]
    return (hi_c * lo_a + hi_s * lo_b).astype(BF16).reshape(SEQ, 2 * SEQ)


def _rope_tables():
    rows = SEQ // GRID_W
    row = jnp.repeat(jnp.arange(rows), GRID_W)
    col = jnp.tile(jnp.arange(GRID_W), rows)
    inv_freq = ROPE_THETA ** (-jnp.arange(AXIS_PAIRS, dtype=F32) / AXIS_PAIRS)
    ang = jnp.concatenate([row[:, None] * inv_freq, col[:, None] * inv_freq], axis=-1)
    cos, sin = jnp.cos(ang), jnp.sin(ang)
    zeros = jnp.zeros((SEQ, QK_NOPE - QK_ROPE), F32)
    ca = jnp.concatenate([cos, cos, zeros], axis=-1)
    sb = jnp.concatenate([-sin, sin, zeros], axis=-1)
    ca = jnp.concatenate([ca, ca, jnp.ones((CTX_ROWS, QK_NOPE), F32)], axis=0)
    sb = jnp.concatenate([sb, sb, jnp.zeros((CTX_ROWS, QK_NOPE), F32)], axis=0)
    return ca, sb


def _pack_w_uq(w_uq):
    w = w_uq.reshape(DEPTH, Q_RANK, MLA_HEADS, QK_HEAD)
    ra, rb = _rope_split(w[..., QK_NOPE:])
    return jnp.concatenate([w[..., :QK_NOPE], ra, rb], axis=-1).reshape(DEPTH, Q_RANK, MLA_HEADS * Q_EXT).astype(BF16)


def _pack_head_gain(g):
    ga, gb = _rope_split(g[:, QK_NOPE:])
    rows = jnp.stack([g[:, :QK_NOPE], ga, gb], axis=1)
    return jnp.concatenate([rows, jnp.zeros((DEPTH, 5, QK_NOPE), F32)], axis=1)


def kernel(x, c, ctx, c_ctx, w_ada, b_ada, norm_mix, norm_ffn, w_in, b_gate, q_a_norm, kv_a_norm, w_uq, w_ukv,
           q_norm, k_norm, w_f_out, w_mla_out, conv_w, w_conv_out, w_out, w_ffn_gate, w_ffn_up, w_ffn_down):
    cond = jnp.concatenate([c, c_ctx[None, :], jnp.zeros((MOD_ROWS - BATCH - 1, D_MODEL), F32)], axis=0)
    bias_pack = jnp.concatenate([b_gate[:, None, :], jnp.zeros((DEPTH, 1, P_COLS - PC_F), F32)], axis=-1)
    wq_ext = _pack_w_uq(w_uq)
    gq = _pack_head_gain(q_norm)
    gk = _pack_head_gain(k_norm)
    conv_w8 = jnp.concatenate([conv_w, jnp.zeros((DEPTH, 5, CONV_W), F32)], axis=1)
    rope_ca, rope_sb = _rope_tables()
    w_chan = _chan_dft_matrix()
    dft_lat = _lat_dft_matrix()
    dft_ctx = _ctx_dft_matrix()
    w_ukv_b = w_ukv.astype(BF16)
    late_weights = [w_f_out, w_mla_out, w_conv_out, w_out, w_ffn_gate, w_ffn_up, w_ffn_down]
    gain_mix, gain_ffn = norm_mix[:, None, :], norm_ffn[:, None, :]
    qa, kva = q_a_norm[:, None, :], kv_a_norm[:, None, :]

    w_in_t = jnp.swapaxes(w_in, 1, 2)
    mod = _ada(cond, w_ada, b_ada).reshape(DEPTH * MOD_ROWS * N_MOD, 1, D_MODEL)

    x_all = None
    for l in range(DEPTH):
        last = l == DEPTH - 1
        n_tiles = LAT_TILES if last else N_TILES
        if l == 0:
            h, x_all = _prenorm(x.reshape(LAT_ROWS, D_MODEL), ctx.reshape(CTX_ROWS, D_MODEL), 0,
                                gain_mix, mod, l, emit_rows=True)
        else:
            h, = _prenorm(x_all, x_all, LAT_TILES, gain_mix, mod, l, emit_rows=False)
        p = _inproj(h, l, w_in_t, bias_pack)
        q, k, v = _qkv(p, l, wq_ext, w_ukv_b, qa, kva, gq, gk, rope_ca, rope_sb)
        a_lat, cmix, uv, w_f_b, w_m_b, w_c_b, w_out_b, w_g_b, w_u_b, w_d_b = _attention_lat(
            q, k, v, p, conv_w8, w_chan, l, late_weights, n_tiles * TM)
        a_ctx = a_lat if last else _attention_ctx(q, k, v)
        f_lat = _pos_dft(dft_lat, uv, SEQ, 0, 1024, 2048)
        f_ctx = f_lat if last else _pos_dft(dft_ctx, uv, CTX_LEN, LAT_ROWS, CTX_LEN, CTX_LEN)
        x_all = _mixout(f_lat, f_ctx, a_lat, a_ctx, cmix, p, x_all, mod, l, w_f_b, w_m_b, w_c_b, w_out_b, n_tiles)
        x_all = _ffn(x_all, gain_ffn, mod, l, w_g_b, w_u_b, w_d_b, n_tiles)
    return x_all.reshape(BATCH, SEQ, D_MODEL)
```

```python
import functools

import numpy as np
import jax
import jax.numpy as jnp
from jax import lax
from jax.experimental import pallas as pl
from jax.experimental.pallas import tpu as pltpu

F32 = jnp.float32
BF16 = jnp.bfloat16

D_MODEL = 2048
BATCH = 2
SEQ = 4096
DEPTH = 2
GRID_W = 64
CTX_LEN = 256
FOURIER_GROUPS = 4
FOURIER_GROUP_W = 128
FOURIER_W = FOURIER_GROUPS * FOURIER_GROUP_W
MLA_HEADS = 8
Q_RANK = 512
KV_RANK = 256
QK_NOPE = 128
QK_ROPE = 64
QK_HEAD = QK_NOPE + QK_ROPE
V_HEAD = 128
ROPE_THETA = 10000.0
AXIS_PAIRS = QK_ROPE // 4
CONV_W = 512
N_BRANCH = 3
D_FF = ((8 * D_MODEL // 3 + 255) // 256) * 256
N_MOD = 6
RMS_EPS = 1e-6

OFF_F = 0
OFF_CQ = OFF_F + FOURIER_W
OFF_CKV = OFF_CQ + Q_RANK
OFF_KR = OFF_CKV + KV_RANK
OFF_CX = OFF_KR + QK_ROPE
OFF_G = OFF_CX + 3 * CONV_W
N_IN = OFF_G + N_BRANCH * D_MODEL

LAT_ROWS = BATCH * SEQ
CTX_ROWS = BATCH * CTX_LEN
ROWS = LAT_ROWS + CTX_ROWS
KEYS = SEQ + CTX_LEN
TM = 512
N_TILES = ROWS // TM
LAT_TILES = LAT_ROWS // TM
TILES_PER_BATCH = SEQ // TM
TQ = 256
LAT_QTILES = SEQ // TQ
MOD_ROWS = 8

P_TN = 1536
P_CHUNK = 512
P_COLS = 6 * P_TN
PC_G = 0
N_GATE_TILES = N_BRANCH * D_MODEL // P_TN
PC_F = PC_G + N_BRANCH * D_MODEL
PC_CQ = PC_F + FOURIER_W
PC_CKV = PC_CQ + Q_RANK
PC_CX = PC_F + P_TN
HEAD_PAD = 256
Q_EXT = 384


def _rms(x, eps=RMS_EPS):
    return x * lax.rsqrt(jnp.mean(x * x, axis=-1, keepdims=True) + eps)


def _sigmoid(z):
    return 1.0 / (1.0 + jnp.exp(-z))


def _rope_split(w):
    half = QK_ROPE // 2
    pad = jnp.zeros(w.shape[:-1] + (QK_NOPE - QK_ROPE,), w.dtype)
    a = jnp.concatenate([w, pad], axis=-1)
    b = jnp.concatenate([w[..., half:], w[..., :half], pad], axis=-1)
    return a, b


ADA_TN = 1024


def _ada_kernel(a_ref, w_ref, b_ref, o_ref):
    a = a_ref[...]
    a = (a * _sigmoid(a)).astype(BF16)
    o_ref[...] = jnp.dot(a, w_ref[...].astype(BF16), preferred_element_type=F32) + b_ref[...]


def _ada(cond, w_ada, b_ada):
    n = N_MOD * D_MODEL
    return pl.pallas_call(
        _ada_kernel,
        grid=(DEPTH, n // ADA_TN),
        in_specs=[
            pl.BlockSpec((MOD_ROWS, D_MODEL), lambda l, j: (0, 0)),
            pl.BlockSpec((None, D_MODEL, ADA_TN), lambda l, j: (l, 0, j)),
            pl.BlockSpec((None, 1, ADA_TN), lambda l, j: (l, 0, j)),
        ],
        out_specs=pl.BlockSpec((None, MOD_ROWS, ADA_TN), lambda l, j: (l, 0, j)),
        out_shape=jax.ShapeDtypeStruct((DEPTH, MOD_ROWS, n), F32),
        name="ada",
    )(cond, w_ada, b_ada.reshape(DEPTH, 1, n))


def _mod_spec(layer, which):
    base = layer * MOD_ROWS * N_MOD + which
    return pl.BlockSpec((None, 1, D_MODEL), lambda i, *_: (base + (i // TILES_PER_BATCH) * N_MOD, 0, 0))


PACK_STEP = P_TN - OFF_CX


def _pack_src_row(j):
    step = jnp.where(j < N_GATE_TILES, OFF_G // PACK_STEP + j * (P_TN // PACK_STEP),
                     jnp.where(j == N_GATE_TILES, 0, OFF_CX // PACK_STEP))
    return step * PACK_STEP


def _pack_tile(j, w_ref, o_ref):
    half = QK_ROPE // 2

    @pl.when(j == N_GATE_TILES)
    def _():
        o_ref[0:OFF_CX, :] = w_ref[0, 0:OFF_CX, :].astype(BF16)
        o_ref[OFF_CX:OFF_KR + QK_NOPE, :] = jnp.zeros((QK_NOPE - QK_ROPE, D_MODEL), BF16)
        o_ref[OFF_KR + QK_NOPE:OFF_KR + QK_NOPE + half, :] = w_ref[0, OFF_KR + half:OFF_CX, :].astype(BF16)
        o_ref[OFF_KR + QK_NOPE + half:OFF_KR + QK_NOPE + QK_ROPE, :] = w_ref[0, OFF_KR:OFF_KR + half, :].astype(BF16)
        o_ref[OFF_KR + QK_NOPE + QK_ROPE:P_TN, :] = jnp.zeros((QK_NOPE - QK_ROPE, D_MODEL), BF16)

    @pl.when(j != N_GATE_TILES)
    def _():
        o_ref[...] = w_ref[0].astype(BF16)


def _rows_specs(ctx_block):
    return [pl.BlockSpec((TM, D_MODEL), lambda i: (jnp.minimum(i, LAT_TILES - 1), 0)),
            pl.BlockSpec((TM, D_MODEL), lambda i: (ctx_block, 0))]


def _prenorm_kernel(xl_ref, xc_ref, gain_ref, shift_ref, scale_ref, h_ref, *rows_ref):
    x = jnp.where(pl.program_id(0) == LAT_TILES, xc_ref[...], xl_ref[...])
    y = _rms(x) * gain_ref[...]
    h_ref[...] = (y * (1.0 + scale_ref[...]) + shift_ref[...]).astype(BF16)
    for r in rows_ref:
        r[...] = x


def _prenorm(x_lat, x_ctx, ctx_block, gain, mod, layer, emit_rows):
    tile = pl.BlockSpec((TM, D_MODEL), lambda i: (i, 0))
    return pl.pallas_call(
        _prenorm_kernel,
        grid=(N_TILES,),
        in_specs=_rows_specs(ctx_block) + [
            pl.BlockSpec((None, 1, D_MODEL), lambda i: (layer, 0, 0)), _mod_spec(layer, 0), _mod_spec(layer, 1)],
        out_specs=[tile] + [tile] * emit_rows,
        out_shape=[jax.ShapeDtypeStruct((ROWS, D_MODEL), BF16)]
        + [jax.ShapeDtypeStruct((ROWS, D_MODEL), F32)] * emit_rows,
        name="prenorm",
    )(x_lat, x_ctx, gain, mod, mod)


IN_TM = ROWS // 8


def _inproj_kernel(h_ref, w_ref, bias_ref, o_ref, wb_ref):
    j = pl.program_id(0)

    @pl.when(pl.program_id(1) == 0)
    def _():
        _pack_tile(j, w_ref, wb_ref)

    is_gate = j < N_GATE_TILES
    h = h_ref[...]
    for c in range(P_TN // P_CHUNK):
        cols = pl.ds(c * P_CHUNK, P_CHUNK)
        z = lax.dot_general(h, wb_ref[cols, :], (((1,), (1,)), ((), ())), preferred_element_type=F32)
        z = z + bias_ref[:, cols]
        o_ref[:, cols] = jnp.where(is_gate, _sigmoid(z), z).astype(BF16)


def _inproj(h, layer, w_in_t, bias_pack):
    return pl.pallas_call(
        _inproj_kernel,
        grid=(P_COLS // P_TN, ROWS // IN_TM),
        in_specs=[
            pl.BlockSpec((IN_TM, D_MODEL), lambda j, i: (i, 0)),
            pl.BlockSpec((pl.Element(1), pl.Element(P_TN), pl.Element(D_MODEL)),
                         lambda j, i: (layer, _pack_src_row(j), 0)),
            pl.BlockSpec((None, 1, P_TN), lambda j, i: (layer, 0, j)),
        ],
        out_specs=pl.BlockSpec((IN_TM, P_TN), lambda j, i: (i, j)),
        out_shape=jax.ShapeDtypeStruct((ROWS, P_COLS), BF16),
        scratch_shapes=[pltpu.VMEM((P_TN, D_MODEL), BF16)],
        compiler_params=pltpu.CompilerParams(dimension_semantics=("arbitrary", "arbitrary")),
        name="inproj",
    )(h, w_in_t, bias_pack)


def _qkv_kernel(cq_ref, ckv_ref, wq_ref, wkv_ref, qa_ref, kva_ref, gq_ref, gk_ref, ca_ref, sb_ref,
                q_ref, k_ref, v_ref):
    ca = ca_ref[...]
    sb = sb_ref[...]
    scale = QK_HEAD ** -0.5 * np.log2(np.e)
    ones_col = jnp.where(lax.broadcasted_iota(jnp.int32, (TQ, V_HEAD), 1) == 0, 1.0, 0.0).astype(BF16)

    cqn = (_rms(cq_ref[...].astype(F32)) * qa_ref[...]).astype(BF16)
    qe = jnp.dot(cqn, wq_ref[...], preferred_element_type=F32)
    gq = gq_ref[...]
    rot_a = gq[1:2] * ca
    rot_b = gq[2:3] * sb
    for h in range(MLA_HEADS):
        nope = qe[:, h * Q_EXT:h * Q_EXT + QK_NOPE]
        ra = qe[:, h * Q_EXT + QK_NOPE:h * Q_EXT + 2 * QK_NOPE]
        rb = qe[:, h * Q_EXT + 2 * QK_NOPE:(h + 1) * Q_EXT]
        ssq = jnp.sum(nope * nope, axis=-1, keepdims=True) + jnp.sum(ra * ra, axis=-1, keepdims=True)
        r = lax.rsqrt(ssq * (1.0 / QK_HEAD) + RMS_EPS) * scale
        q_ref[h, :, 0:QK_NOPE] = (nope * gq[0:1] * r).astype(BF16)
        q_ref[h, :, QK_NOPE:HEAD_PAD] = ((ra * rot_a + rb * rot_b) * r).astype(BF16)

    c2 = ckv_ref[...].astype(F32)
    ckvn = (_rms(c2[:, 0:KV_RANK]) * kva_ref[...]).astype(BF16)
    kv = jnp.dot(ckvn, wkv_ref[...], preferred_element_type=F32)
    kra = c2[:, KV_RANK:KV_RANK + QK_NOPE]
    krb = c2[:, KV_RANK + QK_NOPE:KV_RANK + 2 * QK_NOPE]
    gk = gk_ref[...]
    krot = kra * (gk[1:2] * ca) + krb * (gk[2:3] * sb)
    ssq_r = jnp.sum(kra * kra, axis=-1, keepdims=True)
    for h in range(MLA_HEADS):
        knope = kv[:, h * 2 * QK_NOPE:h * 2 * QK_NOPE + QK_NOPE]
        ssq = jnp.sum(knope * knope, axis=-1, keepdims=True) + ssq_r
        r = lax.rsqrt(ssq * (1.0 / QK_HEAD) + RMS_EPS)
        k_ref[0, h, :, 0:QK_NOPE] = (knope * gk[0:1] * r).astype(BF16)
        k_ref[0, h, :, QK_NOPE:HEAD_PAD] = (krot * r).astype(BF16)
        v_ref[0, h, :, 0:V_HEAD] = kv[:, h * 2 * QK_NOPE + QK_NOPE:(h + 1) * 2 * QK_NOPE].astype(BF16)
        v_ref[0, h, :, V_HEAD:2 * V_HEAD] = ones_col


def _kv_index(t):
    lat = t < BATCH * LAT_QTILES
    b = jnp.where(lat, t // LAT_QTILES, t - BATCH * LAT_QTILES)
    pos = jnp.where(lat, t % LAT_QTILES, LAT_QTILES)
    return b, pos


def _qkv(p, layer, wq_ext, w_ukv, qa, kva, gq, gk, rope_ca, rope_sb):
    def kv_map(t):
        b, pos = _kv_index(t)
        return (b, 0, pos, 0)

    def per_layer(shape):
        return pl.BlockSpec((None,) + shape, lambda t: (layer,) + (0,) * len(shape))

    return pl.pallas_call(
        _qkv_kernel,
        grid=(ROWS // TQ,),
        in_specs=[
            pl.BlockSpec((TQ, Q_RANK), lambda t: (t, PC_CQ // Q_RANK)),
            pl.BlockSpec((TQ, 512), lambda t: (t, PC_CKV // 512)),
            per_layer((Q_RANK, MLA_HEADS * Q_EXT)),
            per_layer((KV_RANK, MLA_HEADS * 2 * QK_NOPE)),
            per_layer((1, Q_RANK)),
            per_layer((1, KV_RANK)),
            per_layer((8, QK_NOPE)),
            per_layer((8, QK_NOPE)),
            pl.BlockSpec((TQ, QK_NOPE), lambda t: (t, 0)),
            pl.BlockSpec((TQ, QK_NOPE), lambda t: (t, 0)),
        ],
        out_specs=[
            pl.BlockSpec((MLA_HEADS, TQ, HEAD_PAD), lambda t: (0, t, 0)),
            pl.BlockSpec((1, MLA_HEADS, TQ, HEAD_PAD), kv_map),
            pl.BlockSpec((1, MLA_HEADS, TQ, 2 * V_HEAD), kv_map),
        ],
        out_shape=[
            jax.ShapeDtypeStruct((MLA_HEADS, ROWS, HEAD_PAD), BF16),
            jax.ShapeDtypeStruct((BATCH, MLA_HEADS, KEYS, HEAD_PAD), BF16),
            jax.ShapeDtypeStruct((BATCH, MLA_HEADS, KEYS, 2 * V_HEAD), BF16),
        ],
        name="qkv",
    )(p, p, wq_ext, w_ukv, qa, kva, gq, gk, rope_ca, rope_sb)


HALO = 16
STREAM_STARTS = (0, SEQ, LAT_ROWS, LAT_ROWS + CTX_LEN)
N_SIDE_IN = 10


def _conv_slice(row0, cx_ref, cb_ref, cc_ref, pcx_ref, pcc_ref, ncx_ref, ncc_ref, w_ref, o_ref):
    n = cx_ref.shape[0]
    u = cc_ref[...].astype(F32) * cx_ref[...].astype(F32)
    row = lax.broadcasted_iota(jnp.int32, (n, 1), 0)
    g = row + row0
    first = functools.reduce(jnp.logical_or, [g == s for s in STREAM_STARTS])
    last = functools.reduce(jnp.logical_or, [g == s - 1 for s in STREAM_STARTS[1:] + (ROWS,)])
    halo_prev = pcc_ref[HALO - 1:HALO, :].astype(F32) * pcx_ref[HALO - 1:HALO, :].astype(F32)
    halo_next = ncc_ref[0:1, :].astype(F32) * ncx_ref[0:1, :].astype(F32)
    up = jnp.where(row == 0, halo_prev, pltpu.roll(u, 1, 0))
    un = jnp.where(row == n - 1, halo_next, pltpu.roll(u, n - 1, 0))
    up = jnp.where(first, 0.0, up)
    un = jnp.where(last, 0.0, un)
    w = w_ref[...]
    y = up * w[0:1] + u * w[1:2] + un * w[2:3]
    o_ref[...] = (cb_ref[...].astype(F32) * y).astype(BF16)


def _chan_dft_slice(pf_ref, w_ref, o_ref):
    uv = jnp.dot(pf_ref[...], w_ref[...], preferred_element_type=F32)
    o_ref[0] = uv[:, 0:FOURIER_W].astype(BF16)
    o_ref[1] = uv[:, FOURIER_W:2 * FOURIER_W].astype(BF16)


def _side_specs(step, rows, layer):
    cb0 = PC_CX // CONV_W
    per = rows // HALO
    at = lambda c: (lambda *g: (step(*g), c))
    prev = lambda c: (lambda *g: (jnp.maximum(step(*g) * per - 1, 0), c))
    nxt = lambda c: (lambda *g: (jnp.minimum((step(*g) + 1) * per, ROWS // HALO - 1), c))
    in_specs = [
        pl.BlockSpec((rows, CONV_W), at(cb0)),
        pl.BlockSpec((rows, CONV_W), at(cb0 + 1)),
        pl.BlockSpec((rows, CONV_W), at(cb0 + 2)),
        pl.BlockSpec((HALO, CONV_W), prev(cb0)),
        pl.BlockSpec((HALO, CONV_W), prev(cb0 + 2)),
        pl.BlockSpec((HALO, CONV_W), nxt(cb0)),
        pl.BlockSpec((HALO, CONV_W), nxt(cb0 + 2)),
        pl.BlockSpec((None, 8, CONV_W), lambda *g: (layer, 0, 0)),
        pl.BlockSpec((rows, FOURIER_W), at(PC_F // FOURIER_W)),
        pl.BlockSpec((FOURIER_W, 2 * FOURIER_W), lambda *g: (0, 0)),
    ]
    out_specs = [pl.BlockSpec((rows, CONV_W), at(0)),
                 pl.BlockSpec((2, rows, FOURIER_W), lambda *g: (0, step(*g), 0))]
    return in_specs, out_specs


def _attn_kernel(*refs, n_sub, n_cast, side_rows, step):
    q_ref, k_ref, v_ref = refs[:3]
    n_side = N_SIDE_IN if side_rows else 0
    side_in = refs[3:3 + n_side]
    cast_src = refs[3 + n_side:3 + n_side + n_cast]
    o_ref = refs[3 + n_side + n_cast]
    side_out = refs[4 + n_side + n_cast:len(refs) - n_cast]
    cast_dst = refs[len(refs) - n_cast:]
    for src, dst in zip(cast_src, cast_dst):
        dst[...] = src[...].astype(BF16)
    if side_rows:
        row0 = step(pl.program_id(0), pl.program_id(1), pl.program_id(2)) * side_rows
        _conv_slice(row0, *side_in[:8], side_out[0])
        _chan_dft_slice(side_in[8], side_in[9], side_out[1])
    k = k_ref[...]
    v = v_ref[...]
    for t in range(n_sub):
        q = q_ref[t * TQ:(t + 1) * TQ, :]
        s = lax.dot_general(q, k, (((1,), (1,)), ((), ())), preferred_element_type=F32)
        e = jnp.exp2(s - jnp.max(s, axis=-1, keepdims=True)).astype(BF16)
        o = jnp.dot(e, v, preferred_element_type=F32)
        o_ref[t * TQ:(t + 1) * TQ, :] = (o[:, 0:V_HEAD] / o[:, V_HEAD:V_HEAD + 1]).astype(BF16)


ATT_SUB = 8


def _attention_lat(q, k, v, p, conv_w, w_chan, layer, weights, side_total):
    tq = ATT_SUB * TQ
    per_batch = SEQ // tq
    n_steps = BATCH * MLA_HEADS * per_batch
    step = lambda b, h, r: (b * MLA_HEADS + h) * per_batch + r
    side_rows = side_total // n_steps
    side_in, side_out = _side_specs(step, side_rows, layer)
    rows = [w.shape[1] // n_steps for w in weights]
    return pl.pallas_call(
        functools.partial(_attn_kernel, n_sub=ATT_SUB, n_cast=len(weights), side_rows=side_rows, step=step),
        grid=(BATCH, MLA_HEADS, per_batch),
        in_specs=[
            pl.BlockSpec((None, tq, HEAD_PAD), lambda b, h, r: (h, b * per_batch + r, 0)),
            pl.BlockSpec((None, None, KEYS, HEAD_PAD), lambda b, h, r: (b, h, 0, 0)),
            pl.BlockSpec((None, None, KEYS, 2 * V_HEAD), lambda b, h, r: (b, h, 0, 0)),
        ] + side_in
        + [pl.BlockSpec((None, n, w.shape[2]), lambda b, h, r: (layer, step(b, h, r), 0))
           for n, w in zip(rows, weights)],
        out_specs=[pl.BlockSpec((tq, V_HEAD), lambda b, h, r: (b * per_batch + r, h))] + side_out
        + [pl.BlockSpec((n, w.shape[2]), lambda b, h, r: (step(b, h, r), 0)) for n, w in zip(rows, weights)],
        out_shape=[jax.ShapeDtypeStruct((LAT_ROWS, MLA_HEADS * V_HEAD), BF16),
                   jax.ShapeDtypeStruct((side_total, CONV_W), BF16),
                   jax.ShapeDtypeStruct((2, side_total, FOURIER_W), BF16)]
        + [jax.ShapeDtypeStruct(w.shape[1:], BF16) for w in weights],
        compiler_params=pltpu.CompilerParams(dimension_semantics=("arbitrary", "arbitrary", "arbitrary")),
        name="attention_lat",
    )(q, k, v, *([p] * 7), conv_w, p, w_chan, *weights)


def _attention_ctx(q, k, v):
    return pl.pallas_call(
        functools.partial(_attn_kernel, n_sub=1, n_cast=0, side_rows=0, step=None),
        grid=(BATCH, MLA_HEADS),
        in_specs=[
            pl.BlockSpec((None, CTX_LEN, HEAD_PAD), lambda b, h: (h, LAT_ROWS // CTX_LEN + b, 0)),
            pl.BlockSpec((None, None, CTX_LEN, HEAD_PAD), lambda b, h: (b, h, SEQ // CTX_LEN, 0)),
            pl.BlockSpec((None, None, CTX_LEN, 2 * V_HEAD), lambda b, h: (b, h, SEQ // CTX_LEN, 0)),
        ],
        out_specs=pl.BlockSpec((CTX_LEN, V_HEAD), lambda b, h: (b, h)),
        out_shape=jax.ShapeDtypeStruct((CTX_ROWS, MLA_HEADS * V_HEAD), BF16),
        name="attention_ctx",
    )(q, k, v)


def _pos_dft_kernel(de_ref, do_ref, uv0_ref, uv1_ref, o_ref, acc_ref):
    k = pl.program_id(1)

    @pl.when(k == 0)
    def _():
        acc_ref[...] = jnp.zeros_like(acc_ref)

    de = de_ref[...]
    do = do_ref[...]
    for b, uv_ref in enumerate((uv0_ref, uv1_ref)):
        acc_ref[b, 0] += jnp.dot(de, uv_ref[:, 0:FOURIER_W], preferred_element_type=F32)
        acc_ref[b, 1] += jnp.dot(do, uv_ref[:, FOURIER_W:2 * FOURIER_W], preferred_element_type=F32)

    @pl.when(k == pl.num_programs(1) - 1)
    def _():
        for b in range(BATCH):
            e, o = acc_ref[b, 0], acc_ref[b, 1]
            o_ref[b, 0] = (e + o).astype(BF16)
            o_ref[b, 1] = (e - o).astype(BF16)


def _pos_dft(dft, uv, length, row0, tm, tk):
    half = length // 2
    nk_half = half // tk
    kb0 = row0 // 2 // tk
    uv_pairs = uv.reshape(2, uv.shape[1] // 2, 2 * FOURIER_W)

    def uv_spec(b):
        return pl.BlockSpec((None, tk, 2 * FOURIER_W),
                            lambda m, k: (k // nk_half, kb0 + b * nk_half + k % nk_half, 0))

    out = pl.pallas_call(
        _pos_dft_kernel,
        grid=(half // tm, 2 * nk_half),
        in_specs=[pl.BlockSpec((tm, tk), lambda m, k: (m, k)),
                  pl.BlockSpec((tm, tk), lambda m, k: (m, 2 * nk_half + k)),
                  uv_spec(0), uv_spec(1)],
        out_specs=pl.BlockSpec((BATCH, 2, tm, FOURIER_W), lambda m, k: (0, 0, m, 0)),
        out_shape=jax.ShapeDtypeStruct((BATCH, 2, half, FOURIER_W), BF16),
        scratch_shapes=[pltpu.VMEM((BATCH, 2, tm, FOURIER_W), F32)],
        compiler_params=pltpu.CompilerParams(dimension_semantics=("parallel", "arbitrary")),
        name="pos_dft_%d" % length,
    )(dft, dft, uv_pairs, uv_pairs)
    return out.reshape(BATCH * length, FOURIER_W)


MIX_CHUNK = 512


def _mixout_kernel(fl_ref, fc_ref, al_ref, ac_ref, c_ref, g0_ref, g1_ref, g2_ref, x_ref, gate_ref,
                   wf_ref, wm_ref, wc_ref, wo_ref, o_ref, m_ref):
    is_ctx = pl.program_id(0) == LAT_TILES
    f = jnp.where(is_ctx, fc_ref[...], fl_ref[...])
    a = jnp.where(is_ctx, ac_ref[...], al_ref[...])
    c = c_ref[...]
    for n in range(D_MODEL // MIX_CHUNK):
        cols = pl.ds(n * MIX_CHUNK, MIX_CHUNK)
        y = (g0_ref[:, cols].astype(F32) * jnp.dot(f, wf_ref[:, cols], preferred_element_type=F32)
             + g1_ref[:, cols].astype(F32) * jnp.dot(a, wm_ref[:, cols], preferred_element_type=F32)
             + g2_ref[:, cols].astype(F32) * jnp.dot(c, wc_ref[:, cols], preferred_element_type=F32))
        m_ref[:, cols] = y.astype(BF16)
    m = m_ref[...]
    for n in range(D_MODEL // MIX_CHUNK):
        cols = pl.ds(n * MIX_CHUNK, MIX_CHUNK)
        acc = jnp.dot(m, wo_ref[:, cols], preferred_element_type=F32)
        o_ref[:, cols] = x_ref[:, cols] + gate_ref[:, cols] * acc


def _mixout(f_lat, f_ctx, a_lat, a_ctx, cmix, p, x_all, mod, layer, w_f, w_m, w_c, w_out, n_tiles):
    gb = PC_G // D_MODEL
    lat_or_last = lambda i: (jnp.minimum(i, LAT_TILES - 1), 0)
    resident = lambda rows: pl.BlockSpec((rows, D_MODEL), lambda i: (0, 0), pipeline_mode=pl.Buffered(1))
    return pl.pallas_call(
        _mixout_kernel,
        grid=(n_tiles,),
        in_specs=[
            pl.BlockSpec((TM, FOURIER_W), lat_or_last),
            pl.BlockSpec((TM, FOURIER_W), lambda i: (0, 0)),
            pl.BlockSpec((TM, MLA_HEADS * V_HEAD), lat_or_last),
            pl.BlockSpec((TM, MLA_HEADS * V_HEAD), lambda i: (0, 0)),
            pl.BlockSpec((TM, CONV_W), lambda i: (i, 0)),
            pl.BlockSpec((TM, D_MODEL), lambda i: (i, gb)),
            pl.BlockSpec((TM, D_MODEL), lambda i: (i, gb + 1)),
            pl.BlockSpec((TM, D_MODEL), lambda i: (i, gb + 2)),
            pl.BlockSpec((TM, D_MODEL), lambda i: (i, 0)),
            _mod_spec(layer, 2),
            resident(FOURIER_W),
            resident(MLA_HEADS * V_HEAD),
            resident(CONV_W),
            resident(D_MODEL),
        ],
        out_specs=pl.BlockSpec((TM, D_MODEL), lambda i: (i, 0)),
        out_shape=jax.ShapeDtypeStruct((n_tiles * TM, D_MODEL), F32),
        scratch_shapes=[pltpu.VMEM((TM, D_MODEL), BF16)],
        name="mixout",
    )(f_lat, f_ctx, a_lat, a_ctx, cmix, p, p, p, x_all, mod, w_f, w_m, w_c, w_out)


FF_TN = 512


def _ffn_kernel(x_ref, gain_ref, shift_ref, scale_ref, gate_ref, wg_ref, wu_ref, wd_ref, o_ref,
                h_ref, acc_ref):
    f = pl.program_id(1)

    @pl.when(f == 0)
    def _():
        y = _rms(x_ref[...]) * gain_ref[...]
        h_ref[...] = (y * (1.0 + scale_ref[...]) + shift_ref[...]).astype(BF16)
        acc_ref[...] = jnp.zeros_like(acc_ref)

    h = h_ref[...]
    g = jnp.dot(h, wg_ref[...], preferred_element_type=F32)
    u = jnp.dot(h, wu_ref[...], preferred_element_type=F32)
    a = (g * _sigmoid(g) * u).astype(BF16)
    acc_ref[...] += jnp.dot(a, wd_ref[...], preferred_element_type=F32)

    @pl.when(f == pl.num_programs(1) - 1)
    def _():
        o_ref[...] = x_ref[...] + gate_ref[...] * acc_ref[...]


def _ffn(x_all, gain, mod, layer, w_gate, w_up, w_down, n_tiles):
    return pl.pallas_call(
        _ffn_kernel,
        grid=(n_tiles, D_FF // FF_TN),
        in_specs=[
            pl.BlockSpec((TM, D_MODEL), lambda i, f: (i, 0)),
            pl.BlockSpec((None, 1, D_MODEL), lambda i, f: (layer, 0, 0)),
            _mod_spec(layer, 3),
            _mod_spec(layer, 4),
            _mod_spec(layer, 5),
            pl.BlockSpec((D_MODEL, FF_TN), lambda i, f: (0, f)),
            pl.BlockSpec((D_MODEL, FF_TN), lambda i, f: (0, f)),
            pl.BlockSpec((FF_TN, D_MODEL), lambda i, f: (f, 0)),
        ],
        out_specs=pl.BlockSpec((TM, D_MODEL), lambda i, f: (i, 0)),
        out_shape=jax.ShapeDtypeStruct((n_tiles * TM, D_MODEL), F32),
        scratch_shapes=[pltpu.VMEM((TM, D_MODEL), BF16), pltpu.VMEM((TM, D_MODEL), F32)],
        compiler_params=pltpu.CompilerParams(dimension_semantics=("parallel", "arbitrary")),
        name="ffn",
    )(x_all, gain, mod, mod, mod, w_gate, w_up, w_down)


def _dft_cos_sin(length):
    kn = np.outer(np.arange(length), np.arange(length)) % length
    ang = 2.0 * np.pi * kn / length
    return np.cos(ang), np.sin(ang)


def _chan_dft_matrix():
    c, s = _dft_cos_sin(FOURIER_GROUP_W)
    eye = np.eye(FOURIER_GROUPS)
    return jnp.asarray(np.concatenate([np.kron(eye, c), np.kron(eye, s)], axis=1), F32).astype(BF16)


def _dft_column_order(length):
    ev, od = np.arange(0, length, 2), np.arange(1, length, 2)
    return np.concatenate([ev, length + ev, od, length + od])


def _ctx_dft_matrix():
    c, s = _dft_cos_sin(CTX_LEN)
    norm = (CTX_LEN * FOURIER_GROUP_W) ** -0.5
    full = np.concatenate([c, -s], axis=1) * norm
    return jnp.asarray(full[:CTX_LEN // 2][:, _dft_column_order(CTX_LEN)], F32).astype(BF16)


def _lat_dft_matrix():
    r = 64
    n = np.arange(SEQ)
    hi = 2.0 * np.pi * (np.outer(np.arange(SEQ // 2 // r), n) % r) / r
    lo = 2.0 * np.pi * (np.outer(np.arange(r), n) % SEQ) / SEQ
    norm = (SEQ * FOURIER_GROUP_W) ** -0.5
    order = _dft_column_order(SEQ)
    tab = lambda c_part, s_part: jnp.asarray(np.concatenate([c_part, s_part], axis=1)[:, order], F32)
    hi_c = tab(np.cos(hi) * norm, np.cos(hi) * norm)[:, None, :]
    hi_s = tab(np.sin(hi) * norm, np.sin(hi) * norm)[:, None, :]
    lo_a = tab(np.cos(lo), -np.sin(lo))[None, :, :]
    lo_b = tab(-np.sin(lo), -np.cos(lo))[None, :, :]
    return (hi_c * lo_a + hi_s * lo_b).astype(BF16).reshape(SEQ // 2, 2 * SEQ)


def _rope_tables():
    rows = SEQ // GRID_W
    row = jnp.repeat(jnp.arange(rows), GRID_W)
    col = jnp.tile(jnp.arange(GRID_W), rows)
    inv_freq = ROPE_THETA ** (-jnp.arange(AXIS_PAIRS, dtype=F32) / AXIS_PAIRS)
    ang = jnp.concatenate([row[:, None] * inv_freq, col[:, None] * inv_freq], axis=-1)
    cos, sin = jnp.cos(ang), jnp.sin(ang)
    zeros = jnp.zeros((SEQ, QK_NOPE - QK_ROPE), F32)
    ca = jnp.concatenate([cos, cos, zeros], axis=-1)
    sb = jnp.concatenate([-sin, sin, zeros], axis=-1)
    ca = jnp.concatenate([ca, ca, jnp.ones((CTX_ROWS, QK_NOPE), F32)], axis=0)
    sb = jnp.concatenate([sb, sb, jnp.zeros((CTX_ROWS, QK_NOPE), F32)], axis=0)
    return ca, sb


def _pack_w_uq(w_uq):
    w = w_uq.reshape(DEPTH, Q_RANK, MLA_HEADS, QK_HEAD)
    ra, rb = _rope_split(w[..., QK_NOPE:])
    return jnp.concatenate([w[..., :QK_NOPE], ra, rb], axis=-1).reshape(DEPTH, Q_RANK, MLA_HEADS * Q_EXT).astype(BF16)


def _pack_head_gain(g):
    ga, gb = _rope_split(g[:, QK_NOPE:])
    rows = jnp.stack([g[:, :QK_NOPE], ga, gb], axis=1)
    return jnp.concatenate([rows, jnp.zeros((DEPTH, 5, QK_NOPE), F32)], axis=1)


def kernel(x, c, ctx, c_ctx, w_ada, b_ada, norm_mix, norm_ffn, w_in, b_gate, q_a_norm, kv_a_norm, w_uq, w_ukv,
           q_norm, k_norm, w_f_out, w_mla_out, conv_w, w_conv_out, w_out, w_ffn_gate, w_ffn_up, w_ffn_down):
    cond = jnp.concatenate([c, c_ctx[None, :], jnp.zeros((MOD_ROWS - BATCH - 1, D_MODEL), F32)], axis=0)
    bias_pack = jnp.concatenate([b_gate[:, None, :], jnp.zeros((DEPTH, 1, P_COLS - PC_F), F32)], axis=-1)
    wq_ext = _pack_w_uq(w_uq)
    gq = _pack_head_gain(q_norm)
    gk = _pack_head_gain(k_norm)
    conv_w8 = jnp.concatenate([conv_w, jnp.zeros((DEPTH, 5, CONV_W), F32)], axis=1)
    rope_ca, rope_sb = _rope_tables()
    w_chan = _chan_dft_matrix()
    dft_lat = _lat_dft_matrix()
    dft_ctx = _ctx_dft_matrix()
    w_ukv_b = w_ukv.astype(BF16)
    late_weights = [w_f_out, w_mla_out, w_conv_out, w_out, w_ffn_gate, w_ffn_up, w_ffn_down]
    gain_mix, gain_ffn = norm_mix[:, None, :], norm_ffn[:, None, :]
    qa, kva = q_a_norm[:, None, :], kv_a_norm[:, None, :]

    w_in_t = jnp.swapaxes(w_in, 1, 2)
    mod = _ada(cond, w_ada, b_ada).reshape(DEPTH * MOD_ROWS * N_MOD, 1, D_MODEL)

    x_all = None
    for l in range(DEPTH):
        last = l == DEPTH - 1
        n_tiles = LAT_TILES if last else N_TILES
        if l == 0:
            h, x_all = _prenorm(x.reshape(LAT_ROWS, D_MODEL), ctx.reshape(CTX_ROWS, D_MODEL), 0,
                                gain_mix, mod, l, emit_rows=True)
        else:
            h, = _prenorm(x_all, x_all, LAT_TILES, gain_mix, mod, l, emit_rows=False)
        p = _inproj(h, l, w_in_t, bias_pack)
        q, k, v = _qkv(p, l, wq_ext, w_ukv_b, qa, kva, gq, gk, rope_ca, rope_sb)
        a_lat, cmix, uv, w_f_b, w_m_b, w_c_b, w_out_b, w_g_b, w_u_b, w_d_b = _attention_lat(
            q, k, v, p, conv_w8, w_chan, l, late_weights, n_tiles * TM)
        a_ctx = a_lat if last else _attention_ctx(q, k, v)
        f_lat = _pos_dft(dft_lat, uv, SEQ, 0, 512, SEQ // 2)
        f_ctx = f_lat if last else _pos_dft(dft_ctx, uv, CTX_LEN, LAT_ROWS, CTX_LEN // 2, CTX_LEN // 2)
        x_all = _mixout(f_lat, f_ctx, a_lat, a_ctx, cmix, p, x_all, mod, l, w_f_b, w_m_b, w_c_b, w_out_b, n_tiles)
        x_all = _ffn(x_all, gain_ffn, mod, l, w_g_b, w_u_b, w_d_b, n_tiles)
    return x_all.reshape(BATCH, SEQ, D_MODEL)
```

```python
import functools

import numpy as np
import jax
import jax.numpy as jnp
from jax import lax
from jax.experimental import pallas as pl
from jax.experimental.pallas import tpu as pltpu

F32 = jnp.float32
BF16 = jnp.bfloat16

D_MODEL = 2048
BATCH = 2
SEQ = 4096
DEPTH = 2
GRID_W = 64
CTX_LEN = 256
FOURIER_GROUPS = 4
FOURIER_GROUP_W = 128
FOURIER_W = FOURIER_GROUPS * FOURIER_GROUP_W
MLA_HEADS = 8
Q_RANK = 512
KV_RANK = 256
QK_NOPE = 128
QK_ROPE = 64
QK_HEAD = QK_NOPE + QK_ROPE
V_HEAD = 128
ROPE_THETA = 10000.0
AXIS_PAIRS = QK_ROPE // 4
CONV_W = 512
N_BRANCH = 3
D_FF = ((8 * D_MODEL // 3 + 255) // 256) * 256
N_MOD = 6
RMS_EPS = 1e-6

OFF_F = 0
OFF_CQ = OFF_F + FOURIER_W
OFF_CKV = OFF_CQ + Q_RANK
OFF_KR = OFF_CKV + KV_RANK
OFF_CX = OFF_KR + QK_ROPE
OFF_G = OFF_CX + 3 * CONV_W
N_IN = OFF_G + N_BRANCH * D_MODEL

LAT_ROWS = BATCH * SEQ
CTX_ROWS = BATCH * CTX_LEN
ROWS = LAT_ROWS + CTX_ROWS
KEYS = SEQ + CTX_LEN
TM = 512
N_TILES = ROWS // TM
LAT_TILES = LAT_ROWS // TM
TILES_PER_BATCH = SEQ // TM
TQ = 256
LAT_QTILES = SEQ // TQ
MOD_ROWS = 8

P_TN = 1536
P_CHUNK = 512
P_COLS = 6 * P_TN
PC_G = 0
N_GATE_TILES = N_BRANCH * D_MODEL // P_TN
PC_F = PC_G + N_BRANCH * D_MODEL
PC_CQ = PC_F + FOURIER_W
PC_CKV = PC_CQ + Q_RANK
PC_CX = PC_F + P_TN
HEAD_PAD = 256
Q_EXT = 384


def _rms(x, eps=RMS_EPS):
    return x * lax.rsqrt(jnp.mean(x * x, axis=-1, keepdims=True) + eps)


def _sigmoid(z):
    return 1.0 / (1.0 + jnp.exp(-z))


def _rope_split(w):
    half = QK_ROPE // 2
    pad = jnp.zeros(w.shape[:-1] + (QK_NOPE - QK_ROPE,), w.dtype)
    a = jnp.concatenate([w, pad], axis=-1)
    b = jnp.concatenate([w[..., half:], w[..., :half], pad], axis=-1)
    return a, b


ADA_TN = 1024


def _ada_kernel(a_ref, w_ref, b_ref, o_ref):
    a = a_ref[...]
    a = (a * _sigmoid(a)).astype(BF16)
    o_ref[...] = jnp.dot(a, w_ref[...].astype(BF16), preferred_element_type=F32) + b_ref[...]


def _ada(cond, w_ada, b_ada):
    n = N_MOD * D_MODEL
    return pl.pallas_call(
        _ada_kernel,
        grid=(DEPTH, n // ADA_TN),
        in_specs=[
            pl.BlockSpec((MOD_ROWS, D_MODEL), lambda l, j: (0, 0)),
            pl.BlockSpec((None, D_MODEL, ADA_TN), lambda l, j: (l, 0, j)),
            pl.BlockSpec((None, 1, ADA_TN), lambda l, j: (l, 0, j)),
        ],
        out_specs=pl.BlockSpec((None, MOD_ROWS, ADA_TN), lambda l, j: (l, 0, j)),
        out_shape=jax.ShapeDtypeStruct((DEPTH, MOD_ROWS, n), F32),
        name="ada",
    )(cond, w_ada, b_ada.reshape(DEPTH, 1, n))


def _mod_spec(layer, which):
    base = layer * MOD_ROWS * N_MOD + which
    return pl.BlockSpec((None, 1, D_MODEL), lambda i, *_: (base + (i // TILES_PER_BATCH) * N_MOD, 0, 0))


PACK_STEP = P_TN - OFF_CX


def _pack_src_row(j):
    step = jnp.where(j < N_GATE_TILES, OFF_G // PACK_STEP + j * (P_TN // PACK_STEP),
                     jnp.where(j == N_GATE_TILES, 0, OFF_CX // PACK_STEP))
    return step * PACK_STEP


def _pack_tile(j, w_ref, o_ref):
    half = QK_ROPE // 2

    @pl.when(j == N_GATE_TILES)
    def _():
        o_ref[0:OFF_CX, :] = w_ref[0, 0:OFF_CX, :].astype(BF16)
        o_ref[OFF_CX:OFF_KR + QK_NOPE, :] = jnp.zeros((QK_NOPE - QK_ROPE, D_MODEL), BF16)
        o_ref[OFF_KR + QK_NOPE:OFF_KR + QK_NOPE + half, :] = w_ref[0, OFF_KR + half:OFF_CX, :].astype(BF16)
        o_ref[OFF_KR + QK_NOPE + half:OFF_KR + QK_NOPE + QK_ROPE, :] = w_ref[0, OFF_KR:OFF_KR + half, :].astype(BF16)
        o_ref[OFF_KR + QK_NOPE + QK_ROPE:P_TN, :] = jnp.zeros((QK_NOPE - QK_ROPE, D_MODEL), BF16)

    @pl.when(j != N_GATE_TILES)
    def _():
        o_ref[...] = w_ref[0].astype(BF16)


def _rows_specs(ctx_block):
    return [pl.BlockSpec((TM, D_MODEL), lambda i: (jnp.minimum(i, LAT_TILES - 1), 0)),
            pl.BlockSpec((TM, D_MODEL), lambda i: (ctx_block, 0))]


def _prenorm_kernel(xl_ref, xc_ref, gain_ref, shift_ref, scale_ref, h_ref, *rows_ref):
    x = jnp.where(pl.program_id(0) == LAT_TILES, xc_ref[...], xl_ref[...])
    y = _rms(x) * gain_ref[...]
    h_ref[...] = (y * (1.0 + scale_ref[...]) + shift_ref[...]).astype(BF16)
    for r in rows_ref:
        r[...] = x


def _prenorm(x_lat, x_ctx, ctx_block, gain, mod, layer, emit_rows):
    tile = pl.BlockSpec((TM, D_MODEL), lambda i: (i, 0))
    return pl.pallas_call(
        _prenorm_kernel,
        grid=(N_TILES,),
        in_specs=_rows_specs(ctx_block) + [
            pl.BlockSpec((None, 1, D_MODEL), lambda i: (layer, 0, 0)), _mod_spec(layer, 0), _mod_spec(layer, 1)],
        out_specs=[tile] + [tile] * emit_rows,
        out_shape=[jax.ShapeDtypeStruct((ROWS, D_MODEL), BF16)]
        + [jax.ShapeDtypeStruct((ROWS, D_MODEL), F32)] * emit_rows,
        name="prenorm",
    )(x_lat, x_ctx, gain, mod, mod)


IN_TM = ROWS // 8


def _inproj_kernel(h_ref, w_ref, bias_ref, o_ref, wb_ref):
    j = pl.program_id(0)

    @pl.when(pl.program_id(1) == 0)
    def _():
        _pack_tile(j, w_ref, wb_ref)

    is_gate = j < N_GATE_TILES
    h = h_ref[...]
    for c in range(P_TN // P_CHUNK):
        cols = pl.ds(c * P_CHUNK, P_CHUNK)
        z = lax.dot_general(h, wb_ref[cols, :], (((1,), (1,)), ((), ())), preferred_element_type=F32)
        z = z + bias_ref[:, cols]
        o_ref[:, cols] = jnp.where(is_gate, _sigmoid(z), z).astype(BF16)


def _inproj(h, layer, w_in_t, bias_pack):
    return pl.pallas_call(
        _inproj_kernel,
        grid=(P_COLS // P_TN, ROWS // IN_TM),
        in_specs=[
            pl.BlockSpec((IN_TM, D_MODEL), lambda j, i: (i, 0)),
            pl.BlockSpec((pl.Element(1), pl.Element(P_TN), pl.Element(D_MODEL)),
                         lambda j, i: (layer, _pack_src_row(j), 0)),
            pl.BlockSpec((None, 1, P_TN), lambda j, i: (layer, 0, j)),
        ],
        out_specs=pl.BlockSpec((IN_TM, P_TN), lambda j, i: (i, j)),
        out_shape=jax.ShapeDtypeStruct((ROWS, P_COLS), BF16),
        scratch_shapes=[pltpu.VMEM((P_TN, D_MODEL), BF16)],
        compiler_params=pltpu.CompilerParams(dimension_semantics=("arbitrary", "arbitrary")),
        name="inproj",
    )(h, w_in_t, bias_pack)


def _qkv_kernel(cq_ref, ckv_ref, wq_ref, wkv_ref, qa_ref, kva_ref, gq_ref, gk_ref, ca_ref, sb_ref,
                q_ref, k_ref, v_ref):
    ca = ca_ref[...]
    sb = sb_ref[...]
    scale = QK_HEAD ** -0.5 * np.log2(np.e)
    ones_col = jnp.where(lax.broadcasted_iota(jnp.int32, (TQ, V_HEAD), 1) == 0, 1.0, 0.0).astype(BF16)

    cqn = (_rms(cq_ref[...].astype(F32)) * qa_ref[...]).astype(BF16)
    qe = jnp.dot(cqn, wq_ref[...], preferred_element_type=F32)
    gq = gq_ref[...]
    rot_a = gq[1:2] * ca
    rot_b = gq[2:3] * sb
    for h in range(MLA_HEADS):
        nope = qe[:, h * Q_EXT:h * Q_EXT + QK_NOPE]
        ra = qe[:, h * Q_EXT + QK_NOPE:h * Q_EXT + 2 * QK_NOPE]
        rb = qe[:, h * Q_EXT + 2 * QK_NOPE:(h + 1) * Q_EXT]
        ssq = jnp.sum(nope * nope, axis=-1, keepdims=True) + jnp.sum(ra * ra, axis=-1, keepdims=True)
        r = lax.rsqrt(ssq * (1.0 / QK_HEAD) + RMS_EPS) * scale
        q_ref[h, :, 0:QK_NOPE] = (nope * gq[0:1] * r).astype(BF16)
        q_ref[h, :, QK_NOPE:HEAD_PAD] = ((ra * rot_a + rb * rot_b) * r).astype(BF16)

    c2 = ckv_ref[...].astype(F32)
    ckvn = (_rms(c2[:, 0:KV_RANK]) * kva_ref[...]).astype(BF16)
    kv = jnp.dot(ckvn, wkv_ref[...], preferred_element_type=F32)
    kra = c2[:, KV_RANK:KV_RANK + QK_NOPE]
    krb = c2[:, KV_RANK + QK_NOPE:KV_RANK + 2 * QK_NOPE]
    gk = gk_ref[...]
    krot = kra * (gk[1:2] * ca) + krb * (gk[2:3] * sb)
    ssq_r = jnp.sum(kra * kra, axis=-1, keepdims=True)
    for h in range(MLA_HEADS):
        knope = kv[:, h * 2 * QK_NOPE:h * 2 * QK_NOPE + QK_NOPE]
        ssq = jnp.sum(knope * knope, axis=-1, keepdims=True) + ssq_r
        r = lax.rsqrt(ssq * (1.0 / QK_HEAD) + RMS_EPS)
        k_ref[0, h, :, 0:QK_NOPE] = (knope * gk[0:1] * r).astype(BF16)
        k_ref[0, h, :, QK_NOPE:HEAD_PAD] = (krot * r).astype(BF16)
        v_ref[0, h, :, 0:V_HEAD] = kv[:, h * 2 * QK_NOPE + QK_NOPE:(h + 1) * 2 * QK_NOPE].astype(BF16)
        v_ref[0, h, :, V_HEAD:2 * V_HEAD] = ones_col


def _kv_index(t):
    lat = t < BATCH * LAT_QTILES
    b = jnp.where(lat, t // LAT_QTILES, t - BATCH * LAT_QTILES)
    pos = jnp.where(lat, t % LAT_QTILES, LAT_QTILES)
    return b, pos


def _qkv(p, layer, wq_ext, w_ukv, qa, kva, gq, gk, rope_ca, rope_sb):
    def kv_map(t):
        b, pos = _kv_index(t)
        return (b, 0, pos, 0)

    def per_layer(shape):
        return pl.BlockSpec((None,) + shape, lambda t: (layer,) + (0,) * len(shape))

    return pl.pallas_call(
        _qkv_kernel,
        grid=(ROWS // TQ,),
        in_specs=[
            pl.BlockSpec((TQ, Q_RANK), lambda t: (t, PC_CQ // Q_RANK)),
            pl.BlockSpec((TQ, 512), lambda t: (t, PC_CKV // 512)),
            per_layer((Q_RANK, MLA_HEADS * Q_EXT)),
            per_layer((KV_RANK, MLA_HEADS * 2 * QK_NOPE)),
            per_layer((1, Q_RANK)),
            per_layer((1, KV_RANK)),
            per_layer((8, QK_NOPE)),
            per_layer((8, QK_NOPE)),
            pl.BlockSpec((TQ, QK_NOPE), lambda t: (t, 0)),
            pl.BlockSpec((TQ, QK_NOPE), lambda t: (t, 0)),
        ],
        out_specs=[
            pl.BlockSpec((MLA_HEADS, TQ, HEAD_PAD), lambda t: (0, t, 0)),
            pl.BlockSpec((1, MLA_HEADS, TQ, HEAD_PAD), kv_map),
            pl.BlockSpec((1, MLA_HEADS, TQ, 2 * V_HEAD), kv_map),
        ],
        out_shape=[
            jax.ShapeDtypeStruct((MLA_HEADS, ROWS, HEAD_PAD), BF16),
            jax.ShapeDtypeStruct((BATCH, MLA_HEADS, KEYS, HEAD_PAD), BF16),
            jax.ShapeDtypeStruct((BATCH, MLA_HEADS, KEYS, 2 * V_HEAD), BF16),
        ],
        name="qkv",
    )(p, p, wq_ext, w_ukv, qa, kva, gq, gk, rope_ca, rope_sb)


HALO = 16
STREAM_STARTS = (0, SEQ, LAT_ROWS, LAT_ROWS + CTX_LEN)
N_SIDE_IN = 11


def _conv_slice(row0, cx_ref, cb_ref, cc_ref, pcx_ref, pcc_ref, ncx_ref, ncc_ref, w_ref, o_ref):
    n = cx_ref.shape[0]
    u = cc_ref[...].astype(F32) * cx_ref[...].astype(F32)
    row = lax.broadcasted_iota(jnp.int32, (n, 1), 0)
    g = row + row0
    first = functools.reduce(jnp.logical_or, [g == s for s in STREAM_STARTS])
    last = functools.reduce(jnp.logical_or, [g == s - 1 for s in STREAM_STARTS[1:] + (ROWS,)])
    halo_prev = pcc_ref[HALO - 1:HALO, :].astype(F32) * pcx_ref[HALO - 1:HALO, :].astype(F32)
    halo_next = ncc_ref[0:1, :].astype(F32) * ncx_ref[0:1, :].astype(F32)
    up = jnp.where(row == 0, halo_prev, pltpu.roll(u, 1, 0))
    un = jnp.where(row == n - 1, halo_next, pltpu.roll(u, n - 1, 0))
    up = jnp.where(first, 0.0, up)
    un = jnp.where(last, 0.0, un)
    w = w_ref[...]
    y = up * w[0:1] + u * w[1:2] + un * w[2:3]
    o_ref[...] = (cb_ref[...].astype(F32) * y).astype(BF16)


def _chan_dft_slice(pf_ref, w_ref, sel_ref, o_ref):
    half = pf_ref.shape[0] // 2
    uv = jnp.dot(pf_ref[...], w_ref[...], preferred_element_type=F32).astype(BF16)
    eo = jnp.dot(sel_ref[...], uv, preferred_element_type=F32).astype(BF16)
    for part in range(2):
        cols = slice(part * FOURIER_W, (part + 1) * FOURIER_W)
        o_ref[part, :, 0:FOURIER_W] = eo[0:half, cols]
        o_ref[part, :, FOURIER_W:2 * FOURIER_W] = eo[half:2 * half, cols]


def _even_odd_rows(n):
    order = np.concatenate([np.arange(0, n, 2), np.arange(1, n, 2)])
    return jnp.asarray(np.eye(n)[order], F32).astype(BF16)


def _side_specs(step, rows, layer):
    cb0 = PC_CX // CONV_W
    per = rows // HALO
    at = lambda c: (lambda *g: (step(*g), c))
    prev = lambda c: (lambda *g: (jnp.maximum(step(*g) * per - 1, 0), c))
    nxt = lambda c: (lambda *g: (jnp.minimum((step(*g) + 1) * per, ROWS // HALO - 1), c))
    in_specs = [
        pl.BlockSpec((rows, CONV_W), at(cb0)),
        pl.BlockSpec((rows, CONV_W), at(cb0 + 1)),
        pl.BlockSpec((rows, CONV_W), at(cb0 + 2)),
        pl.BlockSpec((HALO, CONV_W), prev(cb0)),
        pl.BlockSpec((HALO, CONV_W), prev(cb0 + 2)),
        pl.BlockSpec((HALO, CONV_W), nxt(cb0)),
        pl.BlockSpec((HALO, CONV_W), nxt(cb0 + 2)),
        pl.BlockSpec((None, 8, CONV_W), lambda *g: (layer, 0, 0)),
        pl.BlockSpec((rows, FOURIER_W), at(PC_F // FOURIER_W)),
        pl.BlockSpec((FOURIER_W, 2 * FOURIER_W), lambda *g: (0, 0)),
        pl.BlockSpec((rows, rows), lambda *g: (0, 0)),
    ]
    out_specs = [pl.BlockSpec((rows, CONV_W), at(0)),
                 pl.BlockSpec((2, rows // 2, 2 * FOURIER_W), lambda *g: (0, step(*g), 0))]
    return in_specs, out_specs


def _attn_kernel(*refs, n_sub, n_cast, side_rows, step):
    q_ref, k_ref, v_ref = refs[:3]
    n_side = N_SIDE_IN if side_rows else 0
    side_in = refs[3:3 + n_side]
    cast_src = refs[3 + n_side:3 + n_side + n_cast]
    o_ref = refs[3 + n_side + n_cast]
    side_out = refs[4 + n_side + n_cast:len(refs) - n_cast]
    cast_dst = refs[len(refs) - n_cast:]
    for src, dst in zip(cast_src, cast_dst):
        dst[...] = src[...].astype(BF16)
    if side_rows:
        row0 = step(pl.program_id(0), pl.program_id(1), pl.program_id(2)) * side_rows
        _conv_slice(row0, *side_in[:8], side_out[0])
        _chan_dft_slice(side_in[8], side_in[9], side_in[10], side_out[1])
    k = k_ref[...]
    v = v_ref[...]
    for t in range(n_sub):
        q = q_ref[t * TQ:(t + 1) * TQ, :]
        s = lax.dot_general(q, k, (((1,), (1,)), ((), ())), preferred_element_type=F32)
        e = jnp.exp2(s - jnp.max(s, axis=-1, keepdims=True)).astype(BF16)
        o = jnp.dot(e, v, preferred_element_type=F32)
        o_ref[t * TQ:(t + 1) * TQ, :] = (o[:, 0:V_HEAD] / o[:, V_HEAD:V_HEAD + 1]).astype(BF16)


ATT_SUB = 8


def _attention_lat(q, k, v, p, conv_w, w_chan, layer, weights, side_total):
    tq = ATT_SUB * TQ
    per_batch = SEQ // tq
    n_steps = BATCH * MLA_HEADS * per_batch
    step = lambda b, h, r: (b * MLA_HEADS + h) * per_batch + r
    side_rows = side_total // n_steps
    side_in, side_out = _side_specs(step, side_rows, layer)
    rows = [w.shape[1] // n_steps for w in weights]
    return pl.pallas_call(
        functools.partial(_attn_kernel, n_sub=ATT_SUB, n_cast=len(weights), side_rows=side_rows, step=step),
        grid=(BATCH, MLA_HEADS, per_batch),
        in_specs=[
            pl.BlockSpec((None, tq, HEAD_PAD), lambda b, h, r: (h, b * per_batch + r, 0)),
            pl.BlockSpec((None, None, KEYS, HEAD_PAD), lambda b, h, r: (b, h, 0, 0)),
            pl.BlockSpec((None, None, KEYS, 2 * V_HEAD), lambda b, h, r: (b, h, 0, 0)),
        ] + side_in
        + [pl.BlockSpec((None, n, w.shape[2]), lambda b, h, r: (layer, step(b, h, r), 0))
           for n, w in zip(rows, weights)],
        out_specs=[pl.BlockSpec((tq, V_HEAD), lambda b, h, r: (b * per_batch + r, h))] + side_out
        + [pl.BlockSpec((n, w.shape[2]), lambda b, h, r: (step(b, h, r), 0)) for n, w in zip(rows, weights)],
        out_shape=[jax.ShapeDtypeStruct((LAT_ROWS, MLA_HEADS * V_HEAD), BF16),
                   jax.ShapeDtypeStruct((side_total, CONV_W), BF16),
                   jax.ShapeDtypeStruct((2, side_total // 2, 2 * FOURIER_W), BF16)]
        + [jax.ShapeDtypeStruct(w.shape[1:], BF16) for w in weights],
        compiler_params=pltpu.CompilerParams(dimension_semantics=("arbitrary", "arbitrary", "arbitrary")),
        name="attention_lat",
    )(q, k, v, *([p] * 7), conv_w, p, w_chan, _even_odd_rows(side_rows), *weights)


def _attention_ctx(q, k, v):
    return pl.pallas_call(
        functools.partial(_attn_kernel, n_sub=1, n_cast=0, side_rows=0, step=None),
        grid=(BATCH, MLA_HEADS),
        in_specs=[
            pl.BlockSpec((None, CTX_LEN, HEAD_PAD), lambda b, h: (h, LAT_ROWS // CTX_LEN + b, 0)),
            pl.BlockSpec((None, None, CTX_LEN, HEAD_PAD), lambda b, h: (b, h, SEQ // CTX_LEN, 0)),
            pl.BlockSpec((None, None, CTX_LEN, 2 * V_HEAD), lambda b, h: (b, h, SEQ // CTX_LEN, 0)),
        ],
        out_specs=pl.BlockSpec((CTX_LEN, V_HEAD), lambda b, h: (b, h)),
        out_shape=jax.ShapeDtypeStruct((CTX_ROWS, MLA_HEADS * V_HEAD), BF16),
        name="attention_ctx",
    )(q, k, v)


def _pos_dft_kernel(de_ref, do_ref, uv0_ref, uv1_ref, o_ref, acc_ref):
    k = pl.program_id(1)

    @pl.when(k == 0)
    def _():
        acc_ref[...] = jnp.zeros_like(acc_ref)

    de = de_ref[...]
    do = do_ref[...]
    for b, uv_ref in enumerate((uv0_ref, uv1_ref)):
        acc_ref[b, 0] += jnp.dot(de, uv_ref[:, 0:FOURIER_W], preferred_element_type=F32)
        acc_ref[b, 1] += jnp.dot(do, uv_ref[:, FOURIER_W:2 * FOURIER_W], preferred_element_type=F32)

    @pl.when(k == pl.num_programs(1) - 1)
    def _():
        for b in range(BATCH):
            e, o = acc_ref[b, 0], acc_ref[b, 1]
            o_ref[b, 0] = (e + o).astype(BF16)
            o_ref[b, 1] = (e - o).astype(BF16)


def _pos_dft(dft, uv_pairs, length, row0, tm, tk):
    half = length // 2
    nk_half = half // tk
    kb0 = row0 // 2 // tk

    def uv_spec(b):
        return pl.BlockSpec((None, tk, 2 * FOURIER_W),
                            lambda m, k: (k // nk_half, kb0 + b * nk_half + k % nk_half, 0))

    out = pl.pallas_call(
        _pos_dft_kernel,
        grid=(half // tm, 2 * nk_half),
        in_specs=[pl.BlockSpec((tm, tk), lambda m, k: (m, k)),
                  pl.BlockSpec((tm, tk), lambda m, k: (m, 2 * nk_half + k)),
                  uv_spec(0), uv_spec(1)],
        out_specs=pl.BlockSpec((BATCH, 2, tm, FOURIER_W), lambda m, k: (0, 0, m, 0)),
        out_shape=jax.ShapeDtypeStruct((BATCH, 2, half, FOURIER_W), BF16),
        scratch_shapes=[pltpu.VMEM((BATCH, 2, tm, FOURIER_W), F32)],
        compiler_params=pltpu.CompilerParams(dimension_semantics=("parallel", "arbitrary")),
        name="pos_dft_%d" % length,
    )(dft, dft, uv_pairs, uv_pairs)
    return out.reshape(BATCH * length, FOURIER_W)


MIX_CHUNK = 512


def _mixout_kernel(fl_ref, fc_ref, al_ref, ac_ref, c_ref, g0_ref, g1_ref, g2_ref, x_ref, gate_ref,
                   wf_ref, wm_ref, wc_ref, wo_ref, o_ref, m_ref):
    is_ctx = pl.program_id(0) == LAT_TILES
    f = jnp.where(is_ctx, fc_ref[...], fl_ref[...])
    a = jnp.where(is_ctx, ac_ref[...], al_ref[...])
    c = c_ref[...]
    for n in range(D_MODEL // MIX_CHUNK):
        cols = pl.ds(n * MIX_CHUNK, MIX_CHUNK)
        y = (g0_ref[:, cols].astype(F32) * jnp.dot(f, wf_ref[:, cols], preferred_element_type=F32)
             + g1_ref[:, cols].astype(F32) * jnp.dot(a, wm_ref[:, cols], preferred_element_type=F32)
             + g2_ref[:, cols].astype(F32) * jnp.dot(c, wc_ref[:, cols], preferred_element_type=F32))
        m_ref[:, cols] = y.astype(BF16)
    m = m_ref[...]
    for n in range(D_MODEL // MIX_CHUNK):
        cols = pl.ds(n * MIX_CHUNK, MIX_CHUNK)
        acc = jnp.dot(m, wo_ref[:, cols], preferred_element_type=F32)
        o_ref[:, cols] = x_ref[:, cols] + gate_ref[:, cols] * acc


def _mixout(f_lat, f_ctx, a_lat, a_ctx, cmix, p, x_all, mod, layer, w_f, w_m, w_c, w_out, n_tiles):
    gb = PC_G // D_MODEL
    lat_or_last = lambda i: (jnp.minimum(i, LAT_TILES - 1), 0)
    resident = lambda rows: pl.BlockSpec((rows, D_MODEL), lambda i: (0, 0), pipeline_mode=pl.Buffered(1))
    return pl.pallas_call(
        _mixout_kernel,
        grid=(n_tiles,),
        in_specs=[
            pl.BlockSpec((TM, FOURIER_W), lat_or_last),
            pl.BlockSpec((TM, FOURIER_W), lambda i: (0, 0)),
            pl.BlockSpec((TM, MLA_HEADS * V_HEAD), lat_or_last),
            pl.BlockSpec((TM, MLA_HEADS * V_HEAD), lambda i: (0, 0)),
            pl.BlockSpec((TM, CONV_W), lambda i: (i, 0)),
            pl.BlockSpec((TM, D_MODEL), lambda i: (i, gb)),
            pl.BlockSpec((TM, D_MODEL), lambda i: (i, gb + 1)),
            pl.BlockSpec((TM, D_MODEL), lambda i: (i, gb + 2)),
            pl.BlockSpec((TM, D_MODEL), lambda i: (i, 0)),
            _mod_spec(layer, 2),
            resident(FOURIER_W),
            resident(MLA_HEADS * V_HEAD),
            resident(CONV_W),
            resident(D_MODEL),
        ],
        out_specs=pl.BlockSpec((TM, D_MODEL), lambda i: (i, 0)),
        out_shape=jax.ShapeDtypeStruct((n_tiles * TM, D_MODEL), F32),
        scratch_shapes=[pltpu.VMEM((TM, D_MODEL), BF16)],
        name="mixout",
    )(f_lat, f_ctx, a_lat, a_ctx, cmix, p, p, p, x_all, mod, w_f, w_m, w_c, w_out)


FF_TN = 512


def _ffn_kernel(x_ref, gain_ref, shift_ref, scale_ref, gate_ref, wg_ref, wu_ref, wd_ref, o_ref,
                h_ref, acc_ref):
    f = pl.program_id(1)

    @pl.when(f == 0)
    def _():
        y = _rms(x_ref[...]) * gain_ref[...]
        h_ref[...] = (y * (1.0 + scale_ref[...]) + shift_ref[...]).astype(BF16)
        acc_ref[...] = jnp.zeros_like(acc_ref)

    h = h_ref[...]
    g = jnp.dot(h, wg_ref[...], preferred_element_type=F32)
    u = jnp.dot(h, wu_ref[...], preferred_element_type=F32)
    a = (g * _sigmoid(g) * u).astype(BF16)
    acc_ref[...] += jnp.dot(a, wd_ref[...], preferred_element_type=F32)

    @pl.when(f == pl.num_programs(1) - 1)
    def _():
        o_ref[...] = x_ref[...] + gate_ref[...] * acc_ref[...]


def _ffn(x_all, gain, mod, layer, w_gate, w_up, w_down, n_tiles):
    return pl.pallas_call(
        _ffn_kernel,
        grid=(n_tiles, D_FF // FF_TN),
        in_specs=[
            pl.BlockSpec((TM, D_MODEL), lambda i, f: (i, 0)),
            pl.BlockSpec((None, 1, D_MODEL), lambda i, f: (layer, 0, 0)),
            _mod_spec(layer, 3),
            _mod_spec(layer, 4),
            _mod_spec(layer, 5),
            pl.BlockSpec((D_MODEL, FF_TN), lambda i, f: (0, f)),
            pl.BlockSpec((D_MODEL, FF_TN), lambda i, f: (0, f)),
            pl.BlockSpec((FF_TN, D_MODEL), lambda i, f: (f, 0)),
        ],
        out_specs=pl.BlockSpec((TM, D_MODEL), lambda i, f: (i, 0)),
        out_shape=jax.ShapeDtypeStruct((n_tiles * TM, D_MODEL), F32),
        scratch_shapes=[pltpu.VMEM((TM, D_MODEL), BF16), pltpu.VMEM((TM, D_MODEL), F32)],
        compiler_params=pltpu.CompilerParams(dimension_semantics=("parallel", "arbitrary")),
        name="ffn",
    )(x_all, gain, mod, mod, mod, w_gate, w_up, w_down)


def _dft_cos_sin(length):
    kn = np.outer(np.arange(length), np.arange(length)) % length
    ang = 2.0 * np.pi * kn / length
    return np.cos(ang), np.sin(ang)


def _chan_dft_matrix():
    c, s = _dft_cos_sin(FOURIER_GROUP_W)
    eye = np.eye(FOURIER_GROUPS)
    return jnp.asarray(np.concatenate([np.kron(eye, c), np.kron(eye, s)], axis=1), F32).astype(BF16)


def _dft_column_order(length):
    ev, od = np.arange(0, length, 2), np.arange(1, length, 2)
    return np.concatenate([ev, length + ev, od, length + od])


def _ctx_dft_matrix():
    c, s = _dft_cos_sin(CTX_LEN)
    norm = (CTX_LEN * FOURIER_GROUP_W) ** -0.5
    full = np.concatenate([c, -s], axis=1) * norm
    return jnp.asarray(full[:CTX_LEN // 2][:, _dft_column_order(CTX_LEN)], F32).astype(BF16)


def _lat_dft_matrix():
    r = 64
    n = np.arange(SEQ)
    hi = 2.0 * np.pi * (np.outer(np.arange(SEQ // 2 // r), n) % r) / r
    lo = 2.0 * np.pi * (np.outer(np.arange(r), n) % SEQ) / SEQ
    norm = (SEQ * FOURIER_GROUP_W) ** -0.5
    order = _dft_column_order(SEQ)
    tab = lambda c_part, s_part: jnp.asarray(np.concatenate([c_part, s_part], axis=1)[:, order], F32)
    hi_c = tab(np.cos(hi) * norm, np.cos(hi) * norm)[:, None, :]
    hi_s = tab(np.sin(hi) * norm, np.sin(hi) * norm)[:, None, :]
    lo_a = tab(np.cos(lo), -np.sin(lo))[None, :, :]
    lo_b = tab(-np.sin(lo), -np.cos(lo))[None, :, :]
    return (hi_c * lo_a + hi_s * lo_b).astype(BF16).reshape(SEQ // 2, 2 * SEQ)


def _rope_tables():
    rows = SEQ // GRID_W
    row = jnp.repeat(jnp.arange(rows), GRID_W)
    col = jnp.tile(jnp.arange(GRID_W), rows)
    inv_freq = ROPE_THETA ** (-jnp.arange(AXIS_PAIRS, dtype=F32) / AXIS_PAIRS)
    ang = jnp.concatenate([row[:, None] * inv_freq, col[:, None] * inv_freq], axis=-1)
    cos, sin = jnp.cos(ang), jnp.sin(ang)
    zeros = jnp.zeros((SEQ, QK_NOPE - QK_ROPE), F32)
    ca = jnp.concatenate([cos, cos, zeros], axis=-1)
    sb = jnp.concatenate([-sin, sin, zeros], axis=-1)
    ca = jnp.concatenate([ca, ca, jnp.ones((CTX_ROWS, QK_NOPE), F32)], axis=0)
    sb = jnp.concatenate([sb, sb, jnp.zeros((CTX_ROWS, QK_NOPE), F32)], axis=0)
    return ca, sb


def _pack_w_uq(w_uq):
    w = w_uq.reshape(DEPTH, Q_RANK, MLA_HEADS, QK_HEAD)
    ra, rb = _rope_split(w[..., QK_NOPE:])
    return jnp.concatenate([w[..., :QK_NOPE], ra, rb], axis=-1).reshape(DEPTH, Q_RANK, MLA_HEADS * Q_EXT).astype(BF16)


def _pack_head_gain(g):
    ga, gb = _rope_split(g[:, QK_NOPE:])
    rows = jnp.stack([g[:, :QK_NOPE], ga, gb], axis=1)
    return jnp.concatenate([rows, jnp.zeros((DEPTH, 5, QK_NOPE), F32)], axis=1)


def kernel(x, c, ctx, c_ctx, w_ada, b_ada, norm_mix, norm_ffn, w_in, b_gate, q_a_norm, kv_a_norm, w_uq, w_ukv,
           q_norm, k_norm, w_f_out, w_mla_out, conv_w, w_conv_out, w_out, w_ffn_gate, w_ffn_up, w_ffn_down):
    cond = jnp.concatenate([c, c_ctx[None, :], jnp.zeros((MOD_ROWS - BATCH - 1, D_MODEL), F32)], axis=0)
    bias_pack = jnp.concatenate([b_gate[:, None, :], jnp.zeros((DEPTH, 1, P_COLS - PC_F), F32)], axis=-1)
    wq_ext = _pack_w_uq(w_uq)
    gq = _pack_head_gain(q_norm)
    gk = _pack_head_gain(k_norm)
    conv_w8 = jnp.concatenate([conv_w, jnp.zeros((DEPTH, 5, CONV_W), F32)], axis=1)
    rope_ca, rope_sb = _rope_tables()
    w_chan = _chan_dft_matrix()
    dft_lat = _lat_dft_matrix()
    dft_ctx = _ctx_dft_matrix()
    w_ukv_b = w_ukv.astype(BF16)
    late_weights = [w_f_out, w_mla_out, w_conv_out, w_out, w_ffn_gate, w_ffn_up, w_ffn_down]
    gain_mix, gain_ffn = norm_mix[:, None, :], norm_ffn[:, None, :]
    qa, kva = q_a_norm[:, None, :], kv_a_norm[:, None, :]

    w_in_t = jnp.swapaxes(w_in, 1, 2)
    mod = _ada(cond, w_ada, b_ada).reshape(DEPTH * MOD_ROWS * N_MOD, 1, D_MODEL)

    x_all = None
    for l in range(DEPTH):
        last = l == DEPTH - 1
        n_tiles = LAT_TILES if last else N_TILES
        if l == 0:
            h, x_all = _prenorm(x.reshape(LAT_ROWS, D_MODEL), ctx.reshape(CTX_ROWS, D_MODEL), 0,
                                gain_mix, mod, l, emit_rows=True)
        else:
            h, = _prenorm(x_all, x_all, LAT_TILES, gain_mix, mod, l, emit_rows=False)
        p = _inproj(h, l, w_in_t, bias_pack)
        q, k, v = _qkv(p, l, wq_ext, w_ukv_b, qa, kva, gq, gk, rope_ca, rope_sb)
        a_lat, cmix, uv, w_f_b, w_m_b, w_c_b, w_out_b, w_g_b, w_u_b, w_d_b = _attention_lat(
            q, k, v, p, conv_w8, w_chan, l, late_weights, n_tiles * TM)
        a_ctx = a_lat if last else _attention_ctx(q, k, v)
        f_lat = _pos_dft(dft_lat, uv, SEQ, 0, 512, SEQ // 2)
        f_ctx = f_lat if last else _pos_dft(dft_ctx, uv, CTX_LEN, LAT_ROWS, CTX_LEN // 2, CTX_LEN // 2)
        x_all = _mixout(f_lat, f_ctx, a_lat, a_ctx, cmix, p, x_all, mod, l, w_f_b, w_m_b, w_c_b, w_out_b, n_tiles)
        x_all = _ffn(x_all, gain_ffn, mod, l, w_g_b, w_u_b, w_d_b, n_tiles)
    return x_all.reshape(BATCH, SEQ, D_MODEL)
```

```python
import functools

import numpy as np
import jax
import jax.numpy as jnp
from jax import lax
from jax.experimental import pallas as pl
from jax.experimental.pallas import tpu as pltpu

F32 = jnp.float32
BF16 = jnp.bfloat16

D_MODEL = 2048
BATCH = 2
SEQ = 4096
DEPTH = 2
GRID_W = 64
CTX_LEN = 256
FOURIER_GROUPS = 4
FOURIER_GROUP_W = 128
FOURIER_W = FOURIER_GROUPS * FOURIER_GROUP_W
MLA_HEADS = 8
Q_RANK = 512
KV_RANK = 256
QK_NOPE = 128
QK_ROPE = 64
QK_HEAD = QK_NOPE + QK_ROPE
V_HEAD = 128
ROPE_THETA = 10000.0
AXIS_PAIRS = QK_ROPE // 4
CONV_W = 512
N_BRANCH = 3
D_FF = ((8 * D_MODEL // 3 + 255) // 256) * 256
N_MOD = 6
RMS_EPS = 1e-6

OFF_F = 0
OFF_CQ = OFF_F + FOURIER_W
OFF_CKV = OFF_CQ + Q_RANK
OFF_KR = OFF_CKV + KV_RANK
OFF_CX = OFF_KR + QK_ROPE
OFF_G = OFF_CX + 3 * CONV_W
N_IN = OFF_G + N_BRANCH * D_MODEL

LAT_ROWS = BATCH * SEQ
CTX_ROWS = BATCH * CTX_LEN
ROWS = LAT_ROWS + CTX_ROWS
KEYS = SEQ + CTX_LEN
TM = 512
N_TILES = ROWS // TM
LAT_TILES = LAT_ROWS // TM
TILES_PER_BATCH = SEQ // TM
TQ = 256
LAT_QTILES = SEQ // TQ
MOD_ROWS = 8

P_TN = 1536
P_CHUNK = 512
P_COLS = 6 * P_TN
PC_G = 0
N_GATE_TILES = N_BRANCH * D_MODEL // P_TN
PC_F = PC_G + N_BRANCH * D_MODEL
PC_CQ = PC_F + FOURIER_W
PC_CKV = PC_CQ + Q_RANK
PC_CX = PC_F + P_TN
HEAD_PAD = 256
Q_EXT = 384


def _rms(x, eps=RMS_EPS):
    return x * lax.rsqrt(jnp.mean(x * x, axis=-1, keepdims=True) + eps)


def _sigmoid(z):
    return 1.0 / (1.0 + jnp.exp(-z))


def _rope_split(w):
    half = QK_ROPE // 2
    pad = jnp.zeros(w.shape[:-1] + (QK_NOPE - QK_ROPE,), w.dtype)
    a = jnp.concatenate([w, pad], axis=-1)
    b = jnp.concatenate([w[..., half:], w[..., :half], pad], axis=-1)
    return a, b


ADA_TN = 1024


def _ada_kernel(a_ref, w_ref, b_ref, o_ref):
    a = a_ref[...]
    a = (a * _sigmoid(a)).astype(BF16)
    o_ref[...] = jnp.dot(a, w_ref[...].astype(BF16), preferred_element_type=F32) + b_ref[...]


def _ada(cond, w_ada, b_ada):
    n = N_MOD * D_MODEL
    return pl.pallas_call(
        _ada_kernel,
        grid=(DEPTH, n // ADA_TN),
        in_specs=[
            pl.BlockSpec((MOD_ROWS, D_MODEL), lambda l, j: (0, 0)),
            pl.BlockSpec((None, D_MODEL, ADA_TN), lambda l, j: (l, 0, j)),
            pl.BlockSpec((None, 1, ADA_TN), lambda l, j: (l, 0, j)),
        ],
        out_specs=pl.BlockSpec((None, MOD_ROWS, ADA_TN), lambda l, j: (l, 0, j)),
        out_shape=jax.ShapeDtypeStruct((DEPTH, MOD_ROWS, n), F32),
        name="ada",
    )(cond, w_ada, b_ada.reshape(DEPTH, 1, n))


def _mod_spec(layer, which):
    base = layer * MOD_ROWS * N_MOD + which
    return pl.BlockSpec((None, 1, D_MODEL), lambda i, *_: (base + (i // TILES_PER_BATCH) * N_MOD, 0, 0))


PACK_STEP = P_TN - OFF_CX


def _pack_src_row(j):
    step = jnp.where(j < N_GATE_TILES, OFF_G // PACK_STEP + j * (P_TN // PACK_STEP),
                     jnp.where(j == N_GATE_TILES, 0, OFF_CX // PACK_STEP))
    return step * PACK_STEP


def _pack_tile(j, w_ref, o_ref):
    half = QK_ROPE // 2

    @pl.when(j == N_GATE_TILES)
    def _():
        o_ref[0:OFF_CX, :] = w_ref[0, 0:OFF_CX, :].astype(BF16)
        o_ref[OFF_CX:OFF_KR + QK_NOPE, :] = jnp.zeros((QK_NOPE - QK_ROPE, D_MODEL), BF16)
        o_ref[OFF_KR + QK_NOPE:OFF_KR + QK_NOPE + half, :] = w_ref[0, OFF_KR + half:OFF_CX, :].astype(BF16)
        o_ref[OFF_KR + QK_NOPE + half:OFF_KR + QK_NOPE + QK_ROPE, :] = w_ref[0, OFF_KR:OFF_KR + half, :].astype(BF16)
        o_ref[OFF_KR + QK_NOPE + QK_ROPE:P_TN, :] = jnp.zeros((QK_NOPE - QK_ROPE, D_MODEL), BF16)

    @pl.when(j != N_GATE_TILES)
    def _():
        o_ref[...] = w_ref[0].astype(BF16)


def _prenorm_kernel(xl_ref, xc_ref, gain_ref, shift_ref, scale_ref, h_ref, rows_ref):
    x = jnp.where(pl.program_id(0) == LAT_TILES, xc_ref[...], xl_ref[...])
    y = _rms(x) * gain_ref[...]
    h_ref[...] = (y * (1.0 + scale_ref[...]) + shift_ref[...]).astype(BF16)
    rows_ref[...] = x


def _prenorm(x_lat, x_ctx, gain, mod):
    tile = pl.BlockSpec((TM, D_MODEL), lambda i: (i, 0))
    return pl.pallas_call(
        _prenorm_kernel,
        grid=(N_TILES,),
        in_specs=[pl.BlockSpec((TM, D_MODEL), lambda i: (jnp.minimum(i, LAT_TILES - 1), 0)),
                  pl.BlockSpec((TM, D_MODEL), lambda i: (0, 0)),
                  pl.BlockSpec((None, 1, D_MODEL), lambda i: (0, 0, 0)), _mod_spec(0, 0), _mod_spec(0, 1)],
        out_specs=[tile, tile],
        out_shape=[jax.ShapeDtypeStruct((ROWS, D_MODEL), BF16), jax.ShapeDtypeStruct((ROWS, D_MODEL), F32)],
        name="prenorm",
    )(x_lat, x_ctx, gain, mod, mod)


IN_TM = ROWS // 8


def _inproj_kernel(h_ref, w_ref, bias_ref, o_ref, wb_ref):
    j = pl.program_id(0)

    @pl.when(pl.program_id(1) == 0)
    def _():
        _pack_tile(j, w_ref, wb_ref)

    is_gate = j < N_GATE_TILES
    h = h_ref[...]
    for c in range(P_TN // P_CHUNK):
        cols = pl.ds(c * P_CHUNK, P_CHUNK)
        z = lax.dot_general(h, wb_ref[cols, :], (((1,), (1,)), ((), ())), preferred_element_type=F32)
        z = z + bias_ref[:, cols]
        o_ref[:, cols] = jnp.where(is_gate, _sigmoid(z), z).astype(BF16)


def _inproj(h, layer, w_in_t, bias_pack):
    return pl.pallas_call(
        _inproj_kernel,
        grid=(P_COLS // P_TN, ROWS // IN_TM),
        in_specs=[
            pl.BlockSpec((IN_TM, D_MODEL), lambda j, i: (i, 0)),
            pl.BlockSpec((pl.Element(1), pl.Element(P_TN), pl.Element(D_MODEL)),
                         lambda j, i: (layer, _pack_src_row(j), 0)),
            pl.BlockSpec((None, 1, P_TN), lambda j, i: (layer, 0, j)),
        ],
        out_specs=pl.BlockSpec((IN_TM, P_TN), lambda j, i: (i, j)),
        out_shape=jax.ShapeDtypeStruct((ROWS, P_COLS), BF16),
        scratch_shapes=[pltpu.VMEM((P_TN, D_MODEL), BF16)],
        compiler_params=pltpu.CompilerParams(dimension_semantics=("arbitrary", "arbitrary")),
        name="inproj",
    )(h, w_in_t, bias_pack)


def _qkv_kernel(cq_ref, ckv_ref, wq_ref, wkv_ref, qa_ref, kva_ref, gq_ref, gk_ref, ca_ref, sb_ref,
                q_ref, k_ref, v_ref):
    ca = ca_ref[...]
    sb = sb_ref[...]
    scale = QK_HEAD ** -0.5 * np.log2(np.e)
    ones_col = jnp.where(lax.broadcasted_iota(jnp.int32, (TQ, V_HEAD), 1) == 0, 1.0, 0.0).astype(BF16)

    cqn = (_rms(cq_ref[...].astype(F32)) * qa_ref[...]).astype(BF16)
    qe = jnp.dot(cqn, wq_ref[...], preferred_element_type=F32)
    gq = gq_ref[...]
    rot_a = gq[1:2] * ca
    rot_b = gq[2:3] * sb
    for h in range(MLA_HEADS):
        nope = qe[:, h * Q_EXT:h * Q_EXT + QK_NOPE]
        ra = qe[:, h * Q_EXT + QK_NOPE:h * Q_EXT + 2 * QK_NOPE]
        rb = qe[:, h * Q_EXT + 2 * QK_NOPE:(h + 1) * Q_EXT]
        ssq = jnp.sum(nope * nope, axis=-1, keepdims=True) + jnp.sum(ra * ra, axis=-1, keepdims=True)
        r = lax.rsqrt(ssq * (1.0 / QK_HEAD) + RMS_EPS) * scale
        q_ref[h, :, 0:QK_NOPE] = (nope * gq[0:1] * r).astype(BF16)
        q_ref[h, :, QK_NOPE:HEAD_PAD] = ((ra * rot_a + rb * rot_b) * r).astype(BF16)

    c2 = ckv_ref[...].astype(F32)
    ckvn = (_rms(c2[:, 0:KV_RANK]) * kva_ref[...]).astype(BF16)
    kv = jnp.dot(ckvn, wkv_ref[...], preferred_element_type=F32)
    kra = c2[:, KV_RANK:KV_RANK + QK_NOPE]
    krb = c2[:, KV_RANK + QK_NOPE:KV_RANK + 2 * QK_NOPE]
    gk = gk_ref[...]
    krot = kra * (gk[1:2] * ca) + krb * (gk[2:3] * sb)
    ssq_r = jnp.sum(kra * kra, axis=-1, keepdims=True)
    for h in range(MLA_HEADS):
        knope = kv[:, h * 2 * QK_NOPE:h * 2 * QK_NOPE + QK_NOPE]
        ssq = jnp.sum(knope * knope, axis=-1, keepdims=True) + ssq_r
        r = lax.rsqrt(ssq * (1.0 / QK_HEAD) + RMS_EPS)
        k_ref[0, h, :, 0:QK_NOPE] = (knope * gk[0:1] * r).astype(BF16)
        k_ref[0, h, :, QK_NOPE:HEAD_PAD] = (krot * r).astype(BF16)
        v_ref[0, h, :, 0:V_HEAD] = kv[:, h * 2 * QK_NOPE + QK_NOPE:(h + 1) * 2 * QK_NOPE].astype(BF16)
        v_ref[0, h, :, V_HEAD:2 * V_HEAD] = ones_col


def _kv_index(t):
    lat = t < BATCH * LAT_QTILES
    b = jnp.where(lat, t // LAT_QTILES, t - BATCH * LAT_QTILES)
    pos = jnp.where(lat, t % LAT_QTILES, LAT_QTILES)
    return b, pos


def _qkv(p, layer, wq_ext, w_ukv, qa, kva, gq, gk, rope_ca, rope_sb):
    def kv_map(t):
        b, pos = _kv_index(t)
        return (b, 0, pos, 0)

    def per_layer(shape):
        return pl.BlockSpec((None,) + shape, lambda t: (layer,) + (0,) * len(shape))

    return pl.pallas_call(
        _qkv_kernel,
        grid=(ROWS // TQ,),
        in_specs=[
            pl.BlockSpec((TQ, Q_RANK), lambda t: (t, PC_CQ // Q_RANK)),
            pl.BlockSpec((TQ, 512), lambda t: (t, PC_CKV // 512)),
            per_layer((Q_RANK, MLA_HEADS * Q_EXT)),
            per_layer((KV_RANK, MLA_HEADS * 2 * QK_NOPE)),
            per_layer((1, Q_RANK)),
            per_layer((1, KV_RANK)),
            per_layer((8, QK_NOPE)),
            per_layer((8, QK_NOPE)),
            pl.BlockSpec((TQ, QK_NOPE), lambda t: (t, 0)),
            pl.BlockSpec((TQ, QK_NOPE), lambda t: (t, 0)),
        ],
        out_specs=[
            pl.BlockSpec((MLA_HEADS, TQ, HEAD_PAD), lambda t: (0, t, 0)),
            pl.BlockSpec((1, MLA_HEADS, TQ, HEAD_PAD), kv_map),
            pl.BlockSpec((1, MLA_HEADS, TQ, 2 * V_HEAD), kv_map),
        ],
        out_shape=[
            jax.ShapeDtypeStruct((MLA_HEADS, ROWS, HEAD_PAD), BF16),
            jax.ShapeDtypeStruct((BATCH, MLA_HEADS, KEYS, HEAD_PAD), BF16),
            jax.ShapeDtypeStruct((BATCH, MLA_HEADS, KEYS, 2 * V_HEAD), BF16),
        ],
        name="qkv",
    )(p, p, wq_ext, w_ukv, qa, kva, gq, gk, rope_ca, rope_sb)


HALO = 16
STREAM_STARTS = (0, SEQ, LAT_ROWS, LAT_ROWS + CTX_LEN)
N_SIDE_IN = 11


def _conv_slice(row0, cx_ref, cb_ref, cc_ref, pcx_ref, pcc_ref, ncx_ref, ncc_ref, w_ref, o_ref):
    n = cx_ref.shape[0]
    u = cc_ref[...].astype(F32) * cx_ref[...].astype(F32)
    row = lax.broadcasted_iota(jnp.int32, (n, 1), 0)
    g = row + row0
    first = functools.reduce(jnp.logical_or, [g == s for s in STREAM_STARTS])
    last = functools.reduce(jnp.logical_or, [g == s - 1 for s in STREAM_STARTS[1:] + (ROWS,)])
    halo_prev = pcc_ref[HALO - 1:HALO, :].astype(F32) * pcx_ref[HALO - 1:HALO, :].astype(F32)
    halo_next = ncc_ref[0:1, :].astype(F32) * ncx_ref[0:1, :].astype(F32)
    up = jnp.where(row == 0, halo_prev, pltpu.roll(u, 1, 0))
    un = jnp.where(row == n - 1, halo_next, pltpu.roll(u, n - 1, 0))
    up = jnp.where(first, 0.0, up)
    un = jnp.where(last, 0.0, un)
    w = w_ref[...]
    y = up * w[0:1] + u * w[1:2] + un * w[2:3]
    o_ref[...] = (cb_ref[...].astype(F32) * y).astype(BF16)


def _chan_dft_slice(pf_ref, w_ref, sel_ref, o_ref):
    half = pf_ref.shape[0] // 2
    uv = jnp.dot(pf_ref[...], w_ref[...], preferred_element_type=F32).astype(BF16)
    eo = jnp.dot(sel_ref[...], uv, preferred_element_type=F32).astype(BF16)
    for part in range(2):
        cols = slice(part * FOURIER_W, (part + 1) * FOURIER_W)
        o_ref[part, :, 0:FOURIER_W] = eo[0:half, cols]
        o_ref[part, :, FOURIER_W:2 * FOURIER_W] = eo[half:2 * half, cols]


def _even_odd_rows(n):
    order = np.concatenate([np.arange(0, n, 2), np.arange(1, n, 2)])
    return jnp.asarray(np.eye(n)[order], F32).astype(BF16)


def _side_specs(step, rows, layer):
    cb0 = PC_CX // CONV_W
    per = rows // HALO
    at = lambda c: (lambda *g: (step(*g), c))
    prev = lambda c: (lambda *g: (jnp.maximum(step(*g) * per - 1, 0), c))
    nxt = lambda c: (lambda *g: (jnp.minimum((step(*g) + 1) * per, ROWS // HALO - 1), c))
    in_specs = [
        pl.BlockSpec((rows, CONV_W), at(cb0)),
        pl.BlockSpec((rows, CONV_W), at(cb0 + 1)),
        pl.BlockSpec((rows, CONV_W), at(cb0 + 2)),
        pl.BlockSpec((HALO, CONV_W), prev(cb0)),
        pl.BlockSpec((HALO, CONV_W), prev(cb0 + 2)),
        pl.BlockSpec((HALO, CONV_W), nxt(cb0)),
        pl.BlockSpec((HALO, CONV_W), nxt(cb0 + 2)),
        pl.BlockSpec((None, 8, CONV_W), lambda *g: (layer, 0, 0)),
        pl.BlockSpec((rows, FOURIER_W), at(PC_F // FOURIER_W)),
        pl.BlockSpec((FOURIER_W, 2 * FOURIER_W), lambda *g: (0, 0)),
        pl.BlockSpec((rows, rows), lambda *g: (0, 0)),
    ]
    out_specs = [pl.BlockSpec((rows, CONV_W), at(0)),
                 pl.BlockSpec((2, rows // 2, 2 * FOURIER_W), lambda *g: (0, step(*g), 0))]
    return in_specs, out_specs


def _attn_kernel(*refs, n_sub, n_cast, side_rows, step):
    q_ref, k_ref, v_ref = refs[:3]
    n_side = N_SIDE_IN if side_rows else 0
    side_in = refs[3:3 + n_side]
    cast_src = refs[3 + n_side:3 + n_side + n_cast]
    o_ref = refs[3 + n_side + n_cast]
    side_out = refs[4 + n_side + n_cast:len(refs) - n_cast]
    cast_dst = refs[len(refs) - n_cast:]
    for src, dst in zip(cast_src, cast_dst):
        dst[...] = src[...].astype(BF16)
    if side_rows:
        row0 = step(pl.program_id(0), pl.program_id(1), pl.program_id(2)) * side_rows
        _conv_slice(row0, *side_in[:8], side_out[0])
        _chan_dft_slice(side_in[8], side_in[9], side_in[10], side_out[1])
    k = k_ref[...]
    v = v_ref[...]
    for t in range(n_sub):
        q = q_ref[t * TQ:(t + 1) * TQ, :]
        s = lax.dot_general(q, k, (((1,), (1,)), ((), ())), preferred_element_type=F32)
        e = jnp.exp2(s - jnp.max(s, axis=-1, keepdims=True)).astype(BF16)
        o = jnp.dot(e, v, preferred_element_type=F32)
        o_ref[t * TQ:(t + 1) * TQ, :] = (o[:, 0:V_HEAD] / o[:, V_HEAD:V_HEAD + 1]).astype(BF16)


ATT_SUB = 8


def _attention_lat(q, k, v, p, conv_w, w_chan, layer, weights, side_total):
    tq = ATT_SUB * TQ
    per_batch = SEQ // tq
    n_steps = BATCH * MLA_HEADS * per_batch
    step = lambda b, h, r: (b * MLA_HEADS + h) * per_batch + r
    side_rows = side_total // n_steps
    side_in, side_out = _side_specs(step, side_rows, layer)
    rows = [w.shape[1] // n_steps for w in weights]
    return pl.pallas_call(
        functools.partial(_attn_kernel, n_sub=ATT_SUB, n_cast=len(weights), side_rows=side_rows, step=step),
        grid=(BATCH, MLA_HEADS, per_batch),
        in_specs=[
            pl.BlockSpec((None, tq, HEAD_PAD), lambda b, h, r: (h, b * per_batch + r, 0)),
            pl.BlockSpec((None, None, KEYS, HEAD_PAD), lambda b, h, r: (b, h, 0, 0)),
            pl.BlockSpec((None, None, KEYS, 2 * V_HEAD), lambda b, h, r: (b, h, 0, 0)),
        ] + side_in
        + [pl.BlockSpec((None, n, w.shape[2]), lambda b, h, r: (layer, step(b, h, r), 0))
           for n, w in zip(rows, weights)],
        out_specs=[pl.BlockSpec((tq, V_HEAD), lambda b, h, r: (b * per_batch + r, h))] + side_out
        + [pl.BlockSpec((n, w.shape[2]), lambda b, h, r: (step(b, h, r), 0)) for n, w in zip(rows, weights)],
        out_shape=[jax.ShapeDtypeStruct((LAT_ROWS, MLA_HEADS * V_HEAD), BF16),
                   jax.ShapeDtypeStruct((side_total, CONV_W), BF16),
                   jax.ShapeDtypeStruct((2, side_total // 2, 2 * FOURIER_W), BF16)]
        + [jax.ShapeDtypeStruct(w.shape[1:], BF16) for w in weights],
        compiler_params=pltpu.CompilerParams(dimension_semantics=("arbitrary", "arbitrary", "arbitrary")),
        name="attention_lat",
    )(q, k, v, *([p] * 7), conv_w, p, w_chan, _even_odd_rows(side_rows), *weights)


def _attention_ctx(q, k, v):
    return pl.pallas_call(
        functools.partial(_attn_kernel, n_sub=1, n_cast=0, side_rows=0, step=None),
        grid=(BATCH, MLA_HEADS),
        in_specs=[
            pl.BlockSpec((None, CTX_LEN, HEAD_PAD), lambda b, h: (h, LAT_ROWS // CTX_LEN + b, 0)),
            pl.BlockSpec((None, None, CTX_LEN, HEAD_PAD), lambda b, h: (b, h, SEQ // CTX_LEN, 0)),
            pl.BlockSpec((None, None, CTX_LEN, 2 * V_HEAD), lambda b, h: (b, h, SEQ // CTX_LEN, 0)),
        ],
        out_specs=pl.BlockSpec((CTX_LEN, V_HEAD), lambda b, h: (b, h)),
        out_shape=jax.ShapeDtypeStruct((CTX_ROWS, MLA_HEADS * V_HEAD), BF16),
        name="attention_ctx",
    )(q, k, v)


def _pos_dft_kernel(de_ref, do_ref, uv0_ref, uv1_ref, o_ref, acc_ref):
    k = pl.program_id(1)

    @pl.when(k == 0)
    def _():
        acc_ref[...] = jnp.zeros_like(acc_ref)

    de = de_ref[...]
    do = do_ref[...]
    for b, uv_ref in enumerate((uv0_ref, uv1_ref)):
        acc_ref[b, 0] += jnp.dot(de, uv_ref[:, 0:FOURIER_W], preferred_element_type=F32)
        acc_ref[b, 1] += jnp.dot(do, uv_ref[:, FOURIER_W:2 * FOURIER_W], preferred_element_type=F32)

    @pl.when(k == pl.num_programs(1) - 1)
    def _():
        for b in range(BATCH):
            e, o = acc_ref[b, 0], acc_ref[b, 1]
            o_ref[b, 0] = (e + o).astype(BF16)
            o_ref[b, 1] = (e - o).astype(BF16)


def _pos_dft(dft, uv_pairs, length, row0, tm, tk):
    half = length // 2
    nk_half = half // tk
    kb0 = row0 // 2 // tk

    def uv_spec(b):
        return pl.BlockSpec((None, tk, 2 * FOURIER_W),
                            lambda m, k: (k // nk_half, kb0 + b * nk_half + k % nk_half, 0))

    out = pl.pallas_call(
        _pos_dft_kernel,
        grid=(half // tm, 2 * nk_half),
        in_specs=[pl.BlockSpec((tm, tk), lambda m, k: (m, k)),
                  pl.BlockSpec((tm, tk), lambda m, k: (m, 2 * nk_half + k)),
                  uv_spec(0), uv_spec(1)],
        out_specs=pl.BlockSpec((BATCH, 2, tm, FOURIER_W), lambda m, k: (0, 0, m, 0)),
        out_shape=jax.ShapeDtypeStruct((BATCH, 2, half, FOURIER_W), BF16),
        scratch_shapes=[pltpu.VMEM((BATCH, 2, tm, FOURIER_W), F32)],
        compiler_params=pltpu.CompilerParams(dimension_semantics=("parallel", "arbitrary")),
        name="pos_dft_%d" % length,
    )(dft, dft, uv_pairs, uv_pairs)
    return out.reshape(BATCH * length, FOURIER_W)


MIX_CHUNK = 512


def _mixout_kernel(fl_ref, fc_ref, al_ref, ac_ref, c_ref, g0_ref, g1_ref, g2_ref, x_ref, gate_ref,
                   wf_ref, wm_ref, wc_ref, wo_ref, o_ref, m_ref):
    is_ctx = pl.program_id(0) == LAT_TILES
    f = jnp.where(is_ctx, fc_ref[...], fl_ref[...])
    a = jnp.where(is_ctx, ac_ref[...], al_ref[...])
    c = c_ref[...]
    for n in range(D_MODEL // MIX_CHUNK):
        cols = pl.ds(n * MIX_CHUNK, MIX_CHUNK)
        y = (g0_ref[:, cols].astype(F32) * jnp.dot(f, wf_ref[:, cols], preferred_element_type=F32)
             + g1_ref[:, cols].astype(F32) * jnp.dot(a, wm_ref[:, cols], preferred_element_type=F32)
             + g2_ref[:, cols].astype(F32) * jnp.dot(c, wc_ref[:, cols], preferred_element_type=F32))
        m_ref[:, cols] = y.astype(BF16)
    m = m_ref[...]
    for n in range(D_MODEL // MIX_CHUNK):
        cols = pl.ds(n * MIX_CHUNK, MIX_CHUNK)
        acc = jnp.dot(m, wo_ref[:, cols], preferred_element_type=F32)
        o_ref[:, cols] = x_ref[:, cols] + gate_ref[:, cols] * acc


def _mixout(f_lat, f_ctx, a_lat, a_ctx, cmix, p, x_all, mod, layer, w_f, w_m, w_c, w_out, n_tiles):
    gb = PC_G // D_MODEL
    lat_or_last = lambda i: (jnp.minimum(i, LAT_TILES - 1), 0)
    resident = lambda rows: pl.BlockSpec((rows, D_MODEL), lambda i: (0, 0), pipeline_mode=pl.Buffered(1))
    return pl.pallas_call(
        _mixout_kernel,
        grid=(n_tiles,),
        in_specs=[
            pl.BlockSpec((TM, FOURIER_W), lat_or_last),
            pl.BlockSpec((TM, FOURIER_W), lambda i: (0, 0)),
            pl.BlockSpec((TM, MLA_HEADS * V_HEAD), lat_or_last),
            pl.BlockSpec((TM, MLA_HEADS * V_HEAD), lambda i: (0, 0)),
            pl.BlockSpec((TM, CONV_W), lambda i: (i, 0)),
            pl.BlockSpec((TM, D_MODEL), lambda i: (i, gb)),
            pl.BlockSpec((TM, D_MODEL), lambda i: (i, gb + 1)),
            pl.BlockSpec((TM, D_MODEL), lambda i: (i, gb + 2)),
            pl.BlockSpec((TM, D_MODEL), lambda i: (i, 0)),
            _mod_spec(layer, 2),
            resident(FOURIER_W),
            resident(MLA_HEADS * V_HEAD),
            resident(CONV_W),
            resident(D_MODEL),
        ],
        out_specs=pl.BlockSpec((TM, D_MODEL), lambda i: (i, 0)),
        out_shape=jax.ShapeDtypeStruct((n_tiles * TM, D_MODEL), F32),
        scratch_shapes=[pltpu.VMEM((TM, D_MODEL), BF16)],
        name="mixout",
    )(f_lat, f_ctx, a_lat, a_ctx, cmix, p, p, p, x_all, mod, w_f, w_m, w_c, w_out)


FF_TN = 512


def _ffn_kernel(x_ref, gain_ref, shift_ref, scale_ref, gate_ref, wg_ref, wu_ref, wd_ref, *refs):
    with_next = len(refs) > 3
    nxt = refs[:3] if with_next else None
    o_ref = refs[3] if with_next else refs[0]
    hn_ref = refs[4] if with_next else None
    h_ref, acc_ref = refs[-2:]
    f = pl.program_id(1)

    @pl.when(f == 0)
    def _():
        y = _rms(x_ref[...]) * gain_ref[...]
        h_ref[...] = (y * (1.0 + scale_ref[...]) + shift_ref[...]).astype(BF16)
        acc_ref[...] = jnp.zeros_like(acc_ref)

    h = h_ref[...]
    g = jnp.dot(h, wg_ref[...], preferred_element_type=F32)
    u = jnp.dot(h, wu_ref[...], preferred_element_type=F32)
    a = (g * _sigmoid(g) * u).astype(BF16)
    acc_ref[...] += jnp.dot(a, wd_ref[...], preferred_element_type=F32)

    @pl.when(f == pl.num_programs(1) - 1)
    def _():
        x_new = x_ref[...] + gate_ref[...] * acc_ref[...]
        o_ref[...] = x_new
        if with_next:
            gain_n, shift_n, scale_n = nxt
            y = _rms(x_new) * gain_n[...]
            hn_ref[...] = (y * (1.0 + scale_n[...]) + shift_n[...]).astype(BF16)


def _ffn(x_all, gain, mod, layer, w_gate, w_up, w_down, n_tiles, next_gain=None):
    tile = pl.BlockSpec((TM, D_MODEL), lambda i, f: (i, 0))
    with_next = next_gain is not None
    next_specs = [pl.BlockSpec((None, 1, D_MODEL), lambda i, f: (layer + 1, 0, 0)),
                  _mod_spec(layer + 1, 0), _mod_spec(layer + 1, 1)] if with_next else []
    return pl.pallas_call(
        _ffn_kernel,
        grid=(n_tiles, D_FF // FF_TN),
        in_specs=[
            tile,
            pl.BlockSpec((None, 1, D_MODEL), lambda i, f: (layer, 0, 0)),
            _mod_spec(layer, 3),
            _mod_spec(layer, 4),
            _mod_spec(layer, 5),
            pl.BlockSpec((D_MODEL, FF_TN), lambda i, f: (0, f)),
            pl.BlockSpec((D_MODEL, FF_TN), lambda i, f: (0, f)),
            pl.BlockSpec((FF_TN, D_MODEL), lambda i, f: (f, 0)),
        ] + next_specs,
        out_specs=[tile] + [tile] * with_next,
        out_shape=[jax.ShapeDtypeStruct((n_tiles * TM, D_MODEL), F32)]
        + [jax.ShapeDtypeStruct((n_tiles * TM, D_MODEL), BF16)] * with_next,
        scratch_shapes=[pltpu.VMEM((TM, D_MODEL), BF16), pltpu.VMEM((TM, D_MODEL), F32)],
        compiler_params=pltpu.CompilerParams(dimension_semantics=("parallel", "arbitrary")),
        name="ffn",
    )(x_all, gain, mod, mod, mod, w_gate, w_up, w_down, *([next_gain, mod, mod] if with_next else []))


def _dft_cos_sin(length):
    kn = np.outer(np.arange(length), np.arange(length)) % length
    ang = 2.0 * np.pi * kn / length
    return np.cos(ang), np.sin(ang)


def _chan_dft_matrix():
    c, s = _dft_cos_sin(FOURIER_GROUP_W)
    eye = np.eye(FOURIER_GROUPS)
    return jnp.asarray(np.concatenate([np.kron(eye, c), np.kron(eye, s)], axis=1), F32).astype(BF16)


def _dft_column_order(length):
    ev, od = np.arange(0, length, 2), np.arange(1, length, 2)
    return np.concatenate([ev, length + ev, od, length + od])


def _ctx_dft_matrix():
    c, s = _dft_cos_sin(CTX_LEN)
    norm = (CTX_LEN * FOURIER_GROUP_W) ** -0.5
    full = np.concatenate([c, -s], axis=1) * norm
    return jnp.asarray(full[:CTX_LEN // 2][:, _dft_column_order(CTX_LEN)], F32).astype(BF16)


def _lat_dft_matrix():
    r = 64
    n = np.arange(SEQ)
    hi = 2.0 * np.pi * (np.outer(np.arange(SEQ // 2 // r), n) % r) / r
    lo = 2.0 * np.pi * (np.outer(np.arange(r), n) % SEQ) / SEQ
    norm = (SEQ * FOURIER_GROUP_W) ** -0.5
    order = _dft_column_order(SEQ)
    tab = lambda c_part, s_part: jnp.asarray(np.concatenate([c_part, s_part], axis=1)[:, order], F32)
    hi_c = tab(np.cos(hi) * norm, np.cos(hi) * norm)[:, None, :]
    hi_s = tab(np.sin(hi) * norm, np.sin(hi) * norm)[:, None, :]
    lo_a = tab(np.cos(lo), -np.sin(lo))[None, :, :]
    lo_b = tab(-np.sin(lo), -np.cos(lo))[None, :, :]
    return (hi_c * lo_a + hi_s * lo_b).astype(BF16).reshape(SEQ // 2, 2 * SEQ)


def _rope_tables():
    rows = SEQ // GRID_W
    row = jnp.repeat(jnp.arange(rows), GRID_W)
    col = jnp.tile(jnp.arange(GRID_W), rows)
    inv_freq = ROPE_THETA ** (-jnp.arange(AXIS_PAIRS, dtype=F32) / AXIS_PAIRS)
    ang = jnp.concatenate([row[:, None] * inv_freq, col[:, None] * inv_freq], axis=-1)
    cos, sin = jnp.cos(ang), jnp.sin(ang)
    zeros = jnp.zeros((SEQ, QK_NOPE - QK_ROPE), F32)
    ca = jnp.concatenate([cos, cos, zeros], axis=-1)
    sb = jnp.concatenate([-sin, sin, zeros], axis=-1)
    ca = jnp.concatenate([ca, ca, jnp.ones((CTX_ROWS, QK_NOPE), F32)], axis=0)
    sb = jnp.concatenate([sb, sb, jnp.zeros((CTX_ROWS, QK_NOPE), F32)], axis=0)
    return ca, sb


def _pack_w_uq(w_uq):
    w = w_uq.reshape(DEPTH, Q_RANK, MLA_HEADS, QK_HEAD)
    ra, rb = _rope_split(w[..., QK_NOPE:])
    return jnp.concatenate([w[..., :QK_NOPE], ra, rb], axis=-1).reshape(DEPTH, Q_RANK, MLA_HEADS * Q_EXT).astype(BF16)


def _pack_head_gain(g):
    ga, gb = _rope_split(g[:, QK_NOPE:])
    rows = jnp.stack([g[:, :QK_NOPE], ga, gb], axis=1)
    return jnp.concatenate([rows, jnp.zeros((DEPTH, 5, QK_NOPE), F32)], axis=1)


def kernel(x, c, ctx, c_ctx, w_ada, b_ada, norm_mix, norm_ffn, w_in, b_gate, q_a_norm, kv_a_norm, w_uq, w_ukv,
           q_norm, k_norm, w_f_out, w_mla_out, conv_w, w_conv_out, w_out, w_ffn_gate, w_ffn_up, w_ffn_down):
    cond = jnp.concatenate([c, c_ctx[None, :], jnp.zeros((MOD_ROWS - BATCH - 1, D_MODEL), F32)], axis=0)
    bias_pack = jnp.concatenate([b_gate[:, None, :], jnp.zeros((DEPTH, 1, P_COLS - PC_F), F32)], axis=-1)
    wq_ext = _pack_w_uq(w_uq)
    gq = _pack_head_gain(q_norm)
    gk = _pack_head_gain(k_norm)
    conv_w8 = jnp.concatenate([conv_w, jnp.zeros((DEPTH, 5, CONV_W), F32)], axis=1)
    rope_ca, rope_sb = _rope_tables()
    w_chan = _chan_dft_matrix()
    dft_lat = _lat_dft_matrix()
    dft_ctx = _ctx_dft_matrix()
    w_ukv_b = w_ukv.astype(BF16)
    late_weights = [w_f_out, w_mla_out, w_conv_out, w_out, w_ffn_gate, w_ffn_up, w_ffn_down]
    gain_mix, gain_ffn = norm_mix[:, None, :], norm_ffn[:, None, :]
    qa, kva = q_a_norm[:, None, :], kv_a_norm[:, None, :]

    w_in_t = jnp.swapaxes(w_in, 1, 2)
    mod = _ada(cond, w_ada, b_ada).reshape(DEPTH * MOD_ROWS * N_MOD, 1, D_MODEL)

    h, x_all = _prenorm(x.reshape(LAT_ROWS, D_MODEL), ctx.reshape(CTX_ROWS, D_MODEL), gain_mix, mod)
    for l in range(DEPTH):
        last = l == DEPTH - 1
        n_tiles = LAT_TILES if last else N_TILES
        p = _inproj(h, l, w_in_t, bias_pack)
        q, k, v = _qkv(p, l, wq_ext, w_ukv_b, qa, kva, gq, gk, rope_ca, rope_sb)
        a_lat, cmix, uv, w_f_b, w_m_b, w_c_b, w_out_b, w_g_b, w_u_b, w_d_b = _attention_lat(
            q, k, v, p, conv_w8, w_chan, l, late_weights, n_tiles * TM)
        a_ctx = a_lat if last else _attention_ctx(q, k, v)
        f_lat = _pos_dft(dft_lat, uv, SEQ, 0, 512, SEQ // 2)
        f_ctx = f_lat if last else _pos_dft(dft_ctx, uv, CTX_LEN, LAT_ROWS, CTX_LEN // 2, CTX_LEN // 2)
        x_all = _mixout(f_lat, f_ctx, a_lat, a_ctx, cmix, p, x_all, mod, l, w_f_b, w_m_b, w_c_b, w_out_b, n_tiles)
        x_all, *h_next = _ffn(x_all, gain_ffn, mod, l, w_g_b, w_u_b, w_d_b, n_tiles,
                              next_gain=None if last else gain_mix)
        h = h_next[0] if h_next else None
    return x_all.reshape(BATCH, SEQ, D_MODEL)
```

```python
import functools

import numpy as np
import jax
import jax.numpy as jnp
from jax import lax
from jax.experimental import pallas as pl
from jax.experimental.pallas import tpu as pltpu

F32 = jnp.float32
BF16 = jnp.bfloat16

D_MODEL = 2048
BATCH = 2
SEQ = 4096
DEPTH = 2
GRID_W = 64
CTX_LEN = 256
FOURIER_GROUPS = 4
FOURIER_GROUP_W = 128
FOURIER_W = FOURIER_GROUPS * FOURIER_GROUP_W
MLA_HEADS = 8
Q_RANK = 512
KV_RANK = 256
QK_NOPE = 128
QK_ROPE = 64
QK_HEAD = QK_NOPE + QK_ROPE
V_HEAD = 128
ROPE_THETA = 10000.0
AXIS_PAIRS = QK_ROPE // 4
CONV_W = 512
N_BRANCH = 3
D_FF = ((8 * D_MODEL // 3 + 255) // 256) * 256
N_MOD = 6
RMS_EPS = 1e-6

OFF_F = 0
OFF_CQ = OFF_F + FOURIER_W
OFF_CKV = OFF_CQ + Q_RANK
OFF_KR = OFF_CKV + KV_RANK
OFF_CX = OFF_KR + QK_ROPE
OFF_G = OFF_CX + 3 * CONV_W
N_IN = OFF_G + N_BRANCH * D_MODEL

LAT_ROWS = BATCH * SEQ
CTX_ROWS = BATCH * CTX_LEN
ROWS = LAT_ROWS + CTX_ROWS
KEYS = SEQ + CTX_LEN
TM = 512
N_TILES = ROWS // TM
LAT_TILES = LAT_ROWS // TM
TILES_PER_BATCH = SEQ // TM
TQ = 256
LAT_QTILES = SEQ // TQ
MOD_ROWS = 8

P_TN = 1536
P_CHUNK = 512
P_COLS = 6 * P_TN
PC_G = 0
N_GATE_TILES = N_BRANCH * D_MODEL // P_TN
PC_F = PC_G + N_BRANCH * D_MODEL
PC_CQ = PC_F + FOURIER_W
PC_CKV = PC_CQ + Q_RANK
PC_CX = PC_F + P_TN
HEAD_PAD = 256
Q_EXT = 384


def _rms(x, eps=RMS_EPS):
    return x * lax.rsqrt(jnp.mean(x * x, axis=-1, keepdims=True) + eps)


def _sigmoid(z):
    return 1.0 / (1.0 + jnp.exp(-z))


def _rope_split(w):
    half = QK_ROPE // 2
    pad = jnp.zeros(w.shape[:-1] + (QK_NOPE - QK_ROPE,), w.dtype)
    a = jnp.concatenate([w, pad], axis=-1)
    b = jnp.concatenate([w[..., half:], w[..., :half], pad], axis=-1)
    return a, b


ADA_TN = 1024


def _ada_kernel(a_ref, w_ref, b_ref, o_ref):
    a = a_ref[...]
    a = (a * _sigmoid(a)).astype(BF16)
    o_ref[...] = jnp.dot(a, w_ref[...].astype(BF16), preferred_element_type=F32) + b_ref[...]


def _ada(cond, w_ada, b_ada):
    n = N_MOD * D_MODEL
    return pl.pallas_call(
        _ada_kernel,
        grid=(DEPTH, n // ADA_TN),
        in_specs=[
            pl.BlockSpec((MOD_ROWS, D_MODEL), lambda l, j: (0, 0)),
            pl.BlockSpec((None, D_MODEL, ADA_TN), lambda l, j: (l, 0, j)),
            pl.BlockSpec((None, 1, ADA_TN), lambda l, j: (l, 0, j)),
        ],
        out_specs=pl.BlockSpec((None, MOD_ROWS, ADA_TN), lambda l, j: (l, 0, j)),
        out_shape=jax.ShapeDtypeStruct((DEPTH, MOD_ROWS, n), F32),
        name="ada",
    )(cond, w_ada, b_ada.reshape(DEPTH, 1, n))


def _mod_spec(layer, which):
    base = layer * MOD_ROWS * N_MOD + which
    return pl.BlockSpec((None, 1, D_MODEL), lambda i, *_: (base + (i // TILES_PER_BATCH) * N_MOD, 0, 0))


PACK_STEP = P_TN - OFF_CX


def _pack_src_row(j):
    step = jnp.where(j < N_GATE_TILES, OFF_G // PACK_STEP + j * (P_TN // PACK_STEP),
                     jnp.where(j == N_GATE_TILES, 0, OFF_CX // PACK_STEP))
    return step * PACK_STEP


def _pack_tile(j, w_ref, o_ref):
    half = QK_ROPE // 2

    @pl.when(j == N_GATE_TILES)
    def _():
        o_ref[0:OFF_CX, :] = w_ref[0, 0:OFF_CX, :].astype(BF16)
        o_ref[OFF_CX:OFF_KR + QK_NOPE, :] = jnp.zeros((QK_NOPE - QK_ROPE, D_MODEL), BF16)
        o_ref[OFF_KR + QK_NOPE:OFF_KR + QK_NOPE + half, :] = w_ref[0, OFF_KR + half:OFF_CX, :].astype(BF16)
        o_ref[OFF_KR + QK_NOPE + half:OFF_KR + QK_NOPE + QK_ROPE, :] = w_ref[0, OFF_KR:OFF_KR + half, :].astype(BF16)
        o_ref[OFF_KR + QK_NOPE + QK_ROPE:P_TN, :] = jnp.zeros((QK_NOPE - QK_ROPE, D_MODEL), BF16)

    @pl.when(j != N_GATE_TILES)
    def _():
        o_ref[...] = w_ref[0].astype(BF16)


def _prenorm_kernel(xl_ref, xc_ref, gain_ref, shift_ref, scale_ref, h_ref, rows_ref):
    x = jnp.where(pl.program_id(0) == LAT_TILES, xc_ref[...], xl_ref[...])
    y = _rms(x) * gain_ref[...]
    h_ref[...] = (y * (1.0 + scale_ref[...]) + shift_ref[...]).astype(BF16)
    rows_ref[...] = x


def _prenorm(x_lat, x_ctx, gain, mod):
    tile = pl.BlockSpec((TM, D_MODEL), lambda i: (i, 0))
    return pl.pallas_call(
        _prenorm_kernel,
        grid=(N_TILES,),
        in_specs=[pl.BlockSpec((TM, D_MODEL), lambda i: (jnp.minimum(i, LAT_TILES - 1), 0)),
                  pl.BlockSpec((TM, D_MODEL), lambda i: (0, 0)),
                  pl.BlockSpec((None, 1, D_MODEL), lambda i: (0, 0, 0)), _mod_spec(0, 0), _mod_spec(0, 1)],
        out_specs=[tile, tile],
        out_shape=[jax.ShapeDtypeStruct((ROWS, D_MODEL), BF16), jax.ShapeDtypeStruct((ROWS, D_MODEL), F32)],
        name="prenorm",
    )(x_lat, x_ctx, gain, mod, mod)


IN_TM = ROWS // 8


def _inproj_kernel(h_ref, w_ref, bias_ref, o_ref, wb_ref):
    j = pl.program_id(0)

    @pl.when(pl.program_id(1) == 0)
    def _():
        _pack_tile(j, w_ref, wb_ref)

    is_gate = j < N_GATE_TILES
    h = h_ref[...]
    for c in range(P_TN // P_CHUNK):
        cols = pl.ds(c * P_CHUNK, P_CHUNK)
        z = lax.dot_general(h, wb_ref[cols, :], (((1,), (1,)), ((), ())), preferred_element_type=F32)
        z = z + bias_ref[:, cols]
        o_ref[:, cols] = jnp.where(is_gate, _sigmoid(z), z).astype(BF16)


def _inproj(h, layer, w_in_t, bias_pack):
    return pl.pallas_call(
        _inproj_kernel,
        grid=(P_COLS // P_TN, ROWS // IN_TM),
        in_specs=[
            pl.BlockSpec((IN_TM, D_MODEL), lambda j, i: (i, 0)),
            pl.BlockSpec((pl.Element(1), pl.Element(P_TN), pl.Element(D_MODEL)),
                         lambda j, i: (layer, _pack_src_row(j), 0)),
            pl.BlockSpec((None, 1, P_TN), lambda j, i: (layer, 0, j)),
        ],
        out_specs=pl.BlockSpec((IN_TM, P_TN), lambda j, i: (i, j)),
        out_shape=jax.ShapeDtypeStruct((ROWS, P_COLS), BF16),
        scratch_shapes=[pltpu.VMEM((P_TN, D_MODEL), BF16)],
        compiler_params=pltpu.CompilerParams(dimension_semantics=("arbitrary", "arbitrary")),
        name="inproj",
    )(h, w_in_t, bias_pack)


def _qkv_kernel(cq_ref, ckv_ref, wq_ref, wkv_ref, qa_ref, kva_ref, gq_ref, gk_ref, ca_ref, sb_ref,
                q_ref, k_ref, v_ref):
    ca = ca_ref[...]
    sb = sb_ref[...]
    scale = QK_HEAD ** -0.5 * np.log2(np.e)
    ones_col = jnp.where(lax.broadcasted_iota(jnp.int32, (TQ, V_HEAD), 1) == 0, 1.0, 0.0).astype(BF16)

    cqn = (_rms(cq_ref[...].astype(F32)) * qa_ref[...]).astype(BF16)
    qe = jnp.dot(cqn, wq_ref[...], preferred_element_type=F32)
    gq = gq_ref[...]
    rot_a = gq[1:2] * ca
    rot_b = gq[2:3] * sb
    for h in range(MLA_HEADS):
        nope = qe[:, h * Q_EXT:h * Q_EXT + QK_NOPE]
        ra = qe[:, h * Q_EXT + QK_NOPE:h * Q_EXT + 2 * QK_NOPE]
        rb = qe[:, h * Q_EXT + 2 * QK_NOPE:(h + 1) * Q_EXT]
        ssq = jnp.sum(nope * nope, axis=-1, keepdims=True) + jnp.sum(ra * ra, axis=-1, keepdims=True)
        r = lax.rsqrt(ssq * (1.0 / QK_HEAD) + RMS_EPS) * scale
        q_ref[h, :, 0:QK_NOPE] = (nope * gq[0:1] * r).astype(BF16)
        q_ref[h, :, QK_NOPE:HEAD_PAD] = ((ra * rot_a + rb * rot_b) * r).astype(BF16)

    c2 = ckv_ref[...].astype(F32)
    ckvn = (_rms(c2[:, 0:KV_RANK]) * kva_ref[...]).astype(BF16)
    kv = jnp.dot(ckvn, wkv_ref[...], preferred_element_type=F32)
    kra = c2[:, KV_RANK:KV_RANK + QK_NOPE]
    krb = c2[:, KV_RANK + QK_NOPE:KV_RANK + 2 * QK_NOPE]
    gk = gk_ref[...]
    krot = kra * (gk[1:2] * ca) + krb * (gk[2:3] * sb)
    ssq_r = jnp.sum(kra * kra, axis=-1, keepdims=True)
    for h in range(MLA_HEADS):
        knope = kv[:, h * 2 * QK_NOPE:h * 2 * QK_NOPE + QK_NOPE]
        ssq = jnp.sum(knope * knope, axis=-1, keepdims=True) + ssq_r
        r = lax.rsqrt(ssq * (1.0 / QK_HEAD) + RMS_EPS)
        k_ref[0, h, :, 0:QK_NOPE] = (knope * gk[0:1] * r).astype(BF16)
        k_ref[0, h, :, QK_NOPE:HEAD_PAD] = (krot * r).astype(BF16)
        v_ref[0, h, :, 0:V_HEAD] = kv[:, h * 2 * QK_NOPE + QK_NOPE:(h + 1) * 2 * QK_NOPE].astype(BF16)
        v_ref[0, h, :, V_HEAD:2 * V_HEAD] = ones_col


def _kv_index(t):
    lat = t < BATCH * LAT_QTILES
    b = jnp.where(lat, t // LAT_QTILES, t - BATCH * LAT_QTILES)
    pos = jnp.where(lat, t % LAT_QTILES, LAT_QTILES)
    return b, pos


def _qkv(p, layer, wq_ext, w_ukv, qa, kva, gq, gk, rope_ca, rope_sb):
    def kv_map(t):
        b, pos = _kv_index(t)
        return (b, 0, pos, 0)

    def per_layer(shape):
        return pl.BlockSpec((None,) + shape, lambda t: (layer,) + (0,) * len(shape))

    return pl.pallas_call(
        _qkv_kernel,
        grid=(ROWS // TQ,),
        in_specs=[
            pl.BlockSpec((TQ, Q_RANK), lambda t: (t, PC_CQ // Q_RANK)),
            pl.BlockSpec((TQ, 512), lambda t: (t, PC_CKV // 512)),
            per_layer((Q_RANK, MLA_HEADS * Q_EXT)),
            per_layer((KV_RANK, MLA_HEADS * 2 * QK_NOPE)),
            per_layer((1, Q_RANK)),
            per_layer((1, KV_RANK)),
            per_layer((8, QK_NOPE)),
            per_layer((8, QK_NOPE)),
            pl.BlockSpec((TQ, QK_NOPE), lambda t: (t, 0)),
            pl.BlockSpec((TQ, QK_NOPE), lambda t: (t, 0)),
        ],
        out_specs=[
            pl.BlockSpec((MLA_HEADS, TQ, HEAD_PAD), lambda t: (0, t, 0)),
            pl.BlockSpec((1, MLA_HEADS, TQ, HEAD_PAD), kv_map),
            pl.BlockSpec((1, MLA_HEADS, TQ, 2 * V_HEAD), kv_map),
        ],
        out_shape=[
            jax.ShapeDtypeStruct((MLA_HEADS, ROWS, HEAD_PAD), BF16),
            jax.ShapeDtypeStruct((BATCH, MLA_HEADS, KEYS, HEAD_PAD), BF16),
            jax.ShapeDtypeStruct((BATCH, MLA_HEADS, KEYS, 2 * V_HEAD), BF16),
        ],
        name="qkv",
    )(p, p, wq_ext, w_ukv, qa, kva, gq, gk, rope_ca, rope_sb)


HALO = 16
STREAM_STARTS = (0, SEQ, LAT_ROWS, LAT_ROWS + CTX_LEN)
N_SIDE_IN = 11


def _conv_slice(row0, cx_ref, cb_ref, cc_ref, pcx_ref, pcc_ref, ncx_ref, ncc_ref, w_ref, o_ref):
    n = cx_ref.shape[0]
    u = cc_ref[...].astype(F32) * cx_ref[...].astype(F32)
    row = lax.broadcasted_iota(jnp.int32, (n, 1), 0)
    g = row + row0
    first = functools.reduce(jnp.logical_or, [g == s for s in STREAM_STARTS])
    last = functools.reduce(jnp.logical_or, [g == s - 1 for s in STREAM_STARTS[1:] + (ROWS,)])
    halo_prev = pcc_ref[HALO - 1:HALO, :].astype(F32) * pcx_ref[HALO - 1:HALO, :].astype(F32)
    halo_next = ncc_ref[0:1, :].astype(F32) * ncx_ref[0:1, :].astype(F32)
    up = jnp.where(row == 0, halo_prev, pltpu.roll(u, 1, 0))
    un = jnp.where(row == n - 1, halo_next, pltpu.roll(u, n - 1, 0))
    up = jnp.where(first, 0.0, up)
    un = jnp.where(last, 0.0, un)
    w = w_ref[...]
    y = up * w[0:1] + u * w[1:2] + un * w[2:3]
    o_ref[...] = (cb_ref[...].astype(F32) * y).astype(BF16)


def _chan_dft_slice(pf_ref, w_ref, sel_ref, o_ref):
    half = pf_ref.shape[0] // 2
    uv = jnp.dot(pf_ref[...], w_ref[...], preferred_element_type=F32).astype(BF16)
    eo = jnp.dot(sel_ref[...], uv, preferred_element_type=F32).astype(BF16)
    for part in range(2):
        cols = slice(part * FOURIER_W, (part + 1) * FOURIER_W)
        o_ref[part, :, 0:FOURIER_W] = eo[0:half, cols]
        o_ref[part, :, FOURIER_W:2 * FOURIER_W] = eo[half:2 * half, cols]


def _even_odd_rows(n):
    order = np.concatenate([np.arange(0, n, 2), np.arange(1, n, 2)])
    return jnp.asarray(np.eye(n)[order], F32).astype(BF16)


def _side_specs(step, rows, layer):
    cb0 = PC_CX // CONV_W
    per = rows // HALO
    at = lambda c: (lambda *g: (step(*g), c))
    prev = lambda c: (lambda *g: (jnp.maximum(step(*g) * per - 1, 0), c))
    nxt = lambda c: (lambda *g: (jnp.minimum((step(*g) + 1) * per, ROWS // HALO - 1), c))
    in_specs = [
        pl.BlockSpec((rows, CONV_W), at(cb0)),
        pl.BlockSpec((rows, CONV_W), at(cb0 + 1)),
        pl.BlockSpec((rows, CONV_W), at(cb0 + 2)),
        pl.BlockSpec((HALO, CONV_W), prev(cb0)),
        pl.BlockSpec((HALO, CONV_W), prev(cb0 + 2)),
        pl.BlockSpec((HALO, CONV_W), nxt(cb0)),
        pl.BlockSpec((HALO, CONV_W), nxt(cb0 + 2)),
        pl.BlockSpec((None, 8, CONV_W), lambda *g: (layer, 0, 0)),
        pl.BlockSpec((rows, FOURIER_W), at(PC_F // FOURIER_W)),
        pl.BlockSpec((FOURIER_W, 2 * FOURIER_W), lambda *g: (0, 0)),
        pl.BlockSpec((rows, rows), lambda *g: (0, 0)),
    ]
    out_specs = [pl.BlockSpec((rows, CONV_W), at(0)),
                 pl.BlockSpec((2, rows // 2, 2 * FOURIER_W), lambda *g: (0, step(*g), 0))]
    return in_specs, out_specs


def _attn_kernel(*refs, n_sub, n_cast, side_rows, step):
    q_ref, k_ref, v_ref = refs[:3]
    n_side = N_SIDE_IN if side_rows else 0
    side_in = refs[3:3 + n_side]
    cast_src = refs[3 + n_side:3 + n_side + n_cast]
    o_ref = refs[3 + n_side + n_cast]
    side_out = refs[4 + n_side + n_cast:len(refs) - n_cast]
    cast_dst = refs[len(refs) - n_cast:]
    for src, dst in zip(cast_src, cast_dst):
        dst[...] = src[...].astype(BF16)
    if side_rows:
        row0 = step(pl.program_id(0), pl.program_id(1), pl.program_id(2)) * side_rows
        _conv_slice(row0, *side_in[:8], side_out[0])
        _chan_dft_slice(side_in[8], side_in[9], side_in[10], side_out[1])
    k = k_ref[...]
    v = v_ref[...]
    for t in range(n_sub):
        q = q_ref[t * TQ:(t + 1) * TQ, :]
        s = lax.dot_general(q, k, (((1,), (1,)), ((), ())), preferred_element_type=F32)
        e = jnp.exp2(s - jnp.max(s, axis=-1, keepdims=True)).astype(BF16)
        o = jnp.dot(e, v, preferred_element_type=F32)
        o_ref[t * TQ:(t + 1) * TQ, :] = (o[:, 0:V_HEAD] / o[:, V_HEAD:V_HEAD + 1]).astype(BF16)


ATT_SUB = 8


def _attention_lat(q, k, v, p, conv_w, w_chan, layer, weights, side_total):
    tq = ATT_SUB * TQ
    per_batch = SEQ // tq
    n_steps = BATCH * MLA_HEADS * per_batch
    step = lambda b, h, r: (b * MLA_HEADS + h) * per_batch + r
    side_rows = side_total // n_steps
    side_in, side_out = _side_specs(step, side_rows, layer)
    rows = [w.shape[1] // n_steps for w in weights]
    return pl.pallas_call(
        functools.partial(_attn_kernel, n_sub=ATT_SUB, n_cast=len(weights), side_rows=side_rows, step=step),
        grid=(BATCH, MLA_HEADS, per_batch),
        in_specs=[
            pl.BlockSpec((None, tq, HEAD_PAD), lambda b, h, r: (h, b * per_batch + r, 0)),
            pl.BlockSpec((None, None, KEYS, HEAD_PAD), lambda b, h, r: (b, h, 0, 0)),
            pl.BlockSpec((None, None, KEYS, 2 * V_HEAD), lambda b, h, r: (b, h, 0, 0)),
        ] + side_in
        + [pl.BlockSpec((None, n, w.shape[2]), lambda b, h, r: (layer, step(b, h, r), 0))
           for n, w in zip(rows, weights)],
        out_specs=[pl.BlockSpec((tq, V_HEAD), lambda b, h, r: (b * per_batch + r, h))] + side_out
        + [pl.BlockSpec((n, w.shape[2]), lambda b, h, r: (step(b, h, r), 0)) for n, w in zip(rows, weights)],
        out_shape=[jax.ShapeDtypeStruct((LAT_ROWS, MLA_HEADS * V_HEAD), BF16),
                   jax.ShapeDtypeStruct((side_total, CONV_W), BF16),
                   jax.ShapeDtypeStruct((2, side_total // 2, 2 * FOURIER_W), BF16)]
        + [jax.ShapeDtypeStruct(w.shape[1:], BF16) for w in weights],
        compiler_params=pltpu.CompilerParams(dimension_semantics=("arbitrary", "arbitrary", "arbitrary")),
        name="attention_lat",
    )(q, k, v, *([p] * 7), conv_w, p, w_chan, _even_odd_rows(side_rows), *weights)


def _attention_ctx(q, k, v):
    return pl.pallas_call(
        functools.partial(_attn_kernel, n_sub=1, n_cast=0, side_rows=0, step=None),
        grid=(BATCH, MLA_HEADS),
        in_specs=[
            pl.BlockSpec((None, CTX_LEN, HEAD_PAD), lambda b, h: (h, LAT_ROWS // CTX_LEN + b, 0)),
            pl.BlockSpec((None, None, CTX_LEN, HEAD_PAD), lambda b, h: (b, h, SEQ // CTX_LEN, 0)),
            pl.BlockSpec((None, None, CTX_LEN, 2 * V_HEAD), lambda b, h: (b, h, SEQ // CTX_LEN, 0)),
        ],
        out_specs=pl.BlockSpec((CTX_LEN, V_HEAD), lambda b, h: (b, h)),
        out_shape=jax.ShapeDtypeStruct((CTX_ROWS, MLA_HEADS * V_HEAD), BF16),
        name="attention_ctx",
    )(q, k, v)


def _pos_dft_kernel(de_ref, do_ref, uv0_ref, uv1_ref, o_ref, acc_ref):
    k = pl.program_id(1)

    @pl.when(k == 0)
    def _():
        acc_ref[...] = jnp.zeros_like(acc_ref)

    de = de_ref[...]
    do = do_ref[...]
    for b, uv_ref in enumerate((uv0_ref, uv1_ref)):
        acc_ref[b, 0] += jnp.dot(de, uv_ref[:, 0:FOURIER_W], preferred_element_type=F32)
        acc_ref[b, 1] += jnp.dot(do, uv_ref[:, FOURIER_W:2 * FOURIER_W], preferred_element_type=F32)

    @pl.when(k == pl.num_programs(1) - 1)
    def _():
        for b in range(BATCH):
            e, o = acc_ref[b, 0], acc_ref[b, 1]
            o_ref[b, 0] = (e + o).astype(BF16)
            o_ref[b, 1] = (e - o).astype(BF16)


def _pos_dft(dft, uv_pairs, length, row0, tm, tk):
    half = length // 2
    nk_half = half // tk
    kb0 = row0 // 2 // tk

    def uv_spec(b):
        return pl.BlockSpec((None, tk, 2 * FOURIER_W),
                            lambda m, k: (k // nk_half, kb0 + b * nk_half + k % nk_half, 0))

    out = pl.pallas_call(
        _pos_dft_kernel,
        grid=(half // tm, 2 * nk_half),
        in_specs=[pl.BlockSpec((tm, tk), lambda m, k: (m, k)),
                  pl.BlockSpec((tm, tk), lambda m, k: (m, 2 * nk_half + k)),
                  uv_spec(0), uv_spec(1)],
        out_specs=pl.BlockSpec((BATCH, 2, tm, FOURIER_W), lambda m, k: (0, 0, m, 0)),
        out_shape=jax.ShapeDtypeStruct((BATCH, 2, half, FOURIER_W), BF16),
        scratch_shapes=[pltpu.VMEM((BATCH, 2, tm, FOURIER_W), F32)],
        compiler_params=pltpu.CompilerParams(dimension_semantics=("parallel", "arbitrary")),
        name="pos_dft_%d" % length,
    )(dft, dft, uv_pairs, uv_pairs)
    return out.reshape(BATCH * length, FOURIER_W)


MIX_CHUNK = 512


def _mixout_kernel(fl_ref, fc_ref, al_ref, ac_ref, c_ref, g0_ref, g1_ref, g2_ref, x_ref, gate_ref,
                   wf_ref, wm_ref, wc_ref, wo_ref, o_ref, m_ref):
    is_ctx = pl.program_id(0) == LAT_TILES
    f = jnp.where(is_ctx, fc_ref[...], fl_ref[...])
    a = jnp.where(is_ctx, ac_ref[...], al_ref[...])
    c = c_ref[...]
    for n in range(D_MODEL // MIX_CHUNK):
        cols = pl.ds(n * MIX_CHUNK, MIX_CHUNK)
        y = (g0_ref[:, cols].astype(F32) * jnp.dot(f, wf_ref[:, cols], preferred_element_type=F32)
             + g1_ref[:, cols].astype(F32) * jnp.dot(a, wm_ref[:, cols], preferred_element_type=F32)
             + g2_ref[:, cols].astype(F32) * jnp.dot(c, wc_ref[:, cols], preferred_element_type=F32))
        m_ref[:, cols] = y.astype(BF16)
    m = m_ref[...]
    for n in range(D_MODEL // MIX_CHUNK):
        cols = pl.ds(n * MIX_CHUNK, MIX_CHUNK)
        acc = jnp.dot(m, wo_ref[:, cols], preferred_element_type=F32)
        o_ref[:, cols] = x_ref[:, cols] + gate_ref[:, cols] * acc


def _mixout(f_lat, f_ctx, a_lat, a_ctx, cmix, p, x_all, mod, layer, w_f, w_m, w_c, w_out, n_tiles):
    gb = PC_G // D_MODEL
    lat_or_last = lambda i: (jnp.minimum(i, LAT_TILES - 1), 0)
    resident = lambda rows: pl.BlockSpec((rows, D_MODEL), lambda i: (0, 0), pipeline_mode=pl.Buffered(1))
    return pl.pallas_call(
        _mixout_kernel,
        grid=(n_tiles,),
        in_specs=[
            pl.BlockSpec((TM, FOURIER_W), lat_or_last),
            pl.BlockSpec((TM, FOURIER_W), lambda i: (0, 0)),
            pl.BlockSpec((TM, MLA_HEADS * V_HEAD), lat_or_last),
            pl.BlockSpec((TM, MLA_HEADS * V_HEAD), lambda i: (0, 0)),
            pl.BlockSpec((TM, CONV_W), lambda i: (i, 0)),
            pl.BlockSpec((TM, D_MODEL), lambda i: (i, gb)),
            pl.BlockSpec((TM, D_MODEL), lambda i: (i, gb + 1)),
            pl.BlockSpec((TM, D_MODEL), lambda i: (i, gb + 2)),
            pl.BlockSpec((TM, D_MODEL), lambda i: (i, 0)),
            _mod_spec(layer, 2),
            resident(FOURIER_W),
            resident(MLA_HEADS * V_HEAD),
            resident(CONV_W),
            resident(D_MODEL),
        ],
        out_specs=pl.BlockSpec((TM, D_MODEL), lambda i: (i, 0)),
        out_shape=jax.ShapeDtypeStruct((n_tiles * TM, D_MODEL), F32),
        scratch_shapes=[pltpu.VMEM((TM, D_MODEL), BF16)],
        name="mixout",
    )(f_lat, f_ctx, a_lat, a_ctx, cmix, p, p, p, x_all, mod, w_f, w_m, w_c, w_out)


FF_TN = 512
FF_TM_LAST = 1024


def _ffn_kernel(x_ref, gain_ref, shift_ref, scale_ref, gate_ref, wg_ref, wu_ref, wd_ref, *refs):
    with_next = len(refs) > 2
    nxt = refs[:3] if with_next else None
    o_ref = refs[3] if with_next else refs[0]
    hn_ref = refs[4] if with_next else None
    h_ref = refs[-1]
    f = pl.program_id(1)

    @pl.when(f == 0)
    def _():
        y = _rms(x_ref[...]) * gain_ref[...]
        h_ref[...] = (y * (1.0 + scale_ref[...]) + shift_ref[...]).astype(BF16)
        o_ref[...] = jnp.zeros_like(o_ref)

    h = h_ref[...]
    g = jnp.dot(h, wg_ref[...], preferred_element_type=F32)
    u = jnp.dot(h, wu_ref[...], preferred_element_type=F32)
    a = (g * _sigmoid(g) * u).astype(BF16)
    o_ref[...] += jnp.dot(a, wd_ref[...], preferred_element_type=F32)

    @pl.when(f == pl.num_programs(1) - 1)
    def _():
        x_new = x_ref[...] + gate_ref[...] * o_ref[...]
        o_ref[...] = x_new
        if with_next:
            gain_n, shift_n, scale_n = nxt
            y = _rms(x_new) * gain_n[...]
            hn_ref[...] = (y * (1.0 + scale_n[...]) + shift_n[...]).astype(BF16)


def _ffn_vmem_bytes(tm, with_next):
    windows = 2 * (2 * tm * D_MODEL * 4 + with_next * tm * D_MODEL * 2 + 3 * D_MODEL * FF_TN * 2)
    temporaries = 3 * tm * FF_TN * 4 + tm * D_MODEL * 4
    return windows + tm * D_MODEL * 2 + temporaries


def _ffn(x_all, gain, mod, layer, w_gate, w_up, w_down, rows, tm, next_gain=None):
    tile = pl.BlockSpec((tm, D_MODEL), lambda i, f: (i, 0))
    with_next = next_gain is not None

    def mod_spec(lyr, which):
        base = lyr * MOD_ROWS * N_MOD + which
        return pl.BlockSpec((None, 1, D_MODEL), lambda i, f: (base + (i // (SEQ // tm)) * N_MOD, 0, 0))

    next_specs = [pl.BlockSpec((None, 1, D_MODEL), lambda i, f: (layer + 1, 0, 0)),
                  mod_spec(layer + 1, 0), mod_spec(layer + 1, 1)] if with_next else []
    return pl.pallas_call(
        _ffn_kernel,
        grid=(rows // tm, D_FF // FF_TN),
        in_specs=[
            tile,
            pl.BlockSpec((None, 1, D_MODEL), lambda i, f: (layer, 0, 0)),
            mod_spec(layer, 3),
            mod_spec(layer, 4),
            mod_spec(layer, 5),
            pl.BlockSpec((D_MODEL, FF_TN), lambda i, f: (0, f)),
            pl.BlockSpec((D_MODEL, FF_TN), lambda i, f: (0, f)),
            pl.BlockSpec((FF_TN, D_MODEL), lambda i, f: (f, 0)),
        ] + next_specs,
        out_specs=[tile] + [tile] * with_next,
        out_shape=[jax.ShapeDtypeStruct((rows, D_MODEL), F32)]
        + [jax.ShapeDtypeStruct((rows, D_MODEL), BF16)] * with_next,
        scratch_shapes=[pltpu.VMEM((tm, D_MODEL), BF16)],
        compiler_params=pltpu.CompilerParams(dimension_semantics=("parallel", "arbitrary"),
                                             vmem_limit_bytes=_ffn_vmem_bytes(tm, with_next)),
        name="ffn",
    )(x_all, gain, mod, mod, mod, w_gate, w_up, w_down, *([next_gain, mod, mod] if with_next else []))


def _dft_cos_sin(length):
    kn = np.outer(np.arange(length), np.arange(length)) % length
    ang = 2.0 * np.pi * kn / length
    return np.cos(ang), np.sin(ang)


def _chan_dft_matrix():
    c, s = _dft_cos_sin(FOURIER_GROUP_W)
    eye = np.eye(FOURIER_GROUPS)
    return jnp.asarray(np.concatenate([np.kron(eye, c), np.kron(eye, s)], axis=1), F32).astype(BF16)


def _dft_column_order(length):
    ev, od = np.arange(0, length, 2), np.arange(1, length, 2)
    return np.concatenate([ev, length + ev, od, length + od])


def _ctx_dft_matrix():
    c, s = _dft_cos_sin(CTX_LEN)
    norm = (CTX_LEN * FOURIER_GROUP_W) ** -0.5
    full = np.concatenate([c, -s], axis=1) * norm
    return jnp.asarray(full[:CTX_LEN // 2][:, _dft_column_order(CTX_LEN)], F32).astype(BF16)


def _lat_dft_matrix():
    r = 64
    n = np.arange(SEQ)
    hi = 2.0 * np.pi * (np.outer(np.arange(SEQ // 2 // r), n) % r) / r
    lo = 2.0 * np.pi * (np.outer(np.arange(r), n) % SEQ) / SEQ
    norm = (SEQ * FOURIER_GROUP_W) ** -0.5
    order = _dft_column_order(SEQ)
    tab = lambda c_part, s_part: jnp.asarray(np.concatenate([c_part, s_part], axis=1)[:, order], F32)
    hi_c = tab(np.cos(hi) * norm, np.cos(hi) * norm)[:, None, :]
    hi_s = tab(np.sin(hi) * norm, np.sin(hi) * norm)[:, None, :]
    lo_a = tab(np.cos(lo), -np.sin(lo))[None, :, :]
    lo_b = tab(-np.sin(lo), -np.cos(lo))[None, :, :]
    return (hi_c * lo_a + hi_s * lo_b).astype(BF16).reshape(SEQ // 2, 2 * SEQ)


def _rope_tables():
    rows = SEQ // GRID_W
    row = jnp.repeat(jnp.arange(rows), GRID_W)
    col = jnp.tile(jnp.arange(GRID_W), rows)
    inv_freq = ROPE_THETA ** (-jnp.arange(AXIS_PAIRS, dtype=F32) / AXIS_PAIRS)
    ang = jnp.concatenate([row[:, None] * inv_freq, col[:, None] * inv_freq], axis=-1)
    cos, sin = jnp.cos(ang), jnp.sin(ang)
    zeros = jnp.zeros((SEQ, QK_NOPE - QK_ROPE), F32)
    ca = jnp.concatenate([cos, cos, zeros], axis=-1)
    sb = jnp.concatenate([-sin, sin, zeros], axis=-1)
    ca = jnp.concatenate([ca, ca, jnp.ones((CTX_ROWS, QK_NOPE), F32)], axis=0)
    sb = jnp.concatenate([sb, sb, jnp.zeros((CTX_ROWS, QK_NOPE), F32)], axis=0)
    return ca, sb


def _pack_w_uq(w_uq):
    w = w_uq.reshape(DEPTH, Q_RANK, MLA_HEADS, QK_HEAD)
    ra, rb = _rope_split(w[..., QK_NOPE:])
    return jnp.concatenate([w[..., :QK_NOPE], ra, rb], axis=-1).reshape(DEPTH, Q_RANK, MLA_HEADS * Q_EXT).astype(BF16)


def _pack_head_gain(g):
    ga, gb = _rope_split(g[:, QK_NOPE:])
    rows = jnp.stack([g[:, :QK_NOPE], ga, gb], axis=1)
    return jnp.concatenate([rows, jnp.zeros((DEPTH, 5, QK_NOPE), F32)], axis=1)


def kernel(x, c, ctx, c_ctx, w_ada, b_ada, norm_mix, norm_ffn, w_in, b_gate, q_a_norm, kv_a_norm, w_uq, w_ukv,
           q_norm, k_norm, w_f_out, w_mla_out, conv_w, w_conv_out, w_out, w_ffn_gate, w_ffn_up, w_ffn_down):
    cond = jnp.concatenate([c, c_ctx[None, :], jnp.zeros((MOD_ROWS - BATCH - 1, D_MODEL), F32)], axis=0)
    bias_pack = jnp.concatenate([b_gate[:, None, :], jnp.zeros((DEPTH, 1, P_COLS - PC_F), F32)], axis=-1)
    wq_ext = _pack_w_uq(w_uq)
    gq = _pack_head_gain(q_norm)
    gk = _pack_head_gain(k_norm)
    conv_w8 = jnp.concatenate([conv_w, jnp.zeros((DEPTH, 5, CONV_W), F32)], axis=1)
    rope_ca, rope_sb = _rope_tables()
    w_chan = _chan_dft_matrix()
    dft_lat = _lat_dft_matrix()
    dft_ctx = _ctx_dft_matrix()
    w_ukv_b = w_ukv.astype(BF16)
    late_weights = [w_f_out, w_mla_out, w_conv_out, w_out, w_ffn_gate, w_ffn_up, w_ffn_down]
    gain_mix, gain_ffn = norm_mix[:, None, :], norm_ffn[:, None, :]
    qa, kva = q_a_norm[:, None, :], kv_a_norm[:, None, :]

    w_in_t = jnp.swapaxes(w_in, 1, 2)
    mod = _ada(cond, w_ada, b_ada).reshape(DEPTH * MOD_ROWS * N_MOD, 1, D_MODEL)

    h, x_all = _prenorm(x.reshape(LAT_ROWS, D_MODEL), ctx.reshape(CTX_ROWS, D_MODEL), gain_mix, mod)
    for l in range(DEPTH):
        last = l == DEPTH - 1
        n_tiles = LAT_TILES if last else N_TILES
        p = _inproj(h, l, w_in_t, bias_pack)
        q, k, v = _qkv(p, l, wq_ext, w_ukv_b, qa, kva, gq, gk, rope_ca, rope_sb)
        a_lat, cmix, uv, w_f_b, w_m_b, w_c_b, w_out_b, w_g_b, w_u_b, w_d_b = _attention_lat(
            q, k, v, p, conv_w8, w_chan, l, late_weights, n_tiles * TM)
        a_ctx = a_lat if last else _attention_ctx(q, k, v)
        f_lat = _pos_dft(dft_lat, uv, SEQ, 0, 512, SEQ // 2)
        f_ctx = f_lat if last else _pos_dft(dft_ctx, uv, CTX_LEN, LAT_ROWS, CTX_LEN // 2, CTX_LEN // 2)
        x_all = _mixout(f_lat, f_ctx, a_lat, a_ctx, cmix, p, x_all, mod, l, w_f_b, w_m_b, w_c_b, w_out_b, n_tiles)
        x_all, *h_next = _ffn(x_all, gain_ffn, mod, l, w_g_b, w_u_b, w_d_b, n_tiles * TM,
                              FF_TM_LAST if last else TM, next_gain=None if last else gain_mix)
        h = h_next[0] if h_next else None
    return x_all.reshape(BATCH, SEQ, D_MODEL)
```

```python
import functools

import numpy as np
import jax
import jax.numpy as jnp
from jax import lax
from jax.experimental import pallas as pl
from jax.experimental.pallas import tpu as pltpu

F32 = jnp.float32
BF16 = jnp.bfloat16

D_MODEL = 2048
BATCH = 2
SEQ = 4096
DEPTH = 2
GRID_W = 64
CTX_LEN = 256
FOURIER_GROUPS = 4
FOURIER_GROUP_W = 128
FOURIER_W = FOURIER_GROUPS * FOURIER_GROUP_W
MLA_HEADS = 8
Q_RANK = 512
KV_RANK = 256
QK_NOPE = 128
QK_ROPE = 64
QK_HEAD = QK_NOPE + QK_ROPE
V_HEAD = 128
ROPE_THETA = 10000.0
AXIS_PAIRS = QK_ROPE // 4
CONV_W = 512
N_BRANCH = 3
D_FF = ((8 * D_MODEL // 3 + 255) // 256) * 256
N_MOD = 6
RMS_EPS = 1e-6

OFF_F = 0
OFF_CQ = OFF_F + FOURIER_W
OFF_CKV = OFF_CQ + Q_RANK
OFF_KR = OFF_CKV + KV_RANK
OFF_CX = OFF_KR + QK_ROPE
OFF_G = OFF_CX + 3 * CONV_W
N_IN = OFF_G + N_BRANCH * D_MODEL

LAT_ROWS = BATCH * SEQ
CTX_ROWS = BATCH * CTX_LEN
ROWS = LAT_ROWS + CTX_ROWS
KEYS = SEQ + CTX_LEN
TM = 512
N_TILES = ROWS // TM
LAT_TILES = LAT_ROWS // TM
TILES_PER_BATCH = SEQ // TM
TQ = 256
LAT_QTILES = SEQ // TQ
MOD_ROWS = 8

P_TN = 1536
P_CHUNK = 512
P_COLS = 6 * P_TN
PC_G = 0
N_GATE_TILES = N_BRANCH * D_MODEL // P_TN
PC_F = PC_G + N_BRANCH * D_MODEL
PC_CQ = PC_F + FOURIER_W
PC_CKV = PC_CQ + Q_RANK
PC_CX = PC_F + P_TN
HEAD_PAD = 256
Q_EXT = 384


def _rms(x, eps=RMS_EPS):
    return x * lax.rsqrt(jnp.mean(x * x, axis=-1, keepdims=True) + eps)


def _sigmoid(z):
    return 1.0 / (1.0 + jnp.exp(-z))


def _rope_split(w):
    half = QK_ROPE // 2
    pad = jnp.zeros(w.shape[:-1] + (QK_NOPE - QK_ROPE,), w.dtype)
    a = jnp.concatenate([w, pad], axis=-1)
    b = jnp.concatenate([w[..., half:], w[..., :half], pad], axis=-1)
    return a, b


ADA_TN = 1024


def _ada_kernel(a_ref, w_ref, b_ref, o_ref):
    a = a_ref[...]
    a = (a * _sigmoid(a)).astype(BF16)
    o_ref[...] = jnp.dot(a, w_ref[...].astype(BF16), preferred_element_type=F32) + b_ref[...]


def _ada(cond, w_ada, b_ada):
    n = N_MOD * D_MODEL
    return pl.pallas_call(
        _ada_kernel,
        grid=(DEPTH, n // ADA_TN),
        in_specs=[
            pl.BlockSpec((MOD_ROWS, D_MODEL), lambda l, j: (0, 0)),
            pl.BlockSpec((None, D_MODEL, ADA_TN), lambda l, j: (l, 0, j)),
            pl.BlockSpec((None, 1, ADA_TN), lambda l, j: (l, 0, j)),
        ],
        out_specs=pl.BlockSpec((None, MOD_ROWS, ADA_TN), lambda l, j: (l, 0, j)),
        out_shape=jax.ShapeDtypeStruct((DEPTH, MOD_ROWS, n), F32),
        name="ada",
    )(cond, w_ada, b_ada.reshape(DEPTH, 1, n))


def _mod_spec(layer, which):
    base = layer * MOD_ROWS * N_MOD + which
    return pl.BlockSpec((None, 1, D_MODEL), lambda i, *_: (base + (i // TILES_PER_BATCH) * N_MOD, 0, 0))


PACK_STEP = P_TN - OFF_CX


def _pack_src_row(j):
    step = jnp.where(j < N_GATE_TILES, OFF_G // PACK_STEP + j * (P_TN // PACK_STEP),
                     jnp.where(j == N_GATE_TILES, 0, OFF_CX // PACK_STEP))
    return step * PACK_STEP


def _pack_tile(j, w_ref, o_ref):
    half = QK_ROPE // 2

    @pl.when(j == N_GATE_TILES)
    def _():
        o_ref[0:OFF_CX, :] = w_ref[0, 0:OFF_CX, :].astype(BF16)
        o_ref[OFF_CX:OFF_KR + QK_NOPE, :] = jnp.zeros((QK_NOPE - QK_ROPE, D_MODEL), BF16)
        o_ref[OFF_KR + QK_NOPE:OFF_KR + QK_NOPE + half, :] = w_ref[0, OFF_KR + half:OFF_CX, :].astype(BF16)
        o_ref[OFF_KR + QK_NOPE + half:OFF_KR + QK_NOPE + QK_ROPE, :] = w_ref[0, OFF_KR:OFF_KR + half, :].astype(BF16)
        o_ref[OFF_KR + QK_NOPE + QK_ROPE:P_TN, :] = jnp.zeros((QK_NOPE - QK_ROPE, D_MODEL), BF16)

    @pl.when(j != N_GATE_TILES)
    def _():
        o_ref[...] = w_ref[0].astype(BF16)


def _prenorm_kernel(xl_ref, xc_ref, gain_ref, shift_ref, scale_ref, h_ref, rows_ref):
    x = jnp.where(pl.program_id(0) == LAT_TILES, xc_ref[...], xl_ref[...])
    y = _rms(x) * gain_ref[...]
    h_ref[...] = (y * (1.0 + scale_ref[...]) + shift_ref[...]).astype(BF16)
    rows_ref[...] = x


def _prenorm(x_lat, x_ctx, gain, mod):
    tile = pl.BlockSpec((TM, D_MODEL), lambda i: (i, 0))
    return pl.pallas_call(
        _prenorm_kernel,
        grid=(N_TILES,),
        in_specs=[pl.BlockSpec((TM, D_MODEL), lambda i: (jnp.minimum(i, LAT_TILES - 1), 0)),
                  pl.BlockSpec((TM, D_MODEL), lambda i: (0, 0)),
                  pl.BlockSpec((None, 1, D_MODEL), lambda i: (0, 0, 0)), _mod_spec(0, 0), _mod_spec(0, 1)],
        out_specs=[tile, tile],
        out_shape=[jax.ShapeDtypeStruct((ROWS, D_MODEL), BF16), jax.ShapeDtypeStruct((ROWS, D_MODEL), F32)],
        name="prenorm",
    )(x_lat, x_ctx, gain, mod, mod)


IN_TM = ROWS // 8


def _inproj_kernel(h_ref, w_ref, bias_ref, o_ref, wb_ref):
    j = pl.program_id(0)

    @pl.when(pl.program_id(1) == 0)
    def _():
        _pack_tile(j, w_ref, wb_ref)

    is_gate = j < N_GATE_TILES
    h = h_ref[...]
    for c in range(P_TN // P_CHUNK):
        cols = pl.ds(c * P_CHUNK, P_CHUNK)
        z = lax.dot_general(h, wb_ref[cols, :], (((1,), (1,)), ((), ())), preferred_element_type=F32)
        z = z + bias_ref[:, cols]
        o_ref[:, cols] = jnp.where(is_gate, _sigmoid(z), z).astype(BF16)


def _inproj(h, layer, w_in_t, bias_pack):
    return pl.pallas_call(
        _inproj_kernel,
        grid=(P_COLS // P_TN, ROWS // IN_TM),
        in_specs=[
            pl.BlockSpec((IN_TM, D_MODEL), lambda j, i: (i, 0)),
            pl.BlockSpec((pl.Element(1), pl.Element(P_TN), pl.Element(D_MODEL)),
                         lambda j, i: (layer, _pack_src_row(j), 0)),
            pl.BlockSpec((None, 1, P_TN), lambda j, i: (layer, 0, j)),
        ],
        out_specs=pl.BlockSpec((IN_TM, P_TN), lambda j, i: (i, j)),
        out_shape=jax.ShapeDtypeStruct((ROWS, P_COLS), BF16),
        scratch_shapes=[pltpu.VMEM((P_TN, D_MODEL), BF16)],
        compiler_params=pltpu.CompilerParams(dimension_semantics=("arbitrary", "arbitrary")),
        name="inproj",
    )(h, w_in_t, bias_pack)


def _qkv_kernel(cq_ref, ckv_ref, wq_ref, wkv_ref, qa_ref, kva_ref, gq_ref, gk_ref, ca_ref, sb_ref,
                q_ref, k_ref, v_ref):
    ca = ca_ref[...]
    sb = sb_ref[...]
    scale = QK_HEAD ** -0.5 * np.log2(np.e)
    ones_col = jnp.where(lax.broadcasted_iota(jnp.int32, (TQ, V_HEAD), 1) == 0, 1.0, 0.0).astype(BF16)

    cqn = (_rms(cq_ref[...].astype(F32)) * qa_ref[...]).astype(BF16)
    qe = jnp.dot(cqn, wq_ref[...], preferred_element_type=F32)
    gq = gq_ref[...]
    rot_a = gq[1:2] * ca
    rot_b = gq[2:3] * sb
    for h in range(MLA_HEADS):
        nope = qe[:, h * Q_EXT:h * Q_EXT + QK_NOPE]
        ra = qe[:, h * Q_EXT + QK_NOPE:h * Q_EXT + 2 * QK_NOPE]
        rb = qe[:, h * Q_EXT + 2 * QK_NOPE:(h + 1) * Q_EXT]
        ssq = jnp.sum(nope * nope, axis=-1, keepdims=True) + jnp.sum(ra * ra, axis=-1, keepdims=True)
        r = lax.rsqrt(ssq * (1.0 / QK_HEAD) + RMS_EPS) * scale
        q_ref[h, :, 0:QK_NOPE] = (nope * gq[0:1] * r).astype(BF16)
        q_ref[h, :, QK_NOPE:HEAD_PAD] = ((ra * rot_a + rb * rot_b) * r).astype(BF16)

    c2 = ckv_ref[...].astype(F32)
    ckvn = (_rms(c2[:, 0:KV_RANK]) * kva_ref[...]).astype(BF16)
    kv = jnp.dot(ckvn, wkv_ref[...], preferred_element_type=F32)
    kra = c2[:, KV_RANK:KV_RANK + QK_NOPE]
    krb = c2[:, KV_RANK + QK_NOPE:KV_RANK + 2 * QK_NOPE]
    gk = gk_ref[...]
    krot = kra * (gk[1:2] * ca) + krb * (gk[2:3] * sb)
    ssq_r = jnp.sum(kra * kra, axis=-1, keepdims=True)
    for h in range(MLA_HEADS):
        knope = kv[:, h * 2 * QK_NOPE:h * 2 * QK_NOPE + QK_NOPE]
        ssq = jnp.sum(knope * knope, axis=-1, keepdims=True) + ssq_r
        r = lax.rsqrt(ssq * (1.0 / QK_HEAD) + RMS_EPS)
        k_ref[0, h, :, 0:QK_NOPE] = (knope * gk[0:1] * r).astype(BF16)
        k_ref[0, h, :, QK_NOPE:HEAD_PAD] = (krot * r).astype(BF16)
        v_ref[0, h, :, 0:V_HEAD] = kv[:, h * 2 * QK_NOPE + QK_NOPE:(h + 1) * 2 * QK_NOPE].astype(BF16)
        v_ref[0, h, :, V_HEAD:2 * V_HEAD] = ones_col


def _kv_index(t):
    lat = t < BATCH * LAT_QTILES
    b = jnp.where(lat, t // LAT_QTILES, t - BATCH * LAT_QTILES)
    pos = jnp.where(lat, t % LAT_QTILES, LAT_QTILES)
    return b, pos


def _qkv(p, layer, wq_ext, w_ukv, qa, kva, gq, gk, rope_ca, rope_sb):
    def kv_map(t):
        b, pos = _kv_index(t)
        return (b, 0, pos, 0)

    def per_layer(shape):
        return pl.BlockSpec((None,) + shape, lambda t: (layer,) + (0,) * len(shape))

    return pl.pallas_call(
        _qkv_kernel,
        grid=(ROWS // TQ,),
        in_specs=[
            pl.BlockSpec((TQ, Q_RANK), lambda t: (t, PC_CQ // Q_RANK)),
            pl.BlockSpec((TQ, 512), lambda t: (t, PC_CKV // 512)),
            per_layer((Q_RANK, MLA_HEADS * Q_EXT)),
            per_layer((KV_RANK, MLA_HEADS * 2 * QK_NOPE)),
            per_layer((1, Q_RANK)),
            per_layer((1, KV_RANK)),
            per_layer((8, QK_NOPE)),
            per_layer((8, QK_NOPE)),
            pl.BlockSpec((TQ, QK_NOPE), lambda t: (t, 0)),
            pl.BlockSpec((TQ, QK_NOPE), lambda t: (t, 0)),
        ],
        out_specs=[
            pl.BlockSpec((MLA_HEADS, TQ, HEAD_PAD), lambda t: (0, t, 0)),
            pl.BlockSpec((1, MLA_HEADS, TQ, HEAD_PAD), kv_map),
            pl.BlockSpec((1, MLA_HEADS, TQ, 2 * V_HEAD), kv_map),
        ],
        out_shape=[
            jax.ShapeDtypeStruct((MLA_HEADS, ROWS, HEAD_PAD), BF16),
            jax.ShapeDtypeStruct((BATCH, MLA_HEADS, KEYS, HEAD_PAD), BF16),
            jax.ShapeDtypeStruct((BATCH, MLA_HEADS, KEYS, 2 * V_HEAD), BF16),
        ],
        name="qkv",
    )(p, p, wq_ext, w_ukv, qa, kva, gq, gk, rope_ca, rope_sb)


HALO = 16
STREAM_STARTS = (0, SEQ, LAT_ROWS, LAT_ROWS + CTX_LEN)
N_CONV_IN = 8
N_CHAN_IN = 3


def _conv_slice(row0, cx_ref, cb_ref, cc_ref, pcx_ref, pcc_ref, ncx_ref, ncc_ref, w_ref, o_ref):
    n = cx_ref.shape[0]
    u = cc_ref[...].astype(F32) * cx_ref[...].astype(F32)
    row = lax.broadcasted_iota(jnp.int32, (n, 1), 0)
    g = row + row0
    first = functools.reduce(jnp.logical_or, [g == s for s in STREAM_STARTS])
    last = functools.reduce(jnp.logical_or, [g == s - 1 for s in STREAM_STARTS[1:] + (ROWS,)])
    halo_prev = pcc_ref[HALO - 1:HALO, :].astype(F32) * pcx_ref[HALO - 1:HALO, :].astype(F32)
    halo_next = ncc_ref[0:1, :].astype(F32) * ncx_ref[0:1, :].astype(F32)
    up = jnp.where(row == 0, halo_prev, pltpu.roll(u, 1, 0))
    un = jnp.where(row == n - 1, halo_next, pltpu.roll(u, n - 1, 0))
    up = jnp.where(first, 0.0, up)
    un = jnp.where(last, 0.0, un)
    w = w_ref[...]
    y = up * w[0:1] + u * w[1:2] + un * w[2:3]
    o_ref[...] = (cb_ref[...].astype(F32) * y).astype(BF16)


def _chan_dft_slice(pf_ref, w_ref, sel_ref, o_ref):
    half = pf_ref.shape[0] // 2
    uv = jnp.dot(pf_ref[...], w_ref[...], preferred_element_type=F32).astype(BF16)
    eo = jnp.dot(sel_ref[...], uv, preferred_element_type=F32).astype(BF16)
    for part in range(2):
        cols = slice(part * FOURIER_W, (part + 1) * FOURIER_W)
        o_ref[part, :, 0:FOURIER_W] = eo[0:half, cols]
        o_ref[part, :, FOURIER_W:2 * FOURIER_W] = eo[half:2 * half, cols]


def _even_odd_rows(n):
    order = np.concatenate([np.arange(0, n, 2), np.arange(1, n, 2)])
    return jnp.asarray(np.eye(n)[order], F32).astype(BF16)


def _conv_specs(step, rows, layer):
    cb0 = PC_CX // CONV_W
    per = rows // HALO
    at = lambda c: (lambda *g: (step(*g), c))
    prev = lambda c: (lambda *g: (jnp.maximum(step(*g) * per - 1, 0), c))
    nxt = lambda c: (lambda *g: (jnp.minimum((step(*g) + 1) * per, ROWS // HALO - 1), c))
    in_specs = [
        pl.BlockSpec((rows, CONV_W), at(cb0)),
        pl.BlockSpec((rows, CONV_W), at(cb0 + 1)),
        pl.BlockSpec((rows, CONV_W), at(cb0 + 2)),
        pl.BlockSpec((HALO, CONV_W), prev(cb0)),
        pl.BlockSpec((HALO, CONV_W), prev(cb0 + 2)),
        pl.BlockSpec((HALO, CONV_W), nxt(cb0)),
        pl.BlockSpec((HALO, CONV_W), nxt(cb0 + 2)),
        pl.BlockSpec((None, 8, CONV_W), lambda *g: (layer, 0, 0)),
    ]
    return in_specs, pl.BlockSpec((rows, CONV_W), at(0))


def _chan_specs(step, rows):
    in_specs = [
        pl.BlockSpec((rows, FOURIER_W), lambda *g: (step(*g), PC_F // FOURIER_W)),
        pl.BlockSpec((FOURIER_W, 2 * FOURIER_W), lambda *g: (0, 0)),
        pl.BlockSpec((rows, rows), lambda *g: (0, 0)),
    ]
    return in_specs, pl.BlockSpec((2, rows // 2, 2 * FOURIER_W), lambda *g: (0, step(*g), 0))


def _attn_kernel(*refs, n_sub, n_cast, side_rows, step):
    q_ref, k_ref, v_ref = refs[:3]
    n_side = N_CHAN_IN if side_rows else 0
    side_in = refs[3:3 + n_side]
    cast_src = refs[3 + n_side:3 + n_side + n_cast]
    o_ref = refs[3 + n_side + n_cast]
    side_out = refs[4 + n_side + n_cast:len(refs) - n_cast]
    cast_dst = refs[len(refs) - n_cast:]
    for src, dst in zip(cast_src, cast_dst):
        dst[...] = src[...].astype(BF16)
    if side_rows:
        _chan_dft_slice(*side_in, side_out[0])
    k = k_ref[...]
    v = v_ref[...]
    for t in range(n_sub):
        q = q_ref[t * TQ:(t + 1) * TQ, :]
        s = lax.dot_general(q, k, (((1,), (1,)), ((), ())), preferred_element_type=F32)
        e = jnp.exp2(s - jnp.max(s, axis=-1, keepdims=True)).astype(BF16)
        o = jnp.dot(e, v, preferred_element_type=F32)
        o_ref[t * TQ:(t + 1) * TQ, :] = (o[:, 0:V_HEAD] / o[:, V_HEAD:V_HEAD + 1]).astype(BF16)


ATT_SUB = 8


def _attention_lat(q, k, v, p, w_chan, layer, weights, side_total):
    tq = ATT_SUB * TQ
    per_batch = SEQ // tq
    n_steps = BATCH * MLA_HEADS * per_batch
    step = lambda b, h, r: (b * MLA_HEADS + h) * per_batch + r
    side_rows = side_total // n_steps
    side_in, side_out = _chan_specs(step, side_rows)
    rows = [w.shape[1] // n_steps for w in weights]
    return pl.pallas_call(
        functools.partial(_attn_kernel, n_sub=ATT_SUB, n_cast=len(weights), side_rows=side_rows, step=step),
        grid=(BATCH, MLA_HEADS, per_batch),
        in_specs=[
            pl.BlockSpec((None, tq, HEAD_PAD), lambda b, h, r: (h, b * per_batch + r, 0)),
            pl.BlockSpec((None, None, KEYS, HEAD_PAD), lambda b, h, r: (b, h, 0, 0)),
            pl.BlockSpec((None, None, KEYS, 2 * V_HEAD), lambda b, h, r: (b, h, 0, 0)),
        ] + side_in
        + [pl.BlockSpec((None, n, w.shape[2]), lambda b, h, r: (layer, step(b, h, r), 0))
           for n, w in zip(rows, weights)],
        out_specs=[pl.BlockSpec((tq, V_HEAD), lambda b, h, r: (b * per_batch + r, h)), side_out]
        + [pl.BlockSpec((n, w.shape[2]), lambda b, h, r: (step(b, h, r), 0)) for n, w in zip(rows, weights)],
        out_shape=[jax.ShapeDtypeStruct((LAT_ROWS, MLA_HEADS * V_HEAD), BF16),
                   jax.ShapeDtypeStruct((2, side_total // 2, 2 * FOURIER_W), BF16)]
        + [jax.ShapeDtypeStruct(w.shape[1:], BF16) for w in weights],
        compiler_params=pltpu.CompilerParams(dimension_semantics=("arbitrary", "arbitrary", "arbitrary")),
        name="attention_lat",
    )(q, k, v, p, w_chan, _even_odd_rows(side_rows), *weights)


def _attention_ctx(q, k, v):
    return pl.pallas_call(
        functools.partial(_attn_kernel, n_sub=1, n_cast=0, side_rows=0, step=None),
        grid=(BATCH, MLA_HEADS),
        in_specs=[
            pl.BlockSpec((None, CTX_LEN, HEAD_PAD), lambda b, h: (h, LAT_ROWS // CTX_LEN + b, 0)),
            pl.BlockSpec((None, None, CTX_LEN, HEAD_PAD), lambda b, h: (b, h, SEQ // CTX_LEN, 0)),
            pl.BlockSpec((None, None, CTX_LEN, 2 * V_HEAD), lambda b, h: (b, h, SEQ // CTX_LEN, 0)),
        ],
        out_specs=pl.BlockSpec((CTX_LEN, V_HEAD), lambda b, h: (b, h)),
        out_shape=jax.ShapeDtypeStruct((CTX_ROWS, MLA_HEADS * V_HEAD), BF16),
        name="attention_ctx",
    )(q, k, v)


def _pos_dft_kernel(de_ref, do_ref, uv0_ref, uv1_ref, *refs, conv_rows):
    conv_in = refs[:N_CONV_IN] if conv_rows else ()
    o_ref = refs[len(conv_in)]
    acc_ref = refs[-1]
    k = pl.program_id(1)
    if conv_rows:
        row0 = (pl.program_id(0) * pl.num_programs(1) + k) * conv_rows
        _conv_slice(row0, *conv_in, refs[len(conv_in) + 1])

    @pl.when(k == 0)
    def _():
        acc_ref[...] = jnp.zeros_like(acc_ref)

    de = de_ref[...]
    do = do_ref[...]
    for b, uv_ref in enumerate((uv0_ref, uv1_ref)):
        acc_ref[b, 0] += jnp.dot(de, uv_ref[:, 0:FOURIER_W], preferred_element_type=F32)
        acc_ref[b, 1] += jnp.dot(do, uv_ref[:, FOURIER_W:2 * FOURIER_W], preferred_element_type=F32)

    @pl.when(k == pl.num_programs(1) - 1)
    def _():
        for b in range(BATCH):
            e, o = acc_ref[b, 0], acc_ref[b, 1]
            o_ref[b, 0] = (e + o).astype(BF16)
            o_ref[b, 1] = (e - o).astype(BF16)


def _pos_dft(dft, uv_pairs, length, row0, tm, tk, conv=None):
    half = length // 2
    nk_half = half // tk
    kb0 = row0 // 2 // tk
    grid = (half // tm, 2 * nk_half)

    def uv_spec(b):
        return pl.BlockSpec((None, tk, 2 * FOURIER_W),
                            lambda m, k: (k // nk_half, kb0 + b * nk_half + k % nk_half, 0))

    conv_in, conv_out, conv_args, conv_shape, conv_rows = [], [], [], [], 0
    if conv is not None:
        p, conv_w, layer, rows = conv
        conv_rows = rows // (grid[0] * grid[1])
        conv_in, spec = _conv_specs(lambda m, k: m * grid[1] + k, conv_rows, layer)
        conv_out, conv_args = [spec], [p] * 7 + [conv_w]
        conv_shape = [jax.ShapeDtypeStruct((rows, CONV_W), BF16)]
    out = pl.pallas_call(
        functools.partial(_pos_dft_kernel, conv_rows=conv_rows),
        grid=grid,
        in_specs=[pl.BlockSpec((tm, tk), lambda m, k: (m, k)),
                  pl.BlockSpec((tm, tk), lambda m, k: (m, 2 * nk_half + k)),
                  uv_spec(0), uv_spec(1)] + conv_in,
        out_specs=[pl.BlockSpec((BATCH, 2, tm, FOURIER_W), lambda m, k: (0, 0, m, 0))] + conv_out,
        out_shape=[jax.ShapeDtypeStruct((BATCH, 2, half, FOURIER_W), BF16)] + conv_shape,
        scratch_shapes=[pltpu.VMEM((BATCH, 2, tm, FOURIER_W), F32)],
        compiler_params=pltpu.CompilerParams(dimension_semantics=("arbitrary", "arbitrary")),
        name="pos_dft_%d" % length,
    )(dft, dft, uv_pairs, uv_pairs, *conv_args)
    return [out[0].reshape(BATCH * length, FOURIER_W)] + list(out[1:])


MIX_CHUNK = 512


def _mixout_kernel(fl_ref, fc_ref, al_ref, ac_ref, c_ref, g0_ref, g1_ref, g2_ref, x_ref, gate_ref,
                   wf_ref, wm_ref, wc_ref, wo_ref, o_ref, m_ref):
    is_ctx = pl.program_id(0) == LAT_TILES
    f = jnp.where(is_ctx, fc_ref[...], fl_ref[...])
    a = jnp.where(is_ctx, ac_ref[...], al_ref[...])
    c = c_ref[...]
    for n in range(D_MODEL // MIX_CHUNK):
        cols = pl.ds(n * MIX_CHUNK, MIX_CHUNK)
        y = (g0_ref[:, cols].astype(F32) * jnp.dot(f, wf_ref[:, cols], preferred_element_type=F32)
             + g1_ref[:, cols].astype(F32) * jnp.dot(a, wm_ref[:, cols], preferred_element_type=F32)
             + g2_ref[:, cols].astype(F32) * jnp.dot(c, wc_ref[:, cols], preferred_element_type=F32))
        m_ref[:, cols] = y.astype(BF16)
    m = m_ref[...]
    for n in range(D_MODEL // MIX_CHUNK):
        cols = pl.ds(n * MIX_CHUNK, MIX_CHUNK)
        acc = jnp.dot(m, wo_ref[:, cols], preferred_element_type=F32)
        o_ref[:, cols] = x_ref[:, cols] + gate_ref[:, cols] * acc


def _mixout(f_lat, f_ctx, a_lat, a_ctx, cmix, p, x_all, mod, layer, w_f, w_m, w_c, w_out, n_tiles):
    gb = PC_G // D_MODEL
    lat_or_last = lambda i: (jnp.minimum(i, LAT_TILES - 1), 0)
    resident = lambda rows: pl.BlockSpec((rows, D_MODEL), lambda i: (0, 0), pipeline_mode=pl.Buffered(1))
    return pl.pallas_call(
        _mixout_kernel,
        grid=(n_tiles,),
        in_specs=[
            pl.BlockSpec((TM, FOURIER_W), lat_or_last),
            pl.BlockSpec((TM, FOURIER_W), lambda i: (0, 0)),
            pl.BlockSpec((TM, MLA_HEADS * V_HEAD), lat_or_last),
            pl.BlockSpec((TM, MLA_HEADS * V_HEAD), lambda i: (0, 0)),
            pl.BlockSpec((TM, CONV_W), lambda i: (i, 0)),
            pl.BlockSpec((TM, D_MODEL), lambda i: (i, gb)),
            pl.BlockSpec((TM, D_MODEL), lambda i: (i, gb + 1)),
            pl.BlockSpec((TM, D_MODEL), lambda i: (i, gb + 2)),
            pl.BlockSpec((TM, D_MODEL), lambda i: (i, 0)),
            _mod_spec(layer, 2),
            resident(FOURIER_W),
            resident(MLA_HEADS * V_HEAD),
            resident(CONV_W),
            resident(D_MODEL),
        ],
        out_specs=pl.BlockSpec((TM, D_MODEL), lambda i: (i, 0)),
        out_shape=jax.ShapeDtypeStruct((n_tiles * TM, D_MODEL), F32),
        scratch_shapes=[pltpu.VMEM((TM, D_MODEL), BF16)],
        name="mixout",
    )(f_lat, f_ctx, a_lat, a_ctx, cmix, p, p, p, x_all, mod, w_f, w_m, w_c, w_out)


FF_TN = 512
FF_TM_LAST = 1024


def _ffn_kernel(x_ref, gain_ref, shift_ref, scale_ref, gate_ref, wg_ref, wu_ref, wd_ref, *refs):
    with_next = len(refs) > 2
    nxt = refs[:3] if with_next else None
    o_ref = refs[3] if with_next else refs[0]
    hn_ref = refs[4] if with_next else None
    h_ref = refs[-1]
    f = pl.program_id(1)

    @pl.when(f == 0)
    def _():
        y = _rms(x_ref[...]) * gain_ref[...]
        h_ref[...] = (y * (1.0 + scale_ref[...]) + shift_ref[...]).astype(BF16)
        o_ref[...] = jnp.zeros_like(o_ref)

    h = h_ref[...]
    g = jnp.dot(h, wg_ref[...], preferred_element_type=F32)
    u = jnp.dot(h, wu_ref[...], preferred_element_type=F32)
    a = (g * _sigmoid(g) * u).astype(BF16)
    o_ref[...] += jnp.dot(a, wd_ref[...], preferred_element_type=F32)

    @pl.when(f == pl.num_programs(1) - 1)
    def _():
        x_new = x_ref[...] + gate_ref[...] * o_ref[...]
        o_ref[...] = x_new
        if with_next:
            gain_n, shift_n, scale_n = nxt
            y = _rms(x_new) * gain_n[...]
            hn_ref[...] = (y * (1.0 + scale_n[...]) + shift_n[...]).astype(BF16)


def _ffn_vmem_bytes(tm, with_next):
    windows = 2 * (2 * tm * D_MODEL * 4 + with_next * tm * D_MODEL * 2 + 3 * D_MODEL * FF_TN * 2)
    temporaries = 3 * tm * FF_TN * 4 + tm * D_MODEL * 4
    return windows + tm * D_MODEL * 2 + temporaries


def _ffn(x_all, gain, mod, layer, w_gate, w_up, w_down, rows, tm, next_gain=None):
    tile = pl.BlockSpec((tm, D_MODEL), lambda i, f: (i, 0))
    with_next = next_gain is not None

    def mod_spec(lyr, which):
        base = lyr * MOD_ROWS * N_MOD + which
        return pl.BlockSpec((None, 1, D_MODEL), lambda i, f: (base + (i // (SEQ // tm)) * N_MOD, 0, 0))

    next_specs = [pl.BlockSpec((None, 1, D_MODEL), lambda i, f: (layer + 1, 0, 0)),
                  mod_spec(layer + 1, 0), mod_spec(layer + 1, 1)] if with_next else []
    return pl.pallas_call(
        _ffn_kernel,
        grid=(rows // tm, D_FF // FF_TN),
        in_specs=[
            tile,
            pl.BlockSpec((None, 1, D_MODEL), lambda i, f: (layer, 0, 0)),
            mod_spec(layer, 3),
            mod_spec(layer, 4),
            mod_spec(layer, 5),
            pl.BlockSpec((D_MODEL, FF_TN), lambda i, f: (0, f)),
            pl.BlockSpec((D_MODEL, FF_TN), lambda i, f: (0, f)),
            pl.BlockSpec((FF_TN, D_MODEL), lambda i, f: (f, 0)),
        ] + next_specs,
        out_specs=[tile] + [tile] * with_next,
        out_shape=[jax.ShapeDtypeStruct((rows, D_MODEL), F32)]
        + [jax.ShapeDtypeStruct((rows, D_MODEL), BF16)] * with_next,
        scratch_shapes=[pltpu.VMEM((tm, D_MODEL), BF16)],
        compiler_params=pltpu.CompilerParams(dimension_semantics=("parallel", "arbitrary"),
                                             vmem_limit_bytes=_ffn_vmem_bytes(tm, with_next)),
        name="ffn",
    )(x_all, gain, mod, mod, mod, w_gate, w_up, w_down, *([next_gain, mod, mod] if with_next else []))


def _dft_cos_sin(length):
    kn = np.outer(np.arange(length), np.arange(length)) % length
    ang = 2.0 * np.pi * kn / length
    return np.cos(ang), np.sin(ang)


def _chan_dft_matrix():
    c, s = _dft_cos_sin(FOURIER_GROUP_W)
    eye = np.eye(FOURIER_GROUPS)
    return jnp.asarray(np.concatenate([np.kron(eye, c), np.kron(eye, s)], axis=1), F32).astype(BF16)


def _dft_column_order(length):
    ev, od = np.arange(0, length, 2), np.arange(1, length, 2)
    return np.concatenate([ev, length + ev, od, length + od])


def _ctx_dft_matrix():
    c, s = _dft_cos_sin(CTX_LEN)
    norm = (CTX_LEN * FOURIER_GROUP_W) ** -0.5
    full = np.concatenate([c, -s], axis=1) * norm
    return jnp.asarray(full[:CTX_LEN // 2][:, _dft_column_order(CTX_LEN)], F32).astype(BF16)


def _lat_dft_matrix():
    r = 64
    n = np.arange(SEQ)
    hi = 2.0 * np.pi * (np.outer(np.arange(SEQ // 2 // r), n) % r) / r
    lo = 2.0 * np.pi * (np.outer(np.arange(r), n) % SEQ) / SEQ
    norm = (SEQ * FOURIER_GROUP_W) ** -0.5
    order = _dft_column_order(SEQ)
    tab = lambda c_part, s_part: jnp.asarray(np.concatenate([c_part, s_part], axis=1)[:, order], F32)
    hi_c = tab(np.cos(hi) * norm, np.cos(hi) * norm)[:, None, :]
    hi_s = tab(np.sin(hi) * norm, np.sin(hi) * norm)[:, None, :]
    lo_a = tab(np.cos(lo), -np.sin(lo))[None, :, :]
    lo_b = tab(-np.sin(lo), -np.cos(lo))[None, :, :]
    return (hi_c * lo_a + hi_s * lo_b).astype(BF16).reshape(SEQ // 2, 2 * SEQ)


def _rope_tables():
    rows = SEQ // GRID_W
    row = jnp.repeat(jnp.arange(rows), GRID_W)
    col = jnp.tile(jnp.arange(GRID_W), rows)
    inv_freq = ROPE_THETA ** (-jnp.arange(AXIS_PAIRS, dtype=F32) / AXIS_PAIRS)
    ang = jnp.concatenate([row[:, None] * inv_freq, col[:, None] * inv_freq], axis=-1)
    cos, sin = jnp.cos(ang), jnp.sin(ang)
    zeros = jnp.zeros((SEQ, QK_NOPE - QK_ROPE), F32)
    ca = jnp.concatenate([cos, cos, zeros], axis=-1)
    sb = jnp.concatenate([-sin, sin, zeros], axis=-1)
    ca = jnp.concatenate([ca, ca, jnp.ones((CTX_ROWS, QK_NOPE), F32)], axis=0)
    sb = jnp.concatenate([sb, sb, jnp.zeros((CTX_ROWS, QK_NOPE), F32)], axis=0)
    return ca, sb


def _pack_w_uq(w_uq):
    w = w_uq.reshape(DEPTH, Q_RANK, MLA_HEADS, QK_HEAD)
    ra, rb = _rope_split(w[..., QK_NOPE:])
    return jnp.concatenate([w[..., :QK_NOPE], ra, rb], axis=-1).reshape(DEPTH, Q_RANK, MLA_HEADS * Q_EXT).astype(BF16)


def _pack_head_gain(g):
    ga, gb = _rope_split(g[:, QK_NOPE:])
    rows = jnp.stack([g[:, :QK_NOPE], ga, gb], axis=1)
    return jnp.concatenate([rows, jnp.zeros((DEPTH, 5, QK_NOPE), F32)], axis=1)


def kernel(x, c, ctx, c_ctx, w_ada, b_ada, norm_mix, norm_ffn, w_in, b_gate, q_a_norm, kv_a_norm, w_uq, w_ukv,
           q_norm, k_norm, w_f_out, w_mla_out, conv_w, w_conv_out, w_out, w_ffn_gate, w_ffn_up, w_ffn_down):
    cond = jnp.concatenate([c, c_ctx[None, :], jnp.zeros((MOD_ROWS - BATCH - 1, D_MODEL), F32)], axis=0)
    bias_pack = jnp.concatenate([b_gate[:, None, :], jnp.zeros((DEPTH, 1, P_COLS - PC_F), F32)], axis=-1)
    wq_ext = _pack_w_uq(w_uq)
    gq = _pack_head_gain(q_norm)
    gk = _pack_head_gain(k_norm)
    conv_w8 = jnp.concatenate([conv_w, jnp.zeros((DEPTH, 5, CONV_W), F32)], axis=1)
    rope_ca, rope_sb = _rope_tables()
    w_chan = _chan_dft_matrix()
    dft_lat = _lat_dft_matrix()
    dft_ctx = _ctx_dft_matrix()
    w_ukv_b = w_ukv.astype(BF16)
    late_weights = [w_f_out, w_mla_out, w_conv_out, w_out, w_ffn_gate, w_ffn_up, w_ffn_down]
    gain_mix, gain_ffn = norm_mix[:, None, :], norm_ffn[:, None, :]
    qa, kva = q_a_norm[:, None, :], kv_a_norm[:, None, :]

    w_in_t = jnp.swapaxes(w_in, 1, 2)
    mod = _ada(cond, w_ada, b_ada).reshape(DEPTH * MOD_ROWS * N_MOD, 1, D_MODEL)

    h, x_all = _prenorm(x.reshape(LAT_ROWS, D_MODEL), ctx.reshape(CTX_ROWS, D_MODEL), gain_mix, mod)
    for l in range(DEPTH):
        last = l == DEPTH - 1
        n_tiles = LAT_TILES if last else N_TILES
        p = _inproj(h, l, w_in_t, bias_pack)
        q, k, v = _qkv(p, l, wq_ext, w_ukv_b, qa, kva, gq, gk, rope_ca, rope_sb)
        a_lat, uv, w_f_b, w_m_b, w_c_b, w_out_b, w_g_b, w_u_b, w_d_b = _attention_lat(
            q, k, v, p, w_chan, l, late_weights, n_tiles * TM)
        a_ctx = a_lat if last else _attention_ctx(q, k, v)
        f_lat, cmix = _pos_dft(dft_lat, uv, SEQ, 0, 512, SEQ // 2, conv=(p, conv_w8, l, n_tiles * TM))
        f_ctx = f_lat if last else _pos_dft(dft_ctx, uv, CTX_LEN, LAT_ROWS, CTX_LEN // 2, CTX_LEN // 2)[0]
        x_all = _mixout(f_lat, f_ctx, a_lat, a_ctx, cmix, p, x_all, mod, l, w_f_b, w_m_b, w_c_b, w_out_b, n_tiles)
        x_all, *h_next = _ffn(x_all, gain_ffn, mod, l, w_g_b, w_u_b, w_d_b, n_tiles * TM,
                              FF_TM_LAST if last else TM, next_gain=None if last else gain_mix)
        h = h_next[0] if h_next else None
    return x_all.reshape(BATCH, SEQ, D_MODEL)
```
